```python
import math
import jax, jax.numpy as jnp
from jax import lax
import numpy as np

D_MODEL = 2048
BATCH = 1
SEQ = 16384
DEPTH = 2
DEC_BATCH = 8
DEC_SEQ = 2048
PAST_LEN = 128

MIX_WIDTH = D_MODEL
ROPE_THETA = 500000.0
RET_THETA = 10000.0
Q_BLOCK = 128
RET_CHUNK = 128
NORM_EPS = 1e-5
A_HEADS = 4
A_DK = 64
A_DV = 2 * A_DK
A_ROT = A_DK // 4
B_HEADS = 6
B_Q_RANK = 512
B_KV_RANK = 256
B_NOPE = 128
B_ROPE = 64
B_DV = 128
C_HEADS = 6
C_DK = 64
C_DV = 128
D_FF = 5632
CONV_WIDTH = 3
IN_SPLIT_SIZES = (A_HEADS * 2 * A_DK, A_HEADS * 2 * A_DK, A_HEADS * A_DV,
                  B_Q_RANK, B_KV_RANK, B_ROPE,
                  C_HEADS * C_DK, C_HEADS * C_DK, C_HEADS * C_DV, C_HEADS * C_DV)
IN_COLS = sum(IN_SPLIT_SIZES)

kernel_name = "hymba_diffattn_mla_retention_encoder"


def rms_norm(x, g, eps=NORM_EPS):
    xf = x.astype(jnp.float32)
    y = xf * lax.rsqrt(jnp.mean(xf * xf, axis=-1, keepdims=True) + eps)
    return (y * g.astype(jnp.float32)).astype(x.dtype)


def rope(x, pos, theta, n_rot):
    half = n_rot // 2
    inv = jnp.power(jnp.float32(theta), -jnp.arange(half, dtype=jnp.float32) * 2.0 / n_rot)
    ang = pos[:, None] * inv[None, :]
    shape = (pos.shape[0],) + (1,) * (x.ndim - 3) + (half,)
    cos = jnp.cos(ang).reshape(shape).astype(x.dtype)
    sin = jnp.sin(ang).reshape(shape).astype(x.dtype)
    x1 = x[..., :half]
    x2 = x[..., half:n_rot]
    return jnp.concatenate([x1 * cos - x2 * sin, x2 * cos + x1 * sin, x[..., n_rot:]], axis=-1)


def blocked_attention(q, k, v, scale, lam=None):
    B, S, H, M, Dk = q.shape
    Dv = v.shape[-1]
    nb = S // Q_BLOCK
    qb = q.reshape(B, nb, Q_BLOCK, H, M, Dk).transpose(1, 0, 2, 3, 4, 5)

    def one_block(q_blk):
        s = jnp.einsum('bqhmd,bkhmd->bhmqk', q_blk, k).astype(jnp.float32) * scale
        p = jax.nn.softmax(s, axis=-1)
        w = p[:, :, 0] if lam is None else p[:, :, 0] - lam * p[:, :, 1]
        return jnp.einsum('bhqk,bkhd->bqhd', w.astype(v.dtype), v)

    o = lax.map(one_block, qb)
    return o.transpose(1, 0, 2, 3, 4).reshape(B, S, H, Dv)


def retention_one_direction(q, k, v, log_gamma, strict):
    B, S, H, Dk = q.shape
    Dv = v.shape[-1]
    C = RET_CHUNK
    N = S // C
    qc = q.reshape(B, N, C, H, Dk)
    kc = k.reshape(B, N, C, H, Dk)
    vc = v.reshape(B, N, C, H, Dv)
    idx = jnp.arange(C, dtype=jnp.float32)
    dist = idx[:, None] - idx[None, :]
    mask = dist > 0 if strict else dist >= 0
    dmat = jnp.where(mask[None], jnp.exp(log_gamma[:, None, None] * jnp.where(mask, dist, 0.0)[None]), 0.0)
    s = jnp.einsum('bnihd,bnjhd->bnhij', qc, kc) * dmat[None, None]
    inner = jnp.einsum('bnhij,bnjhe->bnihe', s, vc)
    k_dec = jnp.exp(log_gamma[None, :] * (C - 1.0 - idx)[:, None])
    q_dec = jnp.exp(log_gamma[None, :] * (idx + 1.0)[:, None])
    chunk_decay = jnp.exp(log_gamma * C)
    U = jnp.einsum('bnjhd,bnjhe->nbhde', kc * k_dec[:, :, None], vc)

    def step(R, U_n):
        return R * chunk_decay[None, :, None, None] + U_n, R

    _, R_prev = lax.scan(step, jnp.zeros((B, H, Dk, Dv), jnp.float32), U)
    cross = jnp.einsum('bnihd,nbhde->bnihe', qc * q_dec[:, :, None], R_prev)
    return (inner + cross).reshape(B, S, H, Dv)


def head_group_mixer(h, l, w_in, diff_lambda, diff_norm, mla_q_norm, mla_kv_norm, w_uq, w_ukv,
                     ret_decay_fwd, ret_decay_bwd, ret_norm, w_o):
    B, S, _ = h.shape
    pos = jnp.arange(S, dtype=jnp.float32)
    z = h @ w_in
    cuts = [int(c) for c in np.cumsum(IN_SPLIT_SIZES)[:-1]]
    za_q, za_k, za_v, zb_cq, zb_ckv, zb_kr, zc_q, zc_k, zc_v, zc_g = jnp.split(z, cuts, axis=-1)

    lam_init = 0.8 - 0.6 * math.exp(-0.3 * l)
    lp = diff_lambda.astype(jnp.float32)
    lam = jnp.exp(jnp.sum(lp[0] * lp[1])) - jnp.exp(jnp.sum(lp[2] * lp[3])) + lam_init
    qa = rope(za_q.reshape(B, S, A_HEADS, 2, A_DK), pos, ROPE_THETA, A_ROT)
    ka = rope(za_k.reshape(B, S, A_HEADS, 2, A_DK), pos, ROPE_THETA, A_ROT)
    va = za_v.reshape(B, S, A_HEADS, A_DV)
    oa = blocked_attention(qa, ka, va, A_DK ** -0.5, lam)
    oa = (rms_norm(oa, diff_norm) * (1.0 - lam_init)).reshape(B, S, A_HEADS * A_DV)

    cq = rms_norm(zb_cq, mla_q_norm)
    qb = (cq @ w_uq).reshape(B, S, B_HEADS, B_NOPE + B_ROPE)
    qb = jnp.concatenate([qb[..., :B_NOPE], rope(qb[..., B_NOPE:], pos, ROPE_THETA, B_ROPE)], axis=-1)
    ckv = rms_norm(zb_ckv, mla_kv_norm)
    kv = (ckv @ w_ukv).reshape(B, S, B_HEADS, B_NOPE + B_DV)
    k_rope = rope(zb_kr[:, :, None, :], pos, ROPE_THETA, B_ROPE)
    kb = jnp.concatenate([kv[..., :B_NOPE], jnp.broadcast_to(k_rope, (B, S, B_HEADS, B_ROPE))], axis=-1)
    vb = kv[..., B_NOPE:]
    ob = blocked_attention(qb[:, :, :, None, :], kb[:, :, :, None, :], vb, (B_NOPE + B_ROPE) ** -0.5)
    ob = ob.reshape(B, S, B_HEADS * B_DV)

    qc = rope(zc_q.reshape(B, S, C_HEADS, C_DK), pos, RET_THETA, C_DK).astype(jnp.float32)
    kc = (rope(zc_k.reshape(B, S, C_HEADS, C_DK), pos, RET_THETA, C_DK) * (C_DK ** -0.5)).astype(jnp.float32)
    vc = zc_v.reshape(B, S, C_HEADS, C_DV).astype(jnp.float32)
    lg_f = jax.nn.log_sigmoid(ret_decay_fwd.astype(jnp.float32))
    lg_b = jax.nn.log_sigmoid(ret_decay_bwd.astype(jnp.float32))
    ret = retention_one_direction(qc, kc, vc, lg_f, False) + \
        retention_one_direction(qc[:, ::-1], kc[:, ::-1], vc[:, ::-1], lg_b, True)[:, ::-1]
    oc = rms_norm(ret, ret_norm).astype(h.dtype).reshape(B, S, C_HEADS * C_DV)
    oc = jax.nn.silu(zc_g) * oc

    return jnp.concatenate([oa, ob, oc], axis=-1) @ w_o


def conv_gated_mlp(h, w_gate, w_up, conv_w, conv_b, w_down):
    S = h.shape[1]
    g = h @ w_gate
    u = h @ w_up
    pad = CONV_WIDTH // 2
    gp = jnp.pad(g, ((0, 0), (pad, pad), (0, 0)))
    gc = conv_b
    for t in range(CONV_WIDTH):
        gc = gc + gp[:, t:t + S] * conv_w[t]
    return (jax.nn.silu(gc) * u) @ w_down


def setup_inputs(seed: int = 0) -> dict:
    key = jax.random.key(seed)
    ks = jax.random.split(key, 24)
    f32 = jnp.float32

    def nrm(k, shape, scale):
        return jax.random.normal(k, shape, f32) * scale

    decay_base = jnp.log(jnp.power(2.0, 5.0 + jnp.arange(C_HEADS, dtype=f32)) - 1.0)
    return {
        "x_prompt": nrm(ks[0], (BATCH, SEQ, D_MODEL), 1.0),
        "x_sample": nrm(ks[1], (DEC_BATCH, DEC_SEQ, D_MODEL), 1.0),
        "norm_mix": 1.0 + nrm(ks[2], (DEPTH, D_MODEL), 0.02),
        "w_in": nrm(ks[3], (DEPTH, D_MODEL, IN_COLS), D_MODEL ** -0.5),
        "diff_lambda": nrm(ks[4], (DEPTH, 4, A_DK), 0.1),
        "diff_norm": 1.0 + nrm(ks[5], (DEPTH, A_DV), 0.02),
        "mla_q_norm": 1.0 + nrm(ks[6], (DEPTH, B_Q_RANK), 0.02),
        "mla_kv_norm": 1.0 + nrm(ks[7], (DEPTH, B_KV_RANK), 0.02),
        "w_uq": nrm(ks[8], (DEPTH, B_Q_RANK, B_HEADS * (B_NOPE + B_ROPE)), B_Q_RANK ** -0.5),
        "w_ukv": nrm(ks[9], (DEPTH, B_KV_RANK, B_HEADS * (B_NOPE + B_DV)), B_KV_RANK ** -0.5),
        "ret_decay_fwd": decay_base[None, :] + nrm(ks[10], (DEPTH, C_HEADS), 0.1),
        "ret_decay_bwd": decay_base[None, :] + nrm(ks[11], (DEPTH, C_HEADS), 0.1),
        "ret_norm": 1.0 + nrm(ks[12], (DEPTH, C_DV), 0.02),
        "w_o": nrm(ks[13], (DEPTH, MIX_WIDTH, D_MODEL), MIX_WIDTH ** -0.5),
        "norm_ffn": 1.0 + nrm(ks[14], (DEPTH, D_MODEL), 0.02),
        "w_gate": nrm(ks[15], (DEPTH, D_MODEL, D_FF), D_MODEL ** -0.5),
        "w_up": nrm(ks[16], (DEPTH, D_MODEL, D_FF), D_MODEL ** -0.5),
        "conv_w": nrm(ks[17], (DEPTH, CONV_WIDTH, D_FF), CONV_WIDTH ** -0.5),
        "conv_b": nrm(ks[18], (DEPTH, D_FF), 0.01),
        "w_down": nrm(ks[19], (DEPTH, D_FF, D_MODEL), D_FF ** -0.5),
        "norm_final": 1.0 + nrm(ks[20], (D_MODEL,), 0.02),
    }


def reference(x_prompt, x_sample, norm_mix, w_in, diff_lambda, diff_norm, mla_q_norm, mla_kv_norm,
              w_uq, w_ukv, ret_decay_fwd, ret_decay_bwd, ret_norm, w_o, norm_ffn, w_gate, w_up,
              conv_w, conv_b, w_down, norm_final):
    def trunk(x):
        for l in range(DEPTH):
            h = rms_norm(x, norm_mix[l])
            x = x + head_group_mixer(h, l, w_in[l], diff_lambda[l], diff_norm[l], mla_q_norm[l],
                                     mla_kv_norm[l], w_uq[l], w_ukv[l], ret_decay_fwd[l],
                                     ret_decay_bwd[l], ret_norm[l], w_o[l])
            h = rms_norm(x, norm_ffn[l])
            x = x + conv_gated_mlp(h, w_gate[l], w_up[l], conv_w[l], conv_b[l], w_down[l])
        return rms_norm(x, norm_final)

    y_prompt = trunk(x_prompt)
    y_sample = trunk(x_sample)
    return (y_prompt, y_sample)
```

```python
import functools
import math

import jax
import jax.numpy as jnp
from jax import lax
from jax.experimental import pallas as pl
from jax.experimental.pallas import tpu as pltpu

F32 = jnp.float32
BF16 = jnp.bfloat16

NORM_EPS = 1e-5
ROPE_THETA = 500000.0
RET_THETA = 10000.0
RET_CHUNK = 128
A_HEADS, A_DK, A_DV = 4, 64, 128
A_ROT = A_DK // 4
B_HEADS, B_Q_RANK, B_KV_RANK, B_NOPE, B_ROPE, B_DV = 6, 512, 256, 128, 64, 128
C_HEADS, C_DK, C_DV = 6, 64, 128
CONV_WIDTH = 3

LANES = 128
SUBLANES = 8
ZA_COLS = 3 * A_HEADS * 2 * A_DK
ZB_COLS = B_Q_RANK + B_KV_RANK + 2 * B_ROPE
ZC_COLS = 2 * C_HEADS * C_DK + 2 * C_HEADS * C_DV
B_QK = 2 * LANES
VMEM_LIMIT = 56 * 1024 * 1024


def _cparams(*sem):
    return pltpu.CompilerParams(dimension_semantics=sem, vmem_limit_bytes=VMEM_LIMIT)


def _rms(x, g):
    return x * lax.rsqrt(jnp.mean(x * x, axis=-1, keepdims=True) + NORM_EPS) * g


def _rope_lanes(x, cos, sin_a, sin_b, half):
    return (x * cos + pltpu.roll(x, LANES - half, 1) * sin_a + pltpu.roll(x, half, 1) * sin_b)


def _in_proj_kernel(x_ref, g_ref, w_ref, za_ref, zb_ref, zc_ref):
    h = _rms(x_ref[...], g_ref[...]).astype(BF16)
    za_ref[...] = jnp.dot(h, w_ref[:, :ZA_COLS], preferred_element_type=F32)
    zb_ref[...] = jnp.dot(h, w_ref[:, ZA_COLS:ZA_COLS + ZB_COLS], preferred_element_type=F32)
    zc_ref[...] = jnp.dot(h, w_ref[:, ZA_COLS + ZB_COLS:], preferred_element_type=F32)


def _in_proj(x, g, w, tm):
    T, D = x.shape
    n = w.shape[1]
    return pl.pallas_call(
        _in_proj_kernel,
        grid=(T // tm,),
        in_specs=[
            pl.BlockSpec((tm, D), lambda i: (i, 0)),
            pl.BlockSpec((1, D), lambda i: (0, 0)),
            pl.BlockSpec((D, n), lambda i: (0, 0), pipeline_mode=pl.Buffered(1)),
        ],
        out_specs=[
            pl.BlockSpec((tm, ZA_COLS), lambda i: (i, 0)),
            pl.BlockSpec((tm, ZB_COLS), lambda i: (i, 0)),
            pl.BlockSpec((tm, ZC_COLS), lambda i: (i, 0)),
        ],
        out_shape=[
            jax.ShapeDtypeStruct((T, ZA_COLS), F32),
            jax.ShapeDtypeStruct((T, ZB_COLS), F32),
            jax.ShapeDtypeStruct((T, ZC_COLS), F32),
        ],
        compiler_params=_cparams("parallel"),
        name="in_proj",
    )(x, g, w)


def _prep_a_kernel(z_ref, cos_ref, sa_ref, sb_ref, qt_ref, k_ref, vt_ref):
    cos, sa, sb = cos_ref[...], sa_ref[...], sb_ref[...]
    tm = z_ref.shape[0]
    row = lax.broadcasted_iota(jnp.int32, (LANES, tm), 0)
    hw = 2 * A_DK
    for h in range(A_HEADS):
        q = _rope_lanes(z_ref[:, h * hw:(h + 1) * hw], cos, sa, sb, A_ROT // 2) * (A_DK ** -0.5)
        qt = q.T
        qt_ref[h, 0] = jnp.where(row < A_DK, qt, 0.0).astype(BF16)
        qt_ref[h, 1] = jnp.where(row >= A_DK, qt, 0.0).astype(BF16)
        k = _rope_lanes(z_ref[:, (A_HEADS + h) * hw:(A_HEADS + h + 1) * hw], cos, sa, sb, A_ROT // 2)
        k_ref[:, h * hw:(h + 1) * hw] = k.astype(BF16)
        v = z_ref[:, (2 * A_HEADS + h) * hw:(2 * A_HEADS + h + 1) * hw]
        vt_ref[h] = v.T.astype(BF16)


def _prep_a(za, tabs, B, S, tm):
    nb = S // tm
    row_map = lambda b, i: (b * nb + i, 0)
    tab_spec = pl.BlockSpec((tm, LANES), lambda b, i: (i, 0))
    return pl.pallas_call(
        _prep_a_kernel,
        grid=(B, nb),
        in_specs=[pl.BlockSpec((tm, ZA_COLS), row_map), tab_spec, tab_spec, tab_spec],
        out_specs=[
            pl.BlockSpec((None, A_HEADS, 2, LANES, tm), lambda b, i: (b, 0, 0, 0, i)),
            pl.BlockSpec((None, tm, A_HEADS * LANES), lambda b, i: (b, i, 0)),
            pl.BlockSpec((None, A_HEADS, A_DV, tm), lambda b, i: (b, 0, 0, i)),
        ],
        out_shape=[
            jax.ShapeDtypeStruct((B, A_HEADS, 2, LANES, S), BF16),
            jax.ShapeDtypeStruct((B, S, A_HEADS * LANES), BF16),
            jax.ShapeDtypeStruct((B, A_HEADS, A_DV, S), BF16),
        ],
        compiler_params=_cparams("parallel", "parallel"),
        name="prep_a",
    )(za, *tabs)


def _prep_b_kernel(z_ref, qn_ref, kvn_ref, wuq_ref, wukv_ref, cos_ref, sa_ref, sb_ref,
                   qt_ref, k_ref, vt_ref):
    cos, sa, sb = cos_ref[...], sa_ref[...], sb_ref[...]
    tm = z_ref.shape[0]
    half = B_ROPE // 2
    scale = (B_NOPE + B_ROPE) ** -0.5
    lane = lax.broadcasted_iota(jnp.int32, (tm, LANES), 1)
    nope_w = B_HEADS * B_NOPE

    cq = _rms(z_ref[:, :B_Q_RANK], qn_ref[...]).astype(BF16)
    q = jnp.dot(cq, wuq_ref[...], preferred_element_type=F32) * scale
    ckv = _rms(z_ref[:, B_Q_RANK:B_Q_RANK + B_KV_RANK], kvn_ref[...]).astype(BF16)
    kv = jnp.dot(ckv, wukv_ref[...], preferred_element_type=F32)
    kr = _rope_lanes(z_ref[:, B_Q_RANK + B_KV_RANK:ZB_COLS], cos, sa, sb, half).astype(BF16)

    for c in range(B_HEADS // 2):
        qr = _rope_lanes(q[:, nope_w + c * LANES:nope_w + (c + 1) * LANES], cos, sa, sb, half)
        for e in range(2):
            h = 2 * c + e
            tail = qr if e == 0 else pltpu.roll(qr, B_ROPE, 1)
            tail = jnp.where(lane < B_ROPE, tail, 0.0)
            qt_ref[h, :LANES] = q[:, h * B_NOPE:(h + 1) * B_NOPE].T.astype(BF16)
            qt_ref[h, LANES:] = tail.T.astype(BF16)
    for h in range(B_HEADS):
        k_ref[:, h * B_QK:h * B_QK + LANES] = kv[:, h * B_NOPE:(h + 1) * B_NOPE].astype(BF16)
        k_ref[:, h * B_QK + LANES:(h + 1) * B_QK] = kr
        vt_ref[h] = kv[:, nope_w + h * B_DV:nope_w + (h + 1) * B_DV].T.astype(BF16)


def _prep_b(zb, qn, kvn, wuq, wukv, tabs, B, S, tm):
    nb = S // tm
    tab_spec = pl.BlockSpec((tm, LANES), lambda b, i: (i, 0))
    full = lambda a: pl.BlockSpec(a.shape, lambda b, i: (0,) * a.ndim)
    return pl.pallas_call(
        _prep_b_kernel,
        grid=(B, nb),
        in_specs=[pl.BlockSpec((tm, ZB_COLS), lambda b, i: (b * nb + i, 0)),
                  full(qn), full(kvn), full(wuq), full(wukv), tab_spec, tab_spec, tab_spec],
        out_specs=[
            pl.BlockSpec((None, B_HEADS, B_QK, tm), lambda b, i: (b, 0, 0, i)),
            pl.BlockSpec((None, tm, B_HEADS * B_QK), lambda b, i: (b, i, 0)),
            pl.BlockSpec((None, B_HEADS, B_DV, tm), lambda b, i: (b, 0, 0, i)),
        ],
        out_shape=[
            jax.ShapeDtypeStruct((B, B_HEADS, B_QK, S), BF16),
            jax.ShapeDtypeStruct((B, S, B_HEADS * B_QK), BF16),
            jax.ShapeDtypeStruct((B, B_HEADS, B_DV, S), BF16),
        ],
        compiler_params=_cparams("parallel", "parallel"),
        name="prep_b",
    )(zb, qn, kvn, wuq, wukv, *tabs)


def _attn_kernel(lam_ref, qt_ref, k_ref, vt_ref, g_ref, o_ref, m_sc, l_sc, acc_sc,
                 *, n_maps, tk, nk, post_scale):
    m_sc[...] = jnp.full(m_sc.shape, -jnp.inf, F32)
    l_sc[...] = jnp.zeros(l_sc.shape, F32)
    acc_sc[...] = jnp.zeros(acc_sc.shape, F32)

    def body(j, carry):
        off = pl.multiple_of(j * tk, tk)
        kt = k_ref[pl.ds(off, tk), :]
        vt = vt_ref[:, pl.ds(off, tk)]
        for mi in range(n_maps):
            s = jnp.dot(kt, qt_ref[mi], preferred_element_type=F32)
            m_old = m_sc[mi]
            m_new = jnp.maximum(m_old, jnp.max(s, axis=0, keepdims=True))
            alpha = jnp.exp(m_old - m_new)
            p = jnp.exp(s - m_new)
            l_sc[mi] = alpha * l_sc[mi] + jnp.sum(p, axis=0, keepdims=True)
            acc_sc[mi] = alpha * acc_sc[mi] + jnp.dot(vt, p.astype(BF16), preferred_element_type=F32)
            m_sc[mi] = m_new
        return carry

    lax.fori_loop(0, nk, body, 0)

    o = acc_sc[0] / l_sc[0]
    if n_maps == 2:
        o = o - lam_ref[0] * (acc_sc[1] / l_sc[1])
        o = o * lax.rsqrt(jnp.mean(o * o, axis=0, keepdims=True) + NORM_EPS) * g_ref[...] * post_scale
    o_ref[...] = o.T.astype(o_ref.dtype)


def _attention(lam, qt, k, vt, g, *, n_maps, dk, dv, tq, tk, post_scale):
    B, H = qt.shape[0], qt.shape[1]
    S = k.shape[1]
    if n_maps == 2:
        q_spec = pl.BlockSpec((None, None, 2, dk, tq), lambda b, h, i: (b, h, 0, 0, i))
    else:
        qt = qt.reshape(B, H, 1, dk, S)
        q_spec = pl.BlockSpec((None, None, 1, dk, tq), lambda b, h, i: (b, h, 0, 0, i))
    kern = functools.partial(_attn_kernel, n_maps=n_maps, tk=tk, nk=S // tk, post_scale=post_scale)
    return pl.pallas_call(
        kern,
        grid=(B, H, S // tq),
        in_specs=[
            pl.BlockSpec(memory_space=pltpu.SMEM),
            q_spec,
            pl.BlockSpec((None, S, dk), lambda b, h, i: (b, 0, h)),
            pl.BlockSpec((None, None, dv, S), lambda b, h, i: (b, h, 0, 0)),
            pl.BlockSpec((dv, 1), lambda b, h, i: (0, 0)),
        ],
        out_specs=pl.BlockSpec((None, tq, dv), lambda b, h, i: (b, i, h)),
        out_shape=jax.ShapeDtypeStruct((B, S, H * dv), BF16),
        scratch_shapes=[
            pltpu.VMEM((n_maps, 1, tq), F32),
            pltpu.VMEM((n_maps, 1, tq), F32),
            pltpu.VMEM((n_maps, dv, tq), F32),
        ],
        compiler_params=_cparams("parallel", "parallel", "arbitrary"),
        name="attn_diff" if n_maps == 2 else "attn_mla",
    )(lam, qt, k, vt, g)


def _ret_common(q_ref, k_ref, cos_ref, sa_ref, sb_ref):
    cos, sa, sb = cos_ref[...], sa_ref[...], sb_ref[...]
    q = _rope_lanes(q_ref[...], cos, sa, sb, C_DK // 2)
    k = _rope_lanes(k_ref[...], cos, sa, sb, C_DK // 2) * (C_DK ** -0.5)
    return q, k


def _head_masks():
    r = lax.broadcasted_iota(jnp.int32, (2 * C_DK, 2 * C_DV), 0)
    c = lax.broadcasted_iota(jnp.int32, (2 * C_DK, 2 * C_DV), 1)
    return (r < C_DK) == (c < C_DV)


def _ret_bwd_kernel(lg_ref, q_ref, k_ref, v_ref, cos_ref, sa_ref, sb_ref, o_ref, r_sc, *, nc):
    C = RET_CHUNK

    @pl.when(pl.program_id(2) == 0)
    def _():
        r_sc[...] = jnp.zeros(r_sc.shape, F32)

    q, k = _ret_common(q_ref, k_ref, cos_ref, sa_ref, sb_ref)
    lg = lg_ref[...]
    idx = lax.broadcasted_iota(jnp.int32, (C, 2 * C_DK), 0).astype(F32)
    q_dec = jnp.exp(lg * (C - idx))
    k_dec = jnp.exp(lg * idx)
    decay = jnp.exp(lg * C).T
    bd = _head_masks()

    for t in range(nc):
        off = (nc - 1 - t) * C
        qc = (q[off:off + C] * q_dec).astype(BF16)
        kc = (k[off:off + C] * k_dec)
        vc = v_ref[off:off + C, :].astype(BF16)
        r = r_sc[...]
        o_ref[off:off + C, :] = jnp.dot(qc, r.astype(BF16), preferred_element_type=F32)
        u = jnp.dot(kc.T.astype(BF16), vc, preferred_element_type=F32)
        r_sc[...] = r * decay + jnp.where(bd, u, 0.0)


def _ret_fwd_kernel(lgf_ref, lgb_ref, q_ref, k_ref, v_ref, gate_ref, xb_ref, cos_ref, sa_ref, sb_ref,
                    nrm_ref, o_ref, r_sc, *, nc):
    C = RET_CHUNK

    @pl.when(pl.program_id(2) == 0)
    def _():
        r_sc[...] = jnp.zeros(r_sc.shape, F32)

    q, k = _ret_common(q_ref, k_ref, cos_ref, sa_ref, sb_ref)
    lgf, lgb = lgf_ref[...], lgb_ref[...]
    idx = lax.broadcasted_iota(jnp.int32, (C, 2 * C_DK), 0).astype(F32)
    q_dec = jnp.exp(lgf * (idx + 1.0))
    k_dec = jnp.exp(lgf * (C - 1.0 - idx))
    decay = jnp.exp(lgf * C).T
    bd = _head_masks()
    lane = lax.broadcasted_iota(jnp.int32, (C, 2 * C_DK), 1)
    ii = lax.broadcasted_iota(jnp.int32, (C, C), 0)
    jj = lax.broadcasted_iota(jnp.int32, (C, C), 1)
    dist = (ii - jj).astype(F32)
    dmats = []
    for e in range(2):
        gf = lgf[:, e * C_DK:e * C_DK + 1]
        gb = lgb[:, e * C_DK:e * C_DK + 1]
        dmats.append(jnp.where(dist >= 0, jnp.exp(gf * jnp.maximum(dist, 0.0)),
                               jnp.exp(gb * jnp.maximum(-dist, 0.0))))
    g = nrm_ref[...]

    for t in range(nc):
        off = t * C
        qr = q[off:off + C]
        kr = k[off:off + C]
        vc = v_ref[off:off + C, :].astype(BF16)
        r = r_sc[...]
        cross = jnp.dot((qr * q_dec).astype(BF16), r.astype(BF16), preferred_element_type=F32)
        kb = kr.astype(BF16)
        for e in range(2):
            qe = jnp.where((lane < C_DK) == (e == 0), qr, 0.0).astype(BF16)
            s = lax.dot_general(qe, kb, (((1,), (1,)), ((), ())), preferred_element_type=F32)
            inner = jnp.dot((s * dmats[e]).astype(BF16), vc[:, e * C_DV:(e + 1) * C_DV],
                            preferred_element_type=F32)
            ret = inner + cross[:, e * C_DV:(e + 1) * C_DV] + xb_ref[off:off + C, e * C_DV:(e + 1) * C_DV]
            oc = _rms(ret, g)
            gt = gate_ref[off:off + C, e * C_DV:(e + 1) * C_DV]
            gated = (gt * (1.0 / (1.0 + jnp.exp(-gt)))) * oc
            o_ref[off:off + C, e * C_DV:(e + 1) * C_DV] = gated.astype(o_ref.dtype)
        u = jnp.dot((kr * k_dec).T.astype(BF16), vc, preferred_element_type=F32)
        r_sc[...] = r * decay + jnp.where(bd, u, 0.0)


def _retention(zc, lgf, lgb, nrm, tabs, B, S, ts):
    ns = S // ts
    nc = ts // RET_CHUNK
    pairs = C_HEADS // 2
    qk_blocks = C_HEADS * C_DK // LANES
    lg_spec = pl.BlockSpec((None, 1, LANES), lambda b, p, i: (p, 0, 0))

    def specs(rev):
        pos = (lambda i: ns - 1 - i) if rev else (lambda i: i)
        return dict(
            q=pl.BlockSpec((ts, LANES), lambda b, p, i: (b * ns + pos(i), p)),
            k=pl.BlockSpec((ts, LANES), lambda b, p, i: (b * ns + pos(i), qk_blocks + p)),
            v=pl.BlockSpec((ts, 2 * C_DV), lambda b, p, i: (b * ns + pos(i), qk_blocks + p)),
            gate=pl.BlockSpec((ts, 2 * C_DV), lambda b, p, i: (b * ns + pos(i), qk_blocks + pairs + p)),
            out=pl.BlockSpec((ts, 2 * C_DV), lambda b, p, i: (b * ns + pos(i), p)),
            tab=pl.BlockSpec((ts, LANES), lambda b, p, i: (pos(i), 0)),
        )

    sb_ = specs(True)
    xb = pl.pallas_call(
        functools.partial(_ret_bwd_kernel, nc=nc),
        grid=(B, pairs, ns),
        in_specs=[lg_spec, sb_["q"], sb_["k"], sb_["v"], sb_["tab"], sb_["tab"], sb_["tab"]],
        out_specs=sb_["out"],
        out_shape=jax.ShapeDtypeStruct((B * S, C_HEADS * C_DV), F32),
        scratch_shapes=[pltpu.VMEM((2 * C_DK, 2 * C_DV), F32)],
        compiler_params=_cparams("parallel", "parallel", "arbitrary"),
        name="ret_bwd",
    )(lgb, zc, zc, zc, *tabs)

    sf = specs(False)
    return pl.pallas_call(
        functools.partial(_ret_fwd_kernel, nc=nc),
        grid=(B, pairs, ns),
        in_specs=[lg_spec, lg_spec, sf["q"], sf["k"], sf["v"], sf["gate"], sf["out"],
                  sf["tab"], sf["tab"], sf["tab"], pl.BlockSpec((1, C_DV), lambda b, p, i: (0, 0))],
        out_specs=sf["out"],
        out_shape=jax.ShapeDtypeStruct((B * S, C_HEADS * C_DV), BF16),
        scratch_shapes=[pltpu.VMEM((2 * C_DK, 2 * C_DV), F32)],
        compiler_params=_cparams("parallel", "parallel", "arbitrary"),
        name="ret_fwd",
    )(lgf, lgb, zc, zc, zc, zc, xb, *tabs, nrm)


def _out_proj_kernel(x_ref, oa_ref, ob_ref, oc_ref, w_ref, o_ref):
    na, nb = oa_ref.shape[1], ob_ref.shape[1]
    acc = jnp.dot(oa_ref[...], w_ref[:na], preferred_element_type=F32)
    acc += jnp.dot(ob_ref[...], w_ref[na:na + nb], preferred_element_type=F32)
    acc += jnp.dot(oc_ref[...], w_ref[na + nb:], preferred_element_type=F32)
    o_ref[...] = x_ref[...] + acc


def _out_proj(x, oa, ob, oc, w, tm):
    T, D = x.shape
    row = lambda a: pl.BlockSpec((tm, a.shape[1]), lambda i: (i, 0))
    return pl.pallas_call(
        _out_proj_kernel,
        grid=(T // tm,),
        in_specs=[row(x), row(oa), row(ob), row(oc),
                  pl.BlockSpec(w.shape, lambda i: (0, 0), pipeline_mode=pl.Buffered(1))],
        out_specs=row(x),
        out_shape=jax.ShapeDtypeStruct((T, D), F32),
        compiler_params=_cparams("parallel"),
        name="out_proj",
    )(x, oa, ob, oc, w)


def _ffn_kernel(x_ref, xp_ref, xn_ref, g_ref, wg_ref, wu_ref, cw_ref, cb_ref, wd_ref, gf_ref, o_ref,
                h_sc, acc_sc, *, seq_len, final_norm):
    tm = x_ref.shape[0]
    halo = SUBLANES
    i, j = pl.program_id(0), pl.program_id(1)

    @pl.when(j == 0)
    def _():
        g = g_ref[...]
        h_sc[:halo] = _rms(xp_ref[...], g).astype(BF16)
        h_sc[halo:halo + tm] = _rms(x_ref[...], g).astype(BF16)
        h_sc[halo + tm:] = _rms(xn_ref[...], g).astype(BF16)
        acc_sc[...] = jnp.zeros(acc_sc.shape, F32)

    gate = jnp.dot(h_sc[...], wg_ref[...], preferred_element_type=F32)
    up = jnp.dot(h_sc[halo:halo + tm], wu_ref[...], preferred_element_type=F32)
    ext = tm + 2 * halo
    left = pltpu.roll(gate, 1, 0)[halo:halo + tm]
    right = pltpu.roll(gate, ext - 1, 0)[halo:halo + tm]
    pos = lax.rem(i * tm, seq_len) + lax.broadcasted_iota(jnp.int32, (tm, 1), 0)
    left = jnp.where(pos != 0, left, 0.0)
    right = jnp.where(pos != seq_len - 1, right, 0.0)
    cw = cw_ref[...]
    gc = cb_ref[...] + left * cw[0:1] + gate[halo:halo + tm] * cw[1:2] + right * cw[2:3]
    act = (gc * (1.0 / (1.0 + jnp.exp(-gc)))) * up
    acc_sc[...] += jnp.dot(act.astype(BF16), wd_ref[...], preferred_element_type=F32)

    @pl.when(j == pl.num_programs(1) - 1)
    def _():
        y = x_ref[...] + acc_sc[...]
        if final_norm:
            y = _rms(y, gf_ref[...])
        o_ref[...] = y


def _ffn(x, g, wg, wu, cw, cb, wd, gf, *, seq_len, tm, tf, final_norm):
    T, D = x.shape
    F = wg.shape[1]
    hb = tm // SUBLANES
    last = T // SUBLANES - 1
    kern = functools.partial(_ffn_kernel, seq_len=seq_len, final_norm=final_norm)
    return pl.pallas_call(
        kern,
        grid=(T // tm, F // tf),
        in_specs=[
            pl.BlockSpec((tm, D), lambda i, j: (i, 0)),
            pl.BlockSpec((SUBLANES, D), lambda i, j: (jnp.maximum(i * hb - 1, 0), 0)),
            pl.BlockSpec((SUBLANES, D), lambda i, j: (jnp.minimum((i + 1) * hb, last), 0)),
            pl.BlockSpec((1, D), lambda i, j: (0, 0)),
            pl.BlockSpec((D, tf), lambda i, j: (0, j)),
            pl.BlockSpec((D, tf), lambda i, j: (0, j)),
            pl.BlockSpec((CONV_WIDTH, tf), lambda i, j: (0, j)),
            pl.BlockSpec((1, tf), lambda i, j: (0, j)),
            pl.BlockSpec((tf, D), lambda i, j: (j, 0)),
            pl.BlockSpec((1, D), lambda i, j: (0, 0)),
        ],
        out_specs=pl.BlockSpec((tm, D), lambda i, j: (i, 0)),
        out_shape=jax.ShapeDtypeStruct((T, D), F32),
        scratch_shapes=[pltpu.VMEM((tm + 2 * SUBLANES, D), BF16), pltpu.VMEM((tm, D), F32)],
        compiler_params=_cparams("parallel", "arbitrary"),
        name="ffn",
    )(x, x, x, g, wg, wu, cw, cb, wd, gf)


def _rope_tables(S, theta, n_rot, group):
    half = n_rot // 2
    pos = jnp.arange(S, dtype=F32)
    inv = jnp.power(jnp.float32(theta), -jnp.arange(half, dtype=F32) * 2.0 / n_rot)
    ang = pos[:, None] * inv[None, :]
    cos, sin = jnp.cos(ang), jnp.sin(ang)
    ones = jnp.ones((S, group - n_rot), F32)
    zeros = jnp.zeros((S, half), F32)
    zrest = jnp.zeros((S, group - n_rot), F32)
    reps = LANES // group
    c = jnp.tile(jnp.concatenate([cos, cos, ones], axis=1), (1, reps))
    sa = jnp.tile(jnp.concatenate([-sin, zeros, zrest], axis=1), (1, reps))
    sb = jnp.tile(jnp.concatenate([zeros, sin, zrest], axis=1), (1, reps))
    return c, sa, sb


def _pick(n, pref):
    t = min(n, pref)
    while n % t:
        t //= 2
    return t


def _layer_weights(l, p):
    d = p["w_in"].shape[1]
    cut = ZA_COLS + B_Q_RANK + B_KV_RANK + B_ROPE
    w_in = jnp.concatenate([p["w_in"][l][:, :cut], jnp.zeros((d, B_ROPE), F32), p["w_in"][l][:, cut:]], axis=1)
    wuq = p["w_uq"][l].reshape(B_Q_RANK, B_HEADS, B_NOPE + B_ROPE)
    wuq = jnp.concatenate([wuq[:, :, :B_NOPE].reshape(B_Q_RANK, -1), wuq[:, :, B_NOPE:].reshape(B_Q_RANK, -1)], axis=1)
    wukv = p["w_ukv"][l].reshape(B_KV_RANK, B_HEADS, B_NOPE + B_DV)
    wukv = jnp.concatenate([wukv[:, :, :B_NOPE].reshape(B_KV_RANK, -1), wukv[:, :, B_NOPE:].reshape(B_KV_RANK, -1)], axis=1)
    lp = p["diff_lambda"][l].astype(F32)
    lam_init = 0.8 - 0.6 * math.exp(-0.3 * l)
    lam = jnp.exp(jnp.sum(lp[0] * lp[1])) - jnp.exp(jnp.sum(lp[2] * lp[3])) + lam_init

    def lane_lg(dec):
        lg = jax.nn.log_sigmoid(dec.astype(F32))
        return jnp.repeat(lg, C_DK).reshape(C_HEADS // 2, 1, LANES)

    return dict(
        norm_mix=p["norm_mix"][l][None], w_in=w_in.astype(BF16),
        lam=lam.reshape(1), lam_init=lam_init, diff_norm=p["diff_norm"][l][:, None],
        mla_q_norm=p["mla_q_norm"][l][None], mla_kv_norm=p["mla_kv_norm"][l][None],
        w_uq=wuq.astype(BF16), w_ukv=wukv.astype(BF16),
        lgf=lane_lg(p["ret_decay_fwd"][l]), lgb=lane_lg(p["ret_decay_bwd"][l]), ret_norm=p["ret_norm"][l][None],
        w_o=p["w_o"][l].astype(BF16), norm_ffn=p["norm_ffn"][l][None],
        w_gate=p["w_gate"][l].astype(BF16), w_up=p["w_up"][l].astype(BF16),
        conv_w=p["conv_w"][l], conv_b=p["conv_b"][l][None], w_down=p["w_down"][l].astype(BF16),
    )


def _trunk(x3, layers, norm_final):
    B, S, D = x3.shape
    T = B * S
    x = x3.reshape(T, D)
    tabs_a = _rope_tables(S, ROPE_THETA, A_ROT, A_DK)
    tabs_b = _rope_tables(S, ROPE_THETA, B_ROPE, B_ROPE)
    tabs_c = _rope_tables(S, RET_THETA, C_DK, C_DK)
    tm_in = _pick(T, 256)
    tm_prep = _pick(S, 512)
    tq, tk = _pick(S, 512), _pick(S, 512)
    ts = _pick(S, 1024)
    tm_out = _pick(T, 512)
    tm_ffn, tf = _pick(S, 512), _pick(layers[0]["w_gate"].shape[1], 512)
    no_lam = jnp.zeros((1,), F32)
    no_norm = jnp.ones((B_DV, 1), F32)
    for l, w in enumerate(layers):
        za, zb, zc = _in_proj(x, w["norm_mix"], w["w_in"], tm_in)
        qt, k, vt = _prep_a(za, tabs_a, B, S, tm_prep)
        oa = _attention(w["lam"], qt, k, vt, w["diff_norm"], n_maps=2, dk=2 * A_DK, dv=A_DV, tq=tq, tk=tk,
                        post_scale=1.0 - w["lam_init"]).reshape(T, -1)
        qt, k, vt = _prep_b(zb, w["mla_q_norm"], w["mla_kv_norm"], w["w_uq"], w["w_ukv"], tabs_b, B, S, tm_prep)
        ob = _attention(no_lam, qt, k, vt, no_norm, n_maps=1, dk=B_QK, dv=B_DV, tq=tq, tk=tk,
                        post_scale=1.0).reshape(T, -1)
        oc = _retention(zc, w["lgf"], w["lgb"], w["ret_norm"], tabs_c, B, S, ts)
        x = _out_proj(x, oa, ob, oc, w["w_o"], tm_out)
        x = _ffn(x, w["norm_ffn"], w["w_gate"], w["w_up"], w["conv_w"], w["conv_b"], w["w_down"], norm_final,
                 seq_len=S, tm=tm_ffn, tf=tf, final_norm=(l == len(layers) - 1))
    return x.reshape(B, S, D)


def kernel(x_prompt, x_sample, norm_mix, w_in, diff_lambda, diff_norm, mla_q_norm, mla_kv_norm, w_uq, w_ukv,
           ret_decay_fwd, ret_decay_bwd, ret_norm, w_o, norm_ffn, w_gate, w_up, conv_w, conv_b, w_down, norm_final):
    p = dict(norm_mix=norm_mix, w_in=w_in, diff_lambda=diff_lambda, diff_norm=diff_norm, mla_q_norm=mla_q_norm,
             mla_kv_norm=mla_kv_norm, w_uq=w_uq, w_ukv=w_ukv, ret_decay_fwd=ret_decay_fwd,
             ret_decay_bwd=ret_decay_bwd, ret_norm=ret_norm, w_o=w_o, norm_ffn=norm_ffn, w_gate=w_gate, w_up=w_up,
             conv_w=conv_w, conv_b=conv_b, w_down=w_down)
    layers = [_layer_weights(l, p) for l in range(norm_mix.shape[0])]
    gf = norm_final[None]
    return (_trunk(x_prompt, layers, gf), _trunk(x_sample, layers, gf))
```

```python
import functools
import math

import jax
import jax.numpy as jnp
from jax import lax
from jax.experimental import pallas as pl
from jax.experimental.pallas import tpu as pltpu

F32 = jnp.float32
BF16 = jnp.bfloat16

NORM_EPS = 1e-5
ROPE_THETA = 500000.0
RET_THETA = 10000.0
RET_CHUNK = 128
A_HEADS, A_DK, A_DV = 4, 64, 128
A_ROT = A_DK // 4
B_HEADS, B_Q_RANK, B_KV_RANK, B_NOPE, B_ROPE, B_DV = 6, 512, 256, 128, 64, 128
C_HEADS, C_DK, C_DV = 6, 64, 128
CONV_WIDTH = 3

LANES = 128
SUBLANES = 8
ZA_COLS = 3 * A_HEADS * 2 * A_DK
ZB_COLS = B_Q_RANK + B_KV_RANK + 2 * B_ROPE
ZC_COLS = 2 * C_HEADS * C_DK + 2 * C_HEADS * C_DV
B_QK = 2 * LANES
VMEM_LIMIT = 56 * 1024 * 1024
LOG2E = math.log2(math.e)


def _cparams(*sem):
    return pltpu.CompilerParams(dimension_semantics=sem, vmem_limit_bytes=VMEM_LIMIT)


def _rms(x, g):
    return x * lax.rsqrt(jnp.mean(x * x, axis=-1, keepdims=True) + NORM_EPS) * g


def _rope_lanes(x, cos, sin_a, sin_b, half):
    return (x * cos + pltpu.roll(x, LANES - half, 1) * sin_a + pltpu.roll(x, half, 1) * sin_b)


def _in_proj_kernel(x_ref, g_ref, w_ref, za_ref, zb_ref, zc_ref):
    h = _rms(x_ref[...], g_ref[...]).astype(BF16)
    za_ref[...] = jnp.dot(h, w_ref[:, :ZA_COLS], preferred_element_type=F32)
    zb_ref[...] = jnp.dot(h, w_ref[:, ZA_COLS:ZA_COLS + ZB_COLS], preferred_element_type=F32)
    zc_ref[...] = jnp.dot(h, w_ref[:, ZA_COLS + ZB_COLS:], preferred_element_type=F32)


def _in_proj(x, g, w, tm):
    T, D = x.shape
    n = w.shape[1]
    return pl.pallas_call(
        _in_proj_kernel,
        grid=(T // tm,),
        in_specs=[
            pl.BlockSpec((tm, D), lambda i: (i, 0)),
            pl.BlockSpec((1, D), lambda i: (0, 0)),
            pl.BlockSpec((D, n), lambda i: (0, 0), pipeline_mode=pl.Buffered(1)),
        ],
        out_specs=[
            pl.BlockSpec((tm, ZA_COLS), lambda i: (i, 0)),
            pl.BlockSpec((tm, ZB_COLS), lambda i: (i, 0)),
            pl.BlockSpec((tm, ZC_COLS), lambda i: (i, 0)),
        ],
        out_shape=[
            jax.ShapeDtypeStruct((T, ZA_COLS), F32),
            jax.ShapeDtypeStruct((T, ZB_COLS), F32),
            jax.ShapeDtypeStruct((T, ZC_COLS), F32),
        ],
        compiler_params=_cparams("parallel"),
        name="in_proj",
    )(x, g, w)


def _prep_a_kernel(z_ref, cos_ref, sa_ref, sb_ref, qt_ref, k_ref, vt_ref):
    cos, sa, sb = cos_ref[...], sa_ref[...], sb_ref[...]
    tm = z_ref.shape[0]
    row = lax.broadcasted_iota(jnp.int32, (LANES, tm), 0)
    hw = 2 * A_DK
    for h in range(A_HEADS):
        q = _rope_lanes(z_ref[:, h * hw:(h + 1) * hw], cos, sa, sb, A_ROT // 2) * (A_DK ** -0.5 * LOG2E)
        qt = q.T
        qt_ref[h, 0] = jnp.where(row < A_DK, qt, 0.0).astype(BF16)
        qt_ref[h, 1] = jnp.where(row >= A_DK, qt, 0.0).astype(BF16)
        k = _rope_lanes(z_ref[:, (A_HEADS + h) * hw:(A_HEADS + h + 1) * hw], cos, sa, sb, A_ROT // 2)
        k_ref[:, h * hw:(h + 1) * hw] = k.astype(BF16)
        v = z_ref[:, (2 * A_HEADS + h) * hw:(2 * A_HEADS + h + 1) * hw]
        vt_ref[h] = v.T.astype(BF16)


def _prep_a(za, tabs, B, S, tm):
    nb = S // tm
    row_map = lambda b, i: (b * nb + i, 0)
    tab_spec = pl.BlockSpec((tm, LANES), lambda b, i: (i, 0))
    return pl.pallas_call(
        _prep_a_kernel,
        grid=(B, nb),
        in_specs=[pl.BlockSpec((tm, ZA_COLS), row_map), tab_spec, tab_spec, tab_spec],
        out_specs=[
            pl.BlockSpec((None, A_HEADS, 2, LANES, tm), lambda b, i: (b, 0, 0, 0, i)),
            pl.BlockSpec((None, tm, A_HEADS * LANES), lambda b, i: (b, i, 0)),
            pl.BlockSpec((None, A_HEADS, A_DV, tm), lambda b, i: (b, 0, 0, i)),
        ],
        out_shape=[
            jax.ShapeDtypeStruct((B, A_HEADS, 2, LANES, S), BF16),
            jax.ShapeDtypeStruct((B, S, A_HEADS * LANES), BF16),
            jax.ShapeDtypeStruct((B, A_HEADS, A_DV, S), BF16),
        ],
        compiler_params=_cparams("parallel", "parallel"),
        name="prep_a",
    )(za, *tabs)


def _prep_b_kernel(z_ref, qn_ref, kvn_ref, wuq_ref, wukv_ref, cos_ref, sa_ref, sb_ref,
                   qt_ref, k_ref, vt_ref):
    cos, sa, sb = cos_ref[...], sa_ref[...], sb_ref[...]
    tm = z_ref.shape[0]
    half = B_ROPE // 2
    scale = (B_NOPE + B_ROPE) ** -0.5 * LOG2E
    lane = lax.broadcasted_iota(jnp.int32, (tm, LANES), 1)
    nope_w = B_HEADS * B_NOPE

    cq = _rms(z_ref[:, :B_Q_RANK], qn_ref[...]).astype(BF16)
    q = jnp.dot(cq, wuq_ref[...], preferred_element_type=F32) * scale
    ckv = _rms(z_ref[:, B_Q_RANK:B_Q_RANK + B_KV_RANK], kvn_ref[...]).astype(BF16)
    kv = jnp.dot(ckv, wukv_ref[...], preferred_element_type=F32)
    kr = _rope_lanes(z_ref[:, B_Q_RANK + B_KV_RANK:ZB_COLS], cos, sa, sb, half).astype(BF16)

    for c in range(B_HEADS // 2):
        qr = _rope_lanes(q[:, nope_w + c * LANES:nope_w + (c + 1) * LANES], cos, sa, sb, half)
        for e in range(2):
            h = 2 * c + e
            tail = qr if e == 0 else pltpu.roll(qr, B_ROPE, 1)
            tail = jnp.where(lane < B_ROPE, tail, 0.0)
            qt_ref[h, :LANES] = q[:, h * B_NOPE:(h + 1) * B_NOPE].T.astype(BF16)
            qt_ref[h, LANES:] = tail.T.astype(BF16)
    for h in range(B_HEADS):
        k_ref[:, h * B_QK:h * B_QK + LANES] = kv[:, h * B_NOPE:(h + 1) * B_NOPE].astype(BF16)
        k_ref[:, h * B_QK + LANES:(h + 1) * B_QK] = kr
        vt_ref[h] = kv[:, nope_w + h * B_DV:nope_w + (h + 1) * B_DV].T.astype(BF16)


def _prep_b(zb, qn, kvn, wuq, wukv, tabs, B, S, tm):
    nb = S // tm
    tab_spec = pl.BlockSpec((tm, LANES), lambda b, i: (i, 0))
    full = lambda a: pl.BlockSpec(a.shape, lambda b, i: (0,) * a.ndim)
    return pl.pallas_call(
        _prep_b_kernel,
        grid=(B, nb),
        in_specs=[pl.BlockSpec((tm, ZB_COLS), lambda b, i: (b * nb + i, 0)),
                  full(qn), full(kvn), full(wuq), full(wukv), tab_spec, tab_spec, tab_spec],
        out_specs=[
            pl.BlockSpec((None, B_HEADS, B_QK, tm), lambda b, i: (b, 0, 0, i)),
            pl.BlockSpec((None, tm, B_HEADS * B_QK), lambda b, i: (b, i, 0)),
            pl.BlockSpec((None, B_HEADS, B_DV, tm), lambda b, i: (b, 0, 0, i)),
        ],
        out_shape=[
            jax.ShapeDtypeStruct((B, B_HEADS, B_QK, S), BF16),
            jax.ShapeDtypeStruct((B, S, B_HEADS * B_QK), BF16),
            jax.ShapeDtypeStruct((B, B_HEADS, B_DV, S), BF16),
        ],
        compiler_params=_cparams("parallel", "parallel"),
        name="prep_b",
    )(zb, qn, kvn, wuq, wukv, *tabs)


def _attn_kernel(lam_ref, qt_ref, k_ref, vt_ref, g_ref, o_ref, s_sc, mx_sc, m_sc, l_sc, acc_sc,
                 *, n_maps, n_q, tq, tk, nk, post_scale):
    chains = [(mi, qi) for mi in range(n_maps) for qi in range(n_q)]
    m_sc[...] = jnp.full(m_sc.shape, -jnp.inf, F32)
    l_sc[...] = jnp.zeros(l_sc.shape, F32)
    acc_sc[...] = jnp.zeros(acc_sc.shape, F32)

    def scores(j, slot):
        kt = k_ref[pl.ds(pl.multiple_of(j * tk, tk), tk), :]
        for c, (mi, qi) in enumerate(chains):
            s = jnp.dot(kt, qt_ref[mi, :, qi * tq:(qi + 1) * tq], preferred_element_type=F32)
            s_sc[slot, c] = s
            mx_sc[slot, c] = jnp.max(s, axis=0, keepdims=True)

    def consume(j, slot):
        vt = vt_ref[:, pl.ds(pl.multiple_of(j * tk, tk), tk)]
        for c in range(len(chains)):
            m_old = m_sc[c]
            m_new = jnp.maximum(m_old, mx_sc[slot, c])
            alpha = jnp.exp2(m_old - m_new)
            p = jnp.exp2(s_sc[slot, c] - m_new)
            l_sc[c] = alpha * l_sc[c] + jnp.sum(p, axis=0, keepdims=True)
            acc_sc[c] = alpha * acc_sc[c] + jnp.dot(vt, p.astype(BF16), preferred_element_type=F32)
            m_sc[c] = m_new

    def pair(t, prefetch):
        scores(2 * t + 1, 1)
        consume(2 * t, 0)
        if prefetch:
            scores(2 * t + 2, 0)
        consume(2 * t + 1, 1)

    def looped_pair(t, carry):
        pair(t, True)
        return carry

    scores(0, 0)
    if nk % 2 == 0:
        lax.fori_loop(0, nk // 2 - 1, looped_pair, 0)
        pair(nk // 2 - 1, False)
    else:
        lax.fori_loop(0, nk // 2, looped_pair, 0)
        consume(nk - 1, 0)

    for qi in range(n_q):
        o = acc_sc[qi] / l_sc[qi]
        if n_maps == 2:
            o = o - lam_ref[0] * (acc_sc[n_q + qi] / l_sc[n_q + qi])
            o = o * lax.rsqrt(jnp.mean(o * o, axis=0, keepdims=True) + NORM_EPS) * g_ref[...] * post_scale
        o_ref[qi * tq:(qi + 1) * tq, :] = o.T.astype(o_ref.dtype)


def _attention(lam, qt, k, vt, g, *, n_maps, n_q, dk, dv, tq, tk, post_scale):
    B, H = qt.shape[0], qt.shape[1]
    S = k.shape[1]
    bq = n_q * tq
    nch = n_maps * n_q
    kern = functools.partial(_attn_kernel, n_maps=n_maps, n_q=n_q, tq=tq, tk=tk, nk=S // tk,
                             post_scale=post_scale)
    return pl.pallas_call(
        kern,
        grid=(B, H, S // bq),
        in_specs=[
            pl.BlockSpec(memory_space=pltpu.SMEM),
            pl.BlockSpec((None, None, n_maps, dk, bq), lambda b, h, i: (b, h, 0, 0, i)),
            pl.BlockSpec((None, S, dk), lambda b, h, i: (b, 0, h)),
            pl.BlockSpec((None, None, dv, S), lambda b, h, i: (b, h, 0, 0)),
            pl.BlockSpec((dv, 1), lambda b, h, i: (0, 0)),
        ],
        out_specs=pl.BlockSpec((None, bq, dv), lambda b, h, i: (b, i, h)),
        out_shape=jax.ShapeDtypeStruct((B, S, H * dv), BF16),
        scratch_shapes=[
            pltpu.VMEM((2, nch, tk, tq), F32),
            pltpu.VMEM((2, nch, 1, tq), F32),
            pltpu.VMEM((nch, 1, tq), F32),
            pltpu.VMEM((nch, 1, tq), F32),
            pltpu.VMEM((nch, dv, tq), F32),
        ],
        compiler_params=_cparams("parallel", "parallel", "arbitrary"),
        name="attn_diff" if n_maps == 2 else "attn_mla",
    )(lam, qt, k, vt, g)


def _ret_common(q_ref, k_ref, cos_ref, sa_ref, sb_ref):
    cos, sa, sb = cos_ref[...], sa_ref[...], sb_ref[...]
    q = _rope_lanes(q_ref[...], cos, sa, sb, C_DK // 2)
    k = _rope_lanes(k_ref[...], cos, sa, sb, C_DK // 2) * (C_DK ** -0.5)
    return q, k


def _head_masks():
    r = lax.broadcasted_iota(jnp.int32, (2 * C_DK, 2 * C_DV), 0)
    c = lax.broadcasted_iota(jnp.int32, (2 * C_DK, 2 * C_DV), 1)
    return (r < C_DK) == (c < C_DV)


def _ret_bwd_kernel(lg_ref, q_ref, k_ref, v_ref, cos_ref, sa_ref, sb_ref, o_ref, r_sc, *, nc):
    C = RET_CHUNK

    @pl.when(pl.program_id(2) == 0)
    def _():
        r_sc[...] = jnp.zeros(r_sc.shape, F32)

    q, k = _ret_common(q_ref, k_ref, cos_ref, sa_ref, sb_ref)
    lg = lg_ref[...]
    idx = lax.broadcasted_iota(jnp.int32, (C, 2 * C_DK), 0).astype(F32)
    q_dec = jnp.exp(lg * (C - idx))
    k_dec = jnp.exp(lg * idx)
    decay = jnp.exp(lg * C).T
    bd = _head_masks()

    for t in range(nc):
        off = (nc - 1 - t) * C
        qc = (q[off:off + C] * q_dec).astype(BF16)
        kc = (k[off:off + C] * k_dec)
        vc = v_ref[off:off + C, :].astype(BF16)
        r = r_sc[...]
        o_ref[off:off + C, :] = jnp.dot(qc, r.astype(BF16), preferred_element_type=F32)
        u = jnp.dot(kc.T.astype(BF16), vc, preferred_element_type=F32)
        r_sc[...] = r * decay + jnp.where(bd, u, 0.0)


def _ret_fwd_kernel(lgf_ref, lgb_ref, q_ref, k_ref, v_ref, gate_ref, xb_ref, cos_ref, sa_ref, sb_ref,
                    nrm_ref, o_ref, r_sc, *, nc):
    C = RET_CHUNK

    @pl.when(pl.program_id(2) == 0)
    def _():
        r_sc[...] = jnp.zeros(r_sc.shape, F32)

    q, k = _ret_common(q_ref, k_ref, cos_ref, sa_ref, sb_ref)
    lgf, lgb = lgf_ref[...], lgb_ref[...]
    idx = lax.broadcasted_iota(jnp.int32, (C, 2 * C_DK), 0).astype(F32)
    q_dec = jnp.exp(lgf * (idx + 1.0))
    k_dec = jnp.exp(lgf * (C - 1.0 - idx))
    decay = jnp.exp(lgf * C).T
    bd = _head_masks()
    lane = lax.broadcasted_iota(jnp.int32, (C, 2 * C_DK), 1)
    ii = lax.broadcasted_iota(jnp.int32, (C, C), 0)
    jj = lax.broadcasted_iota(jnp.int32, (C, C), 1)
    dist = (ii - jj).astype(F32)
    dmats = []
    for e in range(2):
        gf = lgf[:, e * C_DK:e * C_DK + 1]
        gb = lgb[:, e * C_DK:e * C_DK + 1]
        dmats.append(jnp.where(dist >= 0, jnp.exp(gf * jnp.maximum(dist, 0.0)),
                               jnp.exp(gb * jnp.maximum(-dist, 0.0))))
    g = nrm_ref[...]

    for t in range(nc):
        off = t * C
        qr = q[off:off + C]
        kr = k[off:off + C]
        vc = v_ref[off:off + C, :].astype(BF16)
        r = r_sc[...]
        cross = jnp.dot((qr * q_dec).astype(BF16), r.astype(BF16), preferred_element_type=F32)
        kb = kr.astype(BF16)
        for e in range(2):
            qe = jnp.where((lane < C_DK) == (e == 0), qr, 0.0).astype(BF16)
            s = lax.dot_general(qe, kb, (((1,), (1,)), ((), ())), preferred_element_type=F32)
            inner = jnp.dot((s * dmats[e]).astype(BF16), vc[:, e * C_DV:(e + 1) * C_DV],
                            preferred_element_type=F32)
            ret = inner + cross[:, e * C_DV:(e + 1) * C_DV] + xb_ref[off:off + C, e * C_DV:(e + 1) * C_DV]
            oc = _rms(ret, g)
            gt = gate_ref[off:off + C, e * C_DV:(e + 1) * C_DV]
            gated = (gt * (1.0 / (1.0 + jnp.exp(-gt)))) * oc
            o_ref[off:off + C, e * C_DV:(e + 1) * C_DV] = gated.astype(o_ref.dtype)
        u = jnp.dot((kr * k_dec).T.astype(BF16), vc, preferred_element_type=F32)
        r_sc[...] = r * decay + jnp.where(bd, u, 0.0)


def _retention(zc, lgf, lgb, nrm, tabs, B, S, ts):
    ns = S // ts
    nc = ts // RET_CHUNK
    pairs = C_HEADS // 2
    qk_blocks = C_HEADS * C_DK // LANES
    lg_spec = pl.BlockSpec((None, 1, LANES), lambda b, p, i: (p, 0, 0))

    def specs(rev):
        pos = (lambda i: ns - 1 - i) if rev else (lambda i: i)
        return dict(
            q=pl.BlockSpec((ts, LANES), lambda b, p, i: (b * ns + pos(i), p)),
            k=pl.BlockSpec((ts, LANES), lambda b, p, i: (b * ns + pos(i), qk_blocks + p)),
            v=pl.BlockSpec((ts, 2 * C_DV), lambda b, p, i: (b * ns + pos(i), qk_blocks + p)),
            gate=pl.BlockSpec((ts, 2 * C_DV), lambda b, p, i: (b * ns + pos(i), qk_blocks + pairs + p)),
            out=pl.BlockSpec((ts, 2 * C_DV), lambda b, p, i: (b * ns + pos(i), p)),
            tab=pl.BlockSpec((ts, LANES), lambda b, p, i: (pos(i), 0)),
        )

    sb_ = specs(True)
    xb = pl.pallas_call(
        functools.partial(_ret_bwd_kernel, nc=nc),
        grid=(B, pairs, ns),
        in_specs=[lg_spec, sb_["q"], sb_["k"], sb_["v"], sb_["tab"], sb_["tab"], sb_["tab"]],
        out_specs=sb_["out"],
        out_shape=jax.ShapeDtypeStruct((B * S, C_HEADS * C_DV), F32),
        scratch_shapes=[pltpu.VMEM((2 * C_DK, 2 * C_DV), F32)],
        compiler_params=_cparams("parallel", "parallel", "arbitrary"),
        name="ret_bwd",
    )(lgb, zc, zc, zc, *tabs)

    sf = specs(False)
    return pl.pallas_call(
        functools.partial(_ret_fwd_kernel, nc=nc),
        grid=(B, pairs, ns),
        in_specs=[lg_spec, lg_spec, sf["q"], sf["k"], sf["v"], sf["gate"], sf["out"],
                  sf["tab"], sf["tab"], sf["tab"], pl.BlockSpec((1, C_DV), lambda b, p, i: (0, 0))],
        out_specs=sf["out"],
        out_shape=jax.ShapeDtypeStruct((B * S, C_HEADS * C_DV), BF16),
        scratch_shapes=[pltpu.VMEM((2 * C_DK, 2 * C_DV), F32)],
        compiler_params=_cparams("parallel", "parallel", "arbitrary"),
        name="ret_fwd",
    )(lgf, lgb, zc, zc, zc, zc, xb, *tabs, nrm)


def _out_proj_kernel(x_ref, oa_ref, ob_ref, oc_ref, w_ref, o_ref):
    na, nb = oa_ref.shape[1], ob_ref.shape[1]
    acc = jnp.dot(oa_ref[...], w_ref[:na], preferred_element_type=F32)
    acc += jnp.dot(ob_ref[...], w_ref[na:na + nb], preferred_element_type=F32)
    acc += jnp.dot(oc_ref[...], w_ref[na + nb:], preferred_element_type=F32)
    o_ref[...] = x_ref[...] + acc


def _out_proj(x, oa, ob, oc, w, tm):
    T, D = x.shape
    row = lambda a: pl.BlockSpec((tm, a.shape[1]), lambda i: (i, 0))
    return pl.pallas_call(
        _out_proj_kernel,
        grid=(T // tm,),
        in_specs=[row(x), row(oa), row(ob), row(oc),
                  pl.BlockSpec(w.shape, lambda i: (0, 0), pipeline_mode=pl.Buffered(1))],
        out_specs=row(x),
        out_shape=jax.ShapeDtypeStruct((T, D), F32),
        compiler_params=_cparams("parallel"),
        name="out_proj",
    )(x, oa, ob, oc, w)


def _ffn_kernel(x_ref, xp_ref, xn_ref, g_ref, wg_ref, wu_ref, cw_ref, cb_ref, wd_ref, gf_ref, o_ref,
                h_sc, acc_sc, *, seq_len, final_norm):
    tm = x_ref.shape[0]
    halo = SUBLANES
    i, j = pl.program_id(0), pl.program_id(1)

    @pl.when(j == 0)
    def _():
        g = g_ref[...]
        h_sc[:halo] = _rms(xp_ref[...], g).astype(BF16)
        h_sc[halo:halo + tm] = _rms(x_ref[...], g).astype(BF16)
        h_sc[halo + tm:] = _rms(xn_ref[...], g).astype(BF16)
        acc_sc[...] = jnp.zeros(acc_sc.shape, F32)

    gate = jnp.dot(h_sc[...], wg_ref[...], preferred_element_type=F32)
    up = jnp.dot(h_sc[halo:halo + tm], wu_ref[...], preferred_element_type=F32)
    ext = tm + 2 * halo
    left = pltpu.roll(gate, 1, 0)[halo:halo + tm]
    right = pltpu.roll(gate, ext - 1, 0)[halo:halo + tm]
    pos = lax.rem(i * tm, seq_len) + lax.broadcasted_iota(jnp.int32, (tm, 1), 0)
    left = jnp.where(pos != 0, left, 0.0)
    right = jnp.where(pos != seq_len - 1, right, 0.0)
    cw = cw_ref[...]
    gc = cb_ref[...] + left * cw[0:1] + gate[halo:halo + tm] * cw[1:2] + right * cw[2:3]
    act = (gc * (1.0 / (1.0 + jnp.exp(-gc)))) * up
    acc_sc[...] += jnp.dot(act.astype(BF16), wd_ref[...], preferred_element_type=F32)

    @pl.when(j == pl.num_programs(1) - 1)
    def _():
        y = x_ref[...] + acc_sc[...]
        if final_norm:
            y = _rms(y, gf_ref[...])
        o_ref[...] = y


def _ffn(x, g, wg, wu, cw, cb, wd, gf, *, seq_len, tm, tf, final_norm):
    T, D = x.shape
    F = wg.shape[1]
    hb = tm // SUBLANES
    last = T // SUBLANES - 1
    kern = functools.partial(_ffn_kernel, seq_len=seq_len, final_norm=final_norm)
    return pl.pallas_call(
        kern,
        grid=(T // tm, F // tf),
        in_specs=[
            pl.BlockSpec((tm, D), lambda i, j: (i, 0)),
            pl.BlockSpec((SUBLANES, D), lambda i, j: (jnp.maximum(i * hb - 1, 0), 0)),
            pl.BlockSpec((SUBLANES, D), lambda i, j: (jnp.minimum((i + 1) * hb, last), 0)),
            pl.BlockSpec((1, D), lambda i, j: (0, 0)),
            pl.BlockSpec((D, tf), lambda i, j: (0, j)),
            pl.BlockSpec((D, tf), lambda i, j: (0, j)),
            pl.BlockSpec((CONV_WIDTH, tf), lambda i, j: (0, j)),
            pl.BlockSpec((1, tf), lambda i, j: (0, j)),
            pl.BlockSpec((tf, D), lambda i, j: (j, 0)),
            pl.BlockSpec((1, D), lambda i, j: (0, 0)),
        ],
        out_specs=pl.BlockSpec((tm, D), lambda i, j: (i, 0)),
        out_shape=jax.ShapeDtypeStruct((T, D), F32),
        scratch_shapes=[pltpu.VMEM((tm + 2 * SUBLANES, D), BF16), pltpu.VMEM((tm, D), F32)],
        compiler_params=_cparams("parallel", "arbitrary"),
        name="ffn",
    )(x, x, x, g, wg, wu, cw, cb, wd, gf)


def _rope_tables(S, theta, n_rot, group):
    half = n_rot // 2
    pos = jnp.arange(S, dtype=F32)
    inv = jnp.power(jnp.float32(theta), -jnp.arange(half, dtype=F32) * 2.0 / n_rot)
    ang = pos[:, None] * inv[None, :]
    cos, sin = jnp.cos(ang), jnp.sin(ang)
    ones = jnp.ones((S, group - n_rot), F32)
    zeros = jnp.zeros((S, half), F32)
    zrest = jnp.zeros((S, group - n_rot), F32)
    reps = LANES // group
    c = jnp.tile(jnp.concatenate([cos, cos, ones], axis=1), (1, reps))
    sa = jnp.tile(jnp.concatenate([-sin, zeros, zrest], axis=1), (1, reps))
    sb = jnp.tile(jnp.concatenate([zeros, sin, zrest], axis=1), (1, reps))
    return c, sa, sb


def _pick(n, pref):
    t = min(n, pref)
    while n % t:
        t //= 2
    return t


def _layer_weights(l, p):
    d = p["w_in"].shape[1]
    cut = ZA_COLS + B_Q_RANK + B_KV_RANK + B_ROPE
    w_in = jnp.concatenate([p["w_in"][l][:, :cut], jnp.zeros((d, B_ROPE), F32), p["w_in"][l][:, cut:]], axis=1)
    wuq = p["w_uq"][l].reshape(B_Q_RANK, B_HEADS, B_NOPE + B_ROPE)
    wuq = jnp.concatenate([wuq[:, :, :B_NOPE].reshape(B_Q_RANK, -1), wuq[:, :, B_NOPE:].reshape(B_Q_RANK, -1)], axis=1)
    wukv = p["w_ukv"][l].reshape(B_KV_RANK, B_HEADS, B_NOPE + B_DV)
    wukv = jnp.concatenate([wukv[:, :, :B_NOPE].reshape(B_KV_RANK, -1), wukv[:, :, B_NOPE:].reshape(B_KV_RANK, -1)], axis=1)
    lp = p["diff_lambda"][l].astype(F32)
    lam_init = 0.8 - 0.6 * math.exp(-0.3 * l)
    lam = jnp.exp(jnp.sum(lp[0] * lp[1])) - jnp.exp(jnp.sum(lp[2] * lp[3])) + lam_init

    def lane_lg(dec):
        lg = jax.nn.log_sigmoid(dec.astype(F32))
        return jnp.repeat(lg, C_DK).reshape(C_HEADS // 2, 1, LANES)

    return dict(
        norm_mix=p["norm_mix"][l][None], w_in=w_in.astype(BF16),
        lam=lam.reshape(1), lam_init=lam_init, diff_norm=p["diff_norm"][l][:, None],
        mla_q_norm=p["mla_q_norm"][l][None], mla_kv_norm=p["mla_kv_norm"][l][None],
        w_uq=wuq.astype(BF16), w_ukv=wukv.astype(BF16),
        lgf=lane_lg(p["ret_decay_fwd"][l]), lgb=lane_lg(p["ret_decay_bwd"][l]), ret_norm=p["ret_norm"][l][None],
        w_o=p["w_o"][l].astype(BF16), norm_ffn=p["norm_ffn"][l][None],
        w_gate=p["w_gate"][l].astype(BF16), w_up=p["w_up"][l].astype(BF16),
        conv_w=p["conv_w"][l], conv_b=p["conv_b"][l][None], w_down=p["w_down"][l].astype(BF16),
    )


def _trunk(x3, layers, norm_final):
    B, S, D = x3.shape
    T = B * S
    x = x3.reshape(T, D)
    tabs_a = _rope_tables(S, ROPE_THETA, A_ROT, A_DK)
    tabs_b = _rope_tables(S, ROPE_THETA, B_ROPE, B_ROPE)
    tabs_c = _rope_tables(S, RET_THETA, C_DK, C_DK)
    tm_in = _pick(T, 256)
    tm_prep = _pick(S, 512)
    tq, tk = _pick(S, 512), _pick(S, 512)
    nq_b = 2 if S % (2 * tq) == 0 else 1
    ts = _pick(S, 1024)
    tm_out = _pick(T, 512)
    tm_ffn, tf = _pick(S, 512), _pick(layers[0]["w_gate"].shape[1], 512)
    no_lam = jnp.zeros((1,), F32)
    no_norm = jnp.ones((B_DV, 1), F32)
    for l, w in enumerate(layers):
        za, zb, zc = _in_proj(x, w["norm_mix"], w["w_in"], tm_in)
        qt, k, vt = _prep_a(za, tabs_a, B, S, tm_prep)
        oa = _attention(w["lam"], qt, k, vt, w["diff_norm"], n_maps=2, n_q=1, dk=2 * A_DK, dv=A_DV, tq=tq, tk=tk,
                        post_scale=1.0 - w["lam_init"]).reshape(T, -1)
        qt, k, vt = _prep_b(zb, w["mla_q_norm"], w["mla_kv_norm"], w["w_uq"], w["w_ukv"], tabs_b, B, S, tm_prep)
        ob = _attention(no_lam, qt.reshape(B, B_HEADS, 1, B_QK, S), k, vt, no_norm, n_maps=1, n_q=nq_b, dk=B_QK,
                        dv=B_DV, tq=tq, tk=tk, post_scale=1.0).reshape(T, -1)
        oc = _retention(zc, w["lgf"], w["lgb"], w["ret_norm"], tabs_c, B, S, ts)
        x = _out_proj(x, oa, ob, oc, w["w_o"], tm_out)
        x = _ffn(x, w["norm_ffn"], w["w_gate"], w["w_up"], w["conv_w"], w["conv_b"], w["w_down"], norm_final,
                 seq_len=S, tm=tm_ffn, tf=tf, final_norm=(l == len(layers) - 1))
    return x.reshape(B, S, D)


def kernel(x_prompt, x_sample, norm_mix, w_in, diff_lambda, diff_norm, mla_q_norm, mla_kv_norm, w_uq, w_ukv,
           ret_decay_fwd, ret_decay_bwd, ret_norm, w_o, norm_ffn, w_gate, w_up, conv_w, conv_b, w_down, norm_final):
    p = dict(norm_mix=norm_mix, w_in=w_in, diff_lambda=diff_lambda, diff_norm=diff_norm, mla_q_norm=mla_q_norm,
             mla_kv_norm=mla_kv_norm, w_uq=w_uq, w_ukv=w_ukv, ret_decay_fwd=ret_decay_fwd,
             ret_decay_bwd=ret_decay_bwd, ret_norm=ret_norm, w_o=w_o, norm_ffn=norm_ffn, w_gate=w_gate, w_up=w_up,
             conv_w=conv_w, conv_b=conv_b, w_down=w_down)
    layers = [_layer_weights(l, p) for l in range(norm_mix.shape[0])]
    gf = norm_final[None]
    return (_trunk(x_prompt, layers, gf), _trunk(x_sample, layers, gf))
```

```python
import functools
import math

import jax
import jax.numpy as jnp
from jax import lax
from jax.experimental import pallas as pl
from jax.experimental.pallas import tpu as pltpu

F32 = jnp.float32
BF16 = jnp.bfloat16

NORM_EPS = 1e-5
ROPE_THETA = 500000.0
RET_THETA = 10000.0
RET_CHUNK = 128
A_HEADS, A_DK, A_DV = 4, 64, 128
A_ROT = A_DK // 4
B_HEADS, B_Q_RANK, B_KV_RANK, B_NOPE, B_ROPE, B_DV = 6, 512, 256, 128, 64, 128
C_HEADS, C_DK, C_DV = 6, 64, 128
CONV_WIDTH = 3

LANES = 128
SUBLANES = 8
ZA_COLS = 3 * A_HEADS * 2 * A_DK
ZB_COLS = B_Q_RANK + B_KV_RANK + 2 * B_ROPE
ZC_COLS = 2 * C_HEADS * C_DK + 2 * C_HEADS * C_DV
B_QK = 2 * LANES
V_AUG = 2 * SUBLANES
KEY_SUB = 256
VMEM_LIMIT = 56 * 1024 * 1024
LOG2E = math.log2(math.e)


def _cparams(*sem):
    return pltpu.CompilerParams(dimension_semantics=sem, vmem_limit_bytes=VMEM_LIMIT)


def _rms(x, g):
    return x * lax.rsqrt(jnp.mean(x * x, axis=-1, keepdims=True) + NORM_EPS) * g


def _ones_rows(n):
    return jnp.where(lax.broadcasted_iota(jnp.int32, (V_AUG, n), 0) == 0, 1.0, 0.0).astype(BF16)


def _rope_lanes(x, cos, sin_a, sin_b, half):
    return (x * cos + pltpu.roll(x, LANES - half, 1) * sin_a + pltpu.roll(x, half, 1) * sin_b)


def _in_proj_kernel(x_ref, g_ref, w_ref, za_ref, zb_ref, zc_ref):
    h = _rms(x_ref[...], g_ref[...]).astype(BF16)
    za_ref[...] = jnp.dot(h, w_ref[:, :ZA_COLS], preferred_element_type=F32)
    zb_ref[...] = jnp.dot(h, w_ref[:, ZA_COLS:ZA_COLS + ZB_COLS], preferred_element_type=F32)
    zc_ref[...] = jnp.dot(h, w_ref[:, ZA_COLS + ZB_COLS:], preferred_element_type=F32)


def _in_proj(x, g, w, tm):
    T, D = x.shape
    n = w.shape[1]
    return pl.pallas_call(
        _in_proj_kernel,
        grid=(T // tm,),
        in_specs=[
            pl.BlockSpec((tm, D), lambda i: (i, 0)),
            pl.BlockSpec((1, D), lambda i: (0, 0)),
            pl.BlockSpec((D, n), lambda i: (0, 0), pipeline_mode=pl.Buffered(1)),
        ],
        out_specs=[
            pl.BlockSpec((tm, ZA_COLS), lambda i: (i, 0)),
            pl.BlockSpec((tm, ZB_COLS), lambda i: (i, 0)),
            pl.BlockSpec((tm, ZC_COLS), lambda i: (i, 0)),
        ],
        out_shape=[
            jax.ShapeDtypeStruct((T, ZA_COLS), F32),
            jax.ShapeDtypeStruct((T, ZB_COLS), F32),
            jax.ShapeDtypeStruct((T, ZC_COLS), F32),
        ],
        compiler_params=_cparams("parallel"),
        name="in_proj",
    )(x, g, w)


def _prep_a_kernel(z_ref, cos_ref, sa_ref, sb_ref, qt_ref, k_ref, vt_ref):
    cos, sa, sb = cos_ref[...], sa_ref[...], sb_ref[...]
    tm = z_ref.shape[0]
    row = lax.broadcasted_iota(jnp.int32, (LANES, tm), 0)
    hw = 2 * A_DK
    for h in range(A_HEADS):
        q = _rope_lanes(z_ref[:, h * hw:(h + 1) * hw], cos, sa, sb, A_ROT // 2) * (A_DK ** -0.5 * LOG2E)
        qt = q.T
        qt_ref[h, 0] = jnp.where(row < A_DK, qt, 0.0).astype(BF16)
        qt_ref[h, 1] = jnp.where(row >= A_DK, qt, 0.0).astype(BF16)
        k = _rope_lanes(z_ref[:, (A_HEADS + h) * hw:(A_HEADS + h + 1) * hw], cos, sa, sb, A_ROT // 2)
        k_ref[:, h * hw:(h + 1) * hw] = k.astype(BF16)
        v = z_ref[:, (2 * A_HEADS + h) * hw:(2 * A_HEADS + h + 1) * hw]
        vt_ref[h, :A_DV] = v.T.astype(BF16)
        vt_ref[h, A_DV:] = _ones_rows(tm)


def _prep_a(za, tabs, B, S, tm):
    nb = S // tm
    row_map = lambda b, i: (b * nb + i, 0)
    tab_spec = pl.BlockSpec((tm, LANES), lambda b, i: (i, 0))
    return pl.pallas_call(
        _prep_a_kernel,
        grid=(B, nb),
        in_specs=[pl.BlockSpec((tm, ZA_COLS), row_map), tab_spec, tab_spec, tab_spec],
        out_specs=[
            pl.BlockSpec((None, A_HEADS, 2, LANES, tm), lambda b, i: (b, 0, 0, 0, i)),
            pl.BlockSpec((None, tm, A_HEADS * LANES), lambda b, i: (b, i, 0)),
            pl.BlockSpec((None, A_HEADS, A_DV + V_AUG, tm), lambda b, i: (b, 0, 0, i)),
        ],
        out_shape=[
            jax.ShapeDtypeStruct((B, A_HEADS, 2, LANES, S), BF16),
            jax.ShapeDtypeStruct((B, S, A_HEADS * LANES), BF16),
            jax.ShapeDtypeStruct((B, A_HEADS, A_DV + V_AUG, S), BF16),
        ],
        compiler_params=_cparams("parallel", "parallel"),
        name="prep_a",
    )(za, *tabs)


def _prep_b_kernel(z_ref, qn_ref, kvn_ref, wuq_ref, wukv_ref, cos_ref, sa_ref, sb_ref,
                   qt_ref, k_ref, vt_ref):
    cos, sa, sb = cos_ref[...], sa_ref[...], sb_ref[...]
    tm = z_ref.shape[0]
    half = B_ROPE // 2
    scale = (B_NOPE + B_ROPE) ** -0.5 * LOG2E
    lane = lax.broadcasted_iota(jnp.int32, (tm, LANES), 1)
    nope_w = B_HEADS * B_NOPE

    cq = _rms(z_ref[:, :B_Q_RANK], qn_ref[...]).astype(BF16)
    q = jnp.dot(cq, wuq_ref[...], preferred_element_type=F32) * scale
    ckv = _rms(z_ref[:, B_Q_RANK:B_Q_RANK + B_KV_RANK], kvn_ref[...]).astype(BF16)
    kv = jnp.dot(ckv, wukv_ref[...], preferred_element_type=F32)
    kr = _rope_lanes(z_ref[:, B_Q_RANK + B_KV_RANK:ZB_COLS], cos, sa, sb, half).astype(BF16)

    for c in range(B_HEADS // 2):
        qr = _rope_lanes(q[:, nope_w + c * LANES:nope_w + (c + 1) * LANES], cos, sa, sb, half)
        for e in range(2):
            h = 2 * c + e
            tail = qr if e == 0 else pltpu.roll(qr, B_ROPE, 1)
            tail = jnp.where(lane < B_ROPE, tail, 0.0)
            qt_ref[h, :LANES] = q[:, h * B_NOPE:(h + 1) * B_NOPE].T.astype(BF16)
            qt_ref[h, LANES:] = tail.T.astype(BF16)
    for h in range(B_HEADS):
        k_ref[:, h * B_QK:h * B_QK + LANES] = kv[:, h * B_NOPE:(h + 1) * B_NOPE].astype(BF16)
        k_ref[:, h * B_QK + LANES:(h + 1) * B_QK] = kr
        vt_ref[h, :B_DV] = kv[:, nope_w + h * B_DV:nope_w + (h + 1) * B_DV].T.astype(BF16)
        vt_ref[h, B_DV:] = _ones_rows(tm)


def _prep_b(zb, qn, kvn, wuq, wukv, tabs, B, S, tm):
    nb = S // tm
    tab_spec = pl.BlockSpec((tm, LANES), lambda b, i: (i, 0))
    full = lambda a: pl.BlockSpec(a.shape, lambda b, i: (0,) * a.ndim)
    return pl.pallas_call(
        _prep_b_kernel,
        grid=(B, nb),
        in_specs=[pl.BlockSpec((tm, ZB_COLS), lambda b, i: (b * nb + i, 0)),
                  full(qn), full(kvn), full(wuq), full(wukv), tab_spec, tab_spec, tab_spec],
        out_specs=[
            pl.BlockSpec((None, B_HEADS, B_QK, tm), lambda b, i: (b, 0, 0, i)),
            pl.BlockSpec((None, tm, B_HEADS * B_QK), lambda b, i: (b, i, 0)),
            pl.BlockSpec((None, B_HEADS, B_DV + V_AUG, tm), lambda b, i: (b, 0, 0, i)),
        ],
        out_shape=[
            jax.ShapeDtypeStruct((B, B_HEADS, B_QK, S), BF16),
            jax.ShapeDtypeStruct((B, S, B_HEADS * B_QK), BF16),
            jax.ShapeDtypeStruct((B, B_HEADS, B_DV + V_AUG, S), BF16),
        ],
        compiler_params=_cparams("parallel", "parallel"),
        name="prep_b",
    )(zb, qn, kvn, wuq, wukv, *tabs)


def _attn_kernel(lam_ref, qt_ref, k_ref, vt_ref, g_ref, o_ref, *scratch,
                 n_maps, n_q, tq, tk, sub, nk, post_scale):
    chains = [(mi, qi) for mi in range(n_maps) for qi in range(n_q)]
    nch = len(chains)
    dv = o_ref.shape[1]
    nsub = tk // sub
    s_sc = (scratch[0:nch], scratch[nch:2 * nch])
    mx_sc = (scratch[2 * nch:3 * nch], scratch[3 * nch:4 * nch])
    m_sc = scratch[4 * nch:5 * nch]
    acc_sc = scratch[5 * nch:6 * nch]
    for c in range(nch):
        m_sc[c][...] = jnp.full(m_sc[c].shape, -jnp.inf, F32)
        acc_sc[c][...] = jnp.zeros(acc_sc[c].shape, F32)

    def scores_sub(j, slot, r):
        kt = k_ref[pl.ds(pl.multiple_of(j * tk + r * sub, sub), sub), :]
        for c, (mi, qi) in enumerate(chains):
            s = jnp.dot(kt, qt_ref[mi, :, qi * tq:(qi + 1) * tq], preferred_element_type=F32)
            s_sc[slot][c][r * sub:(r + 1) * sub, :] = s
            mx = jnp.max(s, axis=0, keepdims=True)
            mx_sc[slot][c][...] = mx if r == 0 else jnp.maximum(mx_sc[slot][c][...], mx)

    def consume_sub(j, slot, r, stats):
        vt = vt_ref[:, pl.ds(pl.multiple_of(j * tk + r * sub, sub), sub)]
        for c in range(nch):
            m_new, alpha = stats[c]
            p = jnp.exp2(s_sc[slot][c][r * sub:(r + 1) * sub, :] - m_new).astype(BF16)
            pv = jnp.dot(vt, p, preferred_element_type=F32)
            acc_sc[c][...] = (alpha * acc_sc[c][...] if r == 0 else acc_sc[c][...]) + pv

    def tile(j, slot, prefetch):
        stats = []
        for c in range(nch):
            m_old = m_sc[c][...]
            m_new = jnp.maximum(m_old, mx_sc[slot][c][...])
            stats.append((m_new, jnp.exp2(m_old - m_new)))
            m_sc[c][...] = m_new
        for r in range(nsub):
            if prefetch:
                scores_sub(j + 1, 1 - slot, r)
            consume_sub(j, slot, r, stats)

    def looped_pair(t, carry):
        tile(2 * t, 0, True)
        tile(2 * t + 1, 1, True)
        return carry

    for r in range(nsub):
        scores_sub(0, 0, r)
    lax.fori_loop(0, (nk - 1) // 2, looped_pair, 0)
    if nk % 2 == 0:
        tile(nk - 2, 0, True)
    tile(nk - 1, (nk - 1) % 2, False)

    for qi in range(n_q):
        o = acc_sc[qi][:dv] / acc_sc[qi][dv:dv + 1]
        if n_maps == 2:
            o = o - lam_ref[0] * (acc_sc[n_q + qi][:dv] / acc_sc[n_q + qi][dv:dv + 1])
            o = o * lax.rsqrt(jnp.mean(o * o, axis=0, keepdims=True) + NORM_EPS) * g_ref[...] * post_scale
        o_ref[qi * tq:(qi + 1) * tq, :] = o.T.astype(o_ref.dtype)


def _attention(lam, qt, k, vt, g, *, n_maps, n_q, dk, dv, tq, tk, post_scale):
    B, H = qt.shape[0], qt.shape[1]
    S = k.shape[1]
    bq = n_q * tq
    nch = n_maps * n_q
    kern = functools.partial(_attn_kernel, n_maps=n_maps, n_q=n_q, tq=tq, tk=tk, sub=min(tk, KEY_SUB),
                             nk=S // tk, post_scale=post_scale)
    return pl.pallas_call(
        kern,
        grid=(B, H, S // bq),
        in_specs=[
            pl.BlockSpec(memory_space=pltpu.SMEM),
            pl.BlockSpec((None, None, n_maps, dk, bq), lambda b, h, i: (b, h, 0, 0, i)),
            pl.BlockSpec((None, S, dk), lambda b, h, i: (b, 0, h)),
            pl.BlockSpec((None, None, dv + V_AUG, S), lambda b, h, i: (b, h, 0, 0)),
            pl.BlockSpec((dv, 1), lambda b, h, i: (0, 0)),
        ],
        out_specs=pl.BlockSpec((None, bq, dv), lambda b, h, i: (b, i, h)),
        out_shape=jax.ShapeDtypeStruct((B, S, H * dv), BF16),
        scratch_shapes=(
            [pltpu.VMEM((tk, tq), F32)] * (2 * nch)
            + [pltpu.VMEM((1, tq), F32)] * (3 * nch)
            + [pltpu.VMEM((dv + V_AUG, tq), F32)] * nch
        ),
        compiler_params=_cparams("parallel", "parallel", "arbitrary"),
        name="attn_diff" if n_maps == 2 else "attn_mla",
    )(lam, qt, k, vt, g)


def _ret_common(q_ref, k_ref, cos_ref, sa_ref, sb_ref):
    cos, sa, sb = cos_ref[...], sa_ref[...], sb_ref[...]
    q = _rope_lanes(q_ref[...], cos, sa, sb, C_DK // 2)
    k = _rope_lanes(k_ref[...], cos, sa, sb, C_DK // 2) * (C_DK ** -0.5)
    return q, k


def _head_masks():
    r = lax.broadcasted_iota(jnp.int32, (2 * C_DK, 2 * C_DV), 0)
    c = lax.broadcasted_iota(jnp.int32, (2 * C_DK, 2 * C_DV), 1)
    return (r < C_DK) == (c < C_DV)


def _ret_bwd_kernel(lg_ref, q_ref, k_ref, v_ref, cos_ref, sa_ref, sb_ref, o_ref, r_sc, *, nc):
    C = RET_CHUNK

    @pl.when(pl.program_id(2) == 0)
    def _():
        r_sc[...] = jnp.zeros(r_sc.shape, F32)

    q, k = _ret_common(q_ref, k_ref, cos_ref, sa_ref, sb_ref)
    lg = lg_ref[...]
    idx = lax.broadcasted_iota(jnp.int32, (C, 2 * C_DK), 0).astype(F32)
    q_dec = jnp.exp(lg * (C - idx))
    k_dec = jnp.exp(lg * idx)
    decay = jnp.exp(lg * C).T
    bd = _head_masks()

    for t in range(nc):
        off = (nc - 1 - t) * C
        qc = (q[off:off + C] * q_dec).astype(BF16)
        kc = (k[off:off + C] * k_dec)
        vc = v_ref[off:off + C, :].astype(BF16)
        r = r_sc[...]
        o_ref[off:off + C, :] = jnp.dot(qc, r.astype(BF16), preferred_element_type=F32)
        u = jnp.dot(kc.T.astype(BF16), vc, preferred_element_type=F32)
        r_sc[...] = r * decay + jnp.where(bd, u, 0.0)


def _ret_fwd_kernel(lgf_ref, lgb_ref, q_ref, k_ref, v_ref, gate_ref, xb_ref, cos_ref, sa_ref, sb_ref,
                    nrm_ref, o_ref, r_sc, *, nc):
    C = RET_CHUNK

    @pl.when(pl.program_id(2) == 0)
    def _():
        r_sc[...] = jnp.zeros(r_sc.shape, F32)

    q, k = _ret_common(q_ref, k_ref, cos_ref, sa_ref, sb_ref)
    lgf, lgb = lgf_ref[...], lgb_ref[...]
    idx = lax.broadcasted_iota(jnp.int32, (C, 2 * C_DK), 0).astype(F32)
    q_dec = jnp.exp(lgf * (idx + 1.0))
    k_dec = jnp.exp(lgf * (C - 1.0 - idx))
    decay = jnp.exp(lgf * C).T
    bd = _head_masks()
    lane = lax.broadcasted_iota(jnp.int32, (C, 2 * C_DK), 1)
    ii = lax.broadcasted_iota(jnp.int32, (C, C), 0)
    jj = lax.broadcasted_iota(jnp.int32, (C, C), 1)
    dist = (ii - jj).astype(F32)
    dmats = []
    for e in range(2):
        gf = lgf[:, e * C_DK:e * C_DK + 1]
        gb = lgb[:, e * C_DK:e * C_DK + 1]
        dmats.append(jnp.where(dist >= 0, jnp.exp(gf * jnp.maximum(dist, 0.0)),
                               jnp.exp(gb * jnp.maximum(-dist, 0.0))))
    g = nrm_ref[...]

    for t in range(nc):
        off = t * C
        qr = q[off:off + C]
        kr = k[off:off + C]
        vc = v_ref[off:off + C, :].astype(BF16)
        r = r_sc[...]
        cross = jnp.dot((qr * q_dec).astype(BF16), r.astype(BF16), preferred_element_type=F32)
        kb = kr.astype(BF16)
        for e in range(2):
            qe = jnp.where((lane < C_DK) == (e == 0), qr, 0.0).astype(BF16)
            s = lax.dot_general(qe, kb, (((1,), (1,)), ((), ())), preferred_element_type=F32)
            inner = jnp.dot((s * dmats[e]).astype(BF16), vc[:, e * C_DV:(e + 1) * C_DV],
                            preferred_element_type=F32)
            ret = inner + cross[:, e * C_DV:(e + 1) * C_DV] + xb_ref[off:off + C, e * C_DV:(e + 1) * C_DV]
            oc = _rms(ret, g)
            gt = gate_ref[off:off + C, e * C_DV:(e + 1) * C_DV]
            gated = (gt * (1.0 / (1.0 + jnp.exp(-gt)))) * oc
            o_ref[off:off + C, e * C_DV:(e + 1) * C_DV] = gated.astype(o_ref.dtype)
        u = jnp.dot((kr * k_dec).T.astype(BF16), vc, preferred_element_type=F32)
        r_sc[...] = r * decay + jnp.where(bd, u, 0.0)


def _retention(zc, lgf, lgb, nrm, tabs, B, S, ts):
    ns = S // ts
    nc = ts // RET_CHUNK
    pairs = C_HEADS // 2
    qk_blocks = C_HEADS * C_DK // LANES
    lg_spec = pl.BlockSpec((None, 1, LANES), lambda b, p, i: (p, 0, 0))

    def specs(rev):
        pos = (lambda i: ns - 1 - i) if rev else (lambda i: i)
        return dict(
            q=pl.BlockSpec((ts, LANES), lambda b, p, i: (b * ns + pos(i), p)),
            k=pl.BlockSpec((ts, LANES), lambda b, p, i: (b * ns + pos(i), qk_blocks + p)),
            v=pl.BlockSpec((ts, 2 * C_DV), lambda b, p, i: (b * ns + pos(i), qk_blocks + p)),
            gate=pl.BlockSpec((ts, 2 * C_DV), lambda b, p, i: (b * ns + pos(i), qk_blocks + pairs + p)),
            out=pl.BlockSpec((ts, 2 * C_DV), lambda b, p, i: (b * ns + pos(i), p)),
            tab=pl.BlockSpec((ts, LANES), lambda b, p, i: (pos(i), 0)),
        )

    sb_ = specs(True)
    xb = pl.pallas_call(
        functools.partial(_ret_bwd_kernel, nc=nc),
        grid=(B, pairs, ns),
        in_specs=[lg_spec, sb_["q"], sb_["k"], sb_["v"], sb_["tab"], sb_["tab"], sb_["tab"]],
        out_specs=sb_["out"],
        out_shape=jax.ShapeDtypeStruct((B * S, C_HEADS * C_DV), F32),
        scratch_shapes=[pltpu.VMEM((2 * C_DK, 2 * C_DV), F32)],
        compiler_params=_cparams("parallel", "parallel", "arbitrary"),
        name="ret_bwd",
    )(lgb, zc, zc, zc, *tabs)

    sf = specs(False)
    return pl.pallas_call(
        functools.partial(_ret_fwd_kernel, nc=nc),
        grid=(B, pairs, ns),
        in_specs=[lg_spec, lg_spec, sf["q"], sf["k"], sf["v"], sf["gate"], sf["out"],
                  sf["tab"], sf["tab"], sf["tab"], pl.BlockSpec((1, C_DV), lambda b, p, i: (0, 0))],
        out_specs=sf["out"],
        out_shape=jax.ShapeDtypeStruct((B * S, C_HEADS * C_DV), BF16),
        scratch_shapes=[pltpu.VMEM((2 * C_DK, 2 * C_DV), F32)],
        compiler_params=_cparams("parallel", "parallel", "arbitrary"),
        name="ret_fwd",
    )(lgf, lgb, zc, zc, zc, zc, xb, *tabs, nrm)


def _out_proj_kernel(x_ref, oa_ref, ob_ref, oc_ref, w_ref, o_ref):
    na, nb = oa_ref.shape[1], ob_ref.shape[1]
    acc = jnp.dot(oa_ref[...], w_ref[:na], preferred_element_type=F32)
    acc += jnp.dot(ob_ref[...], w_ref[na:na + nb], preferred_element_type=F32)
    acc += jnp.dot(oc_ref[...], w_ref[na + nb:], preferred_element_type=F32)
    o_ref[...] = x_ref[...] + acc


def _out_proj(x, oa, ob, oc, w, tm):
    T, D = x.shape
    row = lambda a: pl.BlockSpec((tm, a.shape[1]), lambda i: (i, 0))
    return pl.pallas_call(
        _out_proj_kernel,
        grid=(T // tm,),
        in_specs=[row(x), row(oa), row(ob), row(oc),
                  pl.BlockSpec(w.shape, lambda i: (0, 0), pipeline_mode=pl.Buffered(1))],
        out_specs=row(x),
        out_shape=jax.ShapeDtypeStruct((T, D), F32),
        compiler_params=_cparams("parallel"),
        name="out_proj",
    )(x, oa, ob, oc, w)


def _ffn_kernel(x_ref, xp_ref, xn_ref, g_ref, wg_ref, wu_ref, cw_ref, cb_ref, wd_ref, gf_ref, o_ref,
                h_sc, acc_sc, *, seq_len, final_norm):
    tm = x_ref.shape[0]
    halo = SUBLANES
    i, j = pl.program_id(0), pl.program_id(1)

    @pl.when(j == 0)
    def _():
        g = g_ref[...]
        h_sc[:halo] = _rms(xp_ref[...], g).astype(BF16)
        h_sc[halo:halo + tm] = _rms(x_ref[...], g).astype(BF16)
        h_sc[halo + tm:] = _rms(xn_ref[...], g).astype(BF16)
        acc_sc[...] = jnp.zeros(acc_sc.shape, F32)

    gate = jnp.dot(h_sc[...], wg_ref[...], preferred_element_type=F32)
    up = jnp.dot(h_sc[halo:halo + tm], wu_ref[...], preferred_element_type=F32)
    ext = tm + 2 * halo
    left = pltpu.roll(gate, 1, 0)[halo:halo + tm]
    right = pltpu.roll(gate, ext - 1, 0)[halo:halo + tm]
    pos = lax.rem(i * tm, seq_len) + lax.broadcasted_iota(jnp.int32, (tm, 1), 0)
    left = jnp.where(pos != 0, left, 0.0)
    right = jnp.where(pos != seq_len - 1, right, 0.0)
    cw = cw_ref[...]
    gc = cb_ref[...] + left * cw[0:1] + gate[halo:halo + tm] * cw[1:2] + right * cw[2:3]
    act = (gc * (1.0 / (1.0 + jnp.exp(-gc)))) * up
    acc_sc[...] += jnp.dot(act.astype(BF16), wd_ref[...], preferred_element_type=F32)

    @pl.when(j == pl.num_programs(1) - 1)
    def _():
        y = x_ref[...] + acc_sc[...]
        if final_norm:
            y = _rms(y, gf_ref[...])
        o_ref[...] = y


def _ffn(x, g, wg, wu, cw, cb, wd, gf, *, seq_len, tm, tf, final_norm):
    T, D = x.shape
    F = wg.shape[1]
    hb = tm // SUBLANES
    last = T // SUBLANES - 1
    kern = functools.partial(_ffn_kernel, seq_len=seq_len, final_norm=final_norm)
    return pl.pallas_call(
        kern,
        grid=(T // tm, F // tf),
        in_specs=[
            pl.BlockSpec((tm, D), lambda i, j: (i, 0)),
            pl.BlockSpec((SUBLANES, D), lambda i, j: (jnp.maximum(i * hb - 1, 0), 0)),
            pl.BlockSpec((SUBLANES, D), lambda i, j: (jnp.minimum((i + 1) * hb, last), 0)),
            pl.BlockSpec((1, D), lambda i, j: (0, 0)),
            pl.BlockSpec((D, tf), lambda i, j: (0, j)),
            pl.BlockSpec((D, tf), lambda i, j: (0, j)),
            pl.BlockSpec((CONV_WIDTH, tf), lambda i, j: (0, j)),
            pl.BlockSpec((1, tf), lambda i, j: (0, j)),
            pl.BlockSpec((tf, D), lambda i, j: (j, 0)),
            pl.BlockSpec((1, D), lambda i, j: (0, 0)),
        ],
        out_specs=pl.BlockSpec((tm, D), lambda i, j: (i, 0)),
        out_shape=jax.ShapeDtypeStruct((T, D), F32),
        scratch_shapes=[pltpu.VMEM((tm + 2 * SUBLANES, D), BF16), pltpu.VMEM((tm, D), F32)],
        compiler_params=_cparams("parallel", "arbitrary"),
        name="ffn",
    )(x, x, x, g, wg, wu, cw, cb, wd, gf)


def _rope_tables(S, theta, n_rot, group):
    half = n_rot // 2
    pos = jnp.arange(S, dtype=F32)
    inv = jnp.power(jnp.float32(theta), -jnp.arange(half, dtype=F32) * 2.0 / n_rot)
    ang = pos[:, None] * inv[None, :]
    cos, sin = jnp.cos(ang), jnp.sin(ang)
    ones = jnp.ones((S, group - n_rot), F32)
    zeros = jnp.zeros((S, half), F32)
    zrest = jnp.zeros((S, group - n_rot), F32)
    reps = LANES // group
    c = jnp.tile(jnp.concatenate([cos, cos, ones], axis=1), (1, reps))
    sa = jnp.tile(jnp.concatenate([-sin, zeros, zrest], axis=1), (1, reps))
    sb = jnp.tile(jnp.concatenate([zeros, sin, zrest], axis=1), (1, reps))
    return c, sa, sb


def _pick(n, pref):
    t = min(n, pref)
    while n % t:
        t //= 2
    return t


def _layer_weights(l, p):
    d = p["w_in"].shape[1]
    cut = ZA_COLS + B_Q_RANK + B_KV_RANK + B_ROPE
    w_in = jnp.concatenate([p["w_in"][l][:, :cut], jnp.zeros((d, B_ROPE), F32), p["w_in"][l][:, cut:]], axis=1)
    wuq = p["w_uq"][l].reshape(B_Q_RANK, B_HEADS, B_NOPE + B_ROPE)
    wuq = jnp.concatenate([wuq[:, :, :B_NOPE].reshape(B_Q_RANK, -1), wuq[:, :, B_NOPE:].reshape(B_Q_RANK, -1)], axis=1)
    wukv = p["w_ukv"][l].reshape(B_KV_RANK, B_HEADS, B_NOPE + B_DV)
    wukv = jnp.concatenate([wukv[:, :, :B_NOPE].reshape(B_KV_RANK, -1), wukv[:, :, B_NOPE:].reshape(B_KV_RANK, -1)], axis=1)
    lp = p["diff_lambda"][l].astype(F32)
    lam_init = 0.8 - 0.6 * math.exp(-0.3 * l)
    lam = jnp.exp(jnp.sum(lp[0] * lp[1])) - jnp.exp(jnp.sum(lp[2] * lp[3])) + lam_init

    def lane_lg(dec):
        lg = jax.nn.log_sigmoid(dec.astype(F32))
        return jnp.repeat(lg, C_DK).reshape(C_HEADS // 2, 1, LANES)

    return dict(
        norm_mix=p["norm_mix"][l][None], w_in=w_in.astype(BF16),
        lam=lam.reshape(1), lam_init=lam_init, diff_norm=p["diff_norm"][l][:, None],
        mla_q_norm=p["mla_q_norm"][l][None], mla_kv_norm=p["mla_kv_norm"][l][None],
        w_uq=wuq.astype(BF16), w_ukv=wukv.astype(BF16),
        lgf=lane_lg(p["ret_decay_fwd"][l]), lgb=lane_lg(p["ret_decay_bwd"][l]), ret_norm=p["ret_norm"][l][None],
        w_o=p["w_o"][l].astype(BF16), norm_ffn=p["norm_ffn"][l][None],
        w_gate=p["w_gate"][l].astype(BF16), w_up=p["w_up"][l].astype(BF16),
        conv_w=p["conv_w"][l], conv_b=p["conv_b"][l][None], w_down=p["w_down"][l].astype(BF16),
    )


def _trunk(x3, layers, norm_final):
    B, S, D = x3.shape
    T = B * S
    x = x3.reshape(T, D)
    tabs_a = _rope_tables(S, ROPE_THETA, A_ROT, A_DK)
    tabs_b = _rope_tables(S, ROPE_THETA, B_ROPE, B_ROPE)
    tabs_c = _rope_tables(S, RET_THETA, C_DK, C_DK)
    tm_in = _pick(T, 256)
    tm_prep = _pick(S, 512)
    tq, tk = _pick(S, 512), _pick(S, 1024)
    nq_b = 2 if S % (2 * tq) == 0 else 1
    ts = _pick(S, 1024)
    tm_out = _pick(T, 512)
    tm_ffn, tf = _pick(S, 512), _pick(layers[0]["w_gate"].shape[1], 512)
    no_lam = jnp.zeros((1,), F32)
    no_norm = jnp.ones((B_DV, 1), F32)
    for l, w in enumerate(layers):
        za, zb, zc = _in_proj(x, w["norm_mix"], w["w_in"], tm_in)
        qt, k, vt = _prep_a(za, tabs_a, B, S, tm_prep)
        oa = _attention(w["lam"], qt, k, vt, w["diff_norm"], n_maps=2, n_q=1, dk=2 * A_DK, dv=A_DV, tq=tq, tk=tk,
                        post_scale=1.0 - w["lam_init"]).reshape(T, -1)
        qt, k, vt = _prep_b(zb, w["mla_q_norm"], w["mla_kv_norm"], w["w_uq"], w["w_ukv"], tabs_b, B, S, tm_prep)
        ob = _attention(no_lam, qt.reshape(B, B_HEADS, 1, B_QK, S), k, vt, no_norm, n_maps=1, n_q=nq_b, dk=B_QK,
                        dv=B_DV, tq=tq, tk=tk, post_scale=1.0).reshape(T, -1)
        oc = _retention(zc, w["lgf"], w["lgb"], w["ret_norm"], tabs_c, B, S, ts)
        x = _out_proj(x, oa, ob, oc, w["w_o"], tm_out)
        x = _ffn(x, w["norm_ffn"], w["w_gate"], w["w_up"], w["conv_w"], w["conv_b"], w["w_down"], norm_final,
                 seq_len=S, tm=tm_ffn, tf=tf, final_norm=(l == len(layers) - 1))
    return x.reshape(B, S, D)


def kernel(x_prompt, x_sample, norm_mix, w_in, diff_lambda, diff_norm, mla_q_norm, mla_kv_norm, w_uq, w_ukv,
           ret_decay_fwd, ret_decay_bwd, ret_norm, w_o, norm_ffn, w_gate, w_up, conv_w, conv_b, w_down, norm_final):
    p = dict(norm_mix=norm_mix, w_in=w_in, diff_lambda=diff_lambda, diff_norm=diff_norm, mla_q_norm=mla_q_norm,
             mla_kv_norm=mla_kv_norm, w_uq=w_uq, w_ukv=w_ukv, ret_decay_fwd=ret_decay_fwd,
             ret_decay_bwd=ret_decay_bwd, ret_norm=ret_norm, w_o=w_o, norm_ffn=norm_ffn, w_gate=w_gate, w_up=w_up,
             conv_w=conv_w, conv_b=conv_b, w_down=w_down)
    layers = [_layer_weights(l, p) for l in range(norm_mix.shape[0])]
    gf = norm_final[None]
    return (_trunk(x_prompt, layers, gf), _trunk(x_sample, layers, gf))
```

```python
import functools
import math

import jax
import jax.numpy as jnp
from jax import lax
from jax.experimental import pallas as pl
from jax.experimental.pallas import tpu as pltpu

F32 = jnp.float32
BF16 = jnp.bfloat16

NORM_EPS = 1e-5
ROPE_THETA = 500000.0
RET_THETA = 10000.0
RET_CHUNK = 128
A_HEADS, A_DK, A_DV = 4, 64, 128
A_ROT = A_DK // 4
B_HEADS, B_Q_RANK, B_KV_RANK, B_NOPE, B_ROPE, B_DV = 6, 512, 256, 128, 64, 128
C_HEADS, C_DK, C_DV = 6, 64, 128
CONV_WIDTH = 3

LANES = 128
SUBLANES = 8
ZA_COLS = 3 * A_HEADS * 2 * A_DK
ZB_COLS = B_Q_RANK + B_KV_RANK + 2 * B_ROPE
ZC_COLS = 2 * C_HEADS * C_DK + 2 * C_HEADS * C_DV
B_QK = 2 * LANES
V_AUG = 2 * SUBLANES
KEY_SUB = 256
ATTN_STEP_QUERIES = 2048
VMEM_LIMIT = 56 * 1024 * 1024
LOG2E = math.log2(math.e)


def _cparams(*sem):
    return pltpu.CompilerParams(dimension_semantics=sem, vmem_limit_bytes=VMEM_LIMIT)


def _rms(x, g):
    return x * lax.rsqrt(jnp.mean(x * x, axis=-1, keepdims=True) + NORM_EPS) * g


def _ones_rows(n):
    return jnp.where(lax.broadcasted_iota(jnp.int32, (V_AUG, n), 0) == 0, 1.0, 0.0).astype(BF16)


def _rope_lanes(x, cos, sin_a, sin_b, half):
    return (x * cos + pltpu.roll(x, LANES - half, 1) * sin_a + pltpu.roll(x, half, 1) * sin_b)


def _in_proj_kernel(x_ref, g_ref, w_ref, za_ref, zb_ref, zc_ref):
    h = _rms(x_ref[...], g_ref[...]).astype(BF16)
    za_ref[...] = jnp.dot(h, w_ref[:, :ZA_COLS], preferred_element_type=F32)
    zb_ref[...] = jnp.dot(h, w_ref[:, ZA_COLS:ZA_COLS + ZB_COLS], preferred_element_type=F32)
    zc_ref[...] = jnp.dot(h, w_ref[:, ZA_COLS + ZB_COLS:], preferred_element_type=F32)


def _in_proj(x, g, w, tm):
    T, D = x.shape
    n = w.shape[1]
    return pl.pallas_call(
        _in_proj_kernel,
        grid=(T // tm,),
        in_specs=[
            pl.BlockSpec((tm, D), lambda i: (i, 0)),
            pl.BlockSpec((1, D), lambda i: (0, 0)),
            pl.BlockSpec((D, n), lambda i: (0, 0), pipeline_mode=pl.Buffered(1)),
        ],
        out_specs=[
            pl.BlockSpec((tm, ZA_COLS), lambda i: (i, 0)),
            pl.BlockSpec((tm, ZB_COLS), lambda i: (i, 0)),
            pl.BlockSpec((tm, ZC_COLS), lambda i: (i, 0)),
        ],
        out_shape=[
            jax.ShapeDtypeStruct((T, ZA_COLS), F32),
            jax.ShapeDtypeStruct((T, ZB_COLS), F32),
            jax.ShapeDtypeStruct((T, ZC_COLS), F32),
        ],
        compiler_params=_cparams("parallel"),
        name="in_proj",
    )(x, g, w)


def _prep_a_kernel(z_ref, cos_ref, sa_ref, sb_ref, qt_ref, k_ref, vt_ref):
    cos, sa, sb = cos_ref[...], sa_ref[...], sb_ref[...]
    tm = z_ref.shape[0]
    row = lax.broadcasted_iota(jnp.int32, (LANES, tm), 0)
    hw = 2 * A_DK
    for h in range(A_HEADS):
        q = _rope_lanes(z_ref[:, h * hw:(h + 1) * hw], cos, sa, sb, A_ROT // 2) * (A_DK ** -0.5 * LOG2E)
        qt = q.T
        qt_ref[h, 0] = jnp.where(row < A_DK, qt, 0.0).astype(BF16)
        qt_ref[h, 1] = jnp.where(row >= A_DK, qt, 0.0).astype(BF16)
        k = _rope_lanes(z_ref[:, (A_HEADS + h) * hw:(A_HEADS + h + 1) * hw], cos, sa, sb, A_ROT // 2)
        k_ref[:, h * hw:(h + 1) * hw] = k.astype(BF16)
        v = z_ref[:, (2 * A_HEADS + h) * hw:(2 * A_HEADS + h + 1) * hw]
        vt_ref[h, :A_DV] = v.T.astype(BF16)
        vt_ref[h, A_DV:] = _ones_rows(tm)


def _prep_a(za, tabs, B, S, tm):
    nb = S // tm
    row_map = lambda b, i: (b * nb + i, 0)
    tab_spec = pl.BlockSpec((tm, LANES), lambda b, i: (i, 0))
    return pl.pallas_call(
        _prep_a_kernel,
        grid=(B, nb),
        in_specs=[pl.BlockSpec((tm, ZA_COLS), row_map), tab_spec, tab_spec, tab_spec],
        out_specs=[
            pl.BlockSpec((None, A_HEADS, 2, LANES, tm), lambda b, i: (b, 0, 0, 0, i)),
            pl.BlockSpec((None, tm, A_HEADS * LANES), lambda b, i: (b, i, 0)),
            pl.BlockSpec((None, A_HEADS, A_DV + V_AUG, tm), lambda b, i: (b, 0, 0, i)),
        ],
        out_shape=[
            jax.ShapeDtypeStruct((B, A_HEADS, 2, LANES, S), BF16),
            jax.ShapeDtypeStruct((B, S, A_HEADS * LANES), BF16),
            jax.ShapeDtypeStruct((B, A_HEADS, A_DV + V_AUG, S), BF16),
        ],
        compiler_params=_cparams("parallel", "parallel"),
        name="prep_a",
    )(za, *tabs)


def _prep_b_kernel(z_ref, qn_ref, kvn_ref, wuq_ref, wukv_ref, cos_ref, sa_ref, sb_ref,
                   qt_ref, k_ref, vt_ref):
    cos, sa, sb = cos_ref[...], sa_ref[...], sb_ref[...]
    tm = z_ref.shape[0]
    half = B_ROPE // 2
    scale = (B_NOPE + B_ROPE) ** -0.5 * LOG2E
    lane = lax.broadcasted_iota(jnp.int32, (tm, LANES), 1)
    nope_w = B_HEADS * B_NOPE

    cq = _rms(z_ref[:, :B_Q_RANK], qn_ref[...]).astype(BF16)
    q = jnp.dot(cq, wuq_ref[...], preferred_element_type=F32) * scale
    ckv = _rms(z_ref[:, B_Q_RANK:B_Q_RANK + B_KV_RANK], kvn_ref[...]).astype(BF16)
    kv = jnp.dot(ckv, wukv_ref[...], preferred_element_type=F32)
    kr = _rope_lanes(z_ref[:, B_Q_RANK + B_KV_RANK:ZB_COLS], cos, sa, sb, half).astype(BF16)

    for c in range(B_HEADS // 2):
        qr = _rope_lanes(q[:, nope_w + c * LANES:nope_w + (c + 1) * LANES], cos, sa, sb, half)
        for e in range(2):
            h = 2 * c + e
            tail = qr if e == 0 else pltpu.roll(qr, B_ROPE, 1)
            tail = jnp.where(lane < B_ROPE, tail, 0.0)
            qt_ref[h, :LANES] = q[:, h * B_NOPE:(h + 1) * B_NOPE].T.astype(BF16)
            qt_ref[h, LANES:] = tail.T.astype(BF16)
    for h in range(B_HEADS):
        k_ref[:, h * B_QK:h * B_QK + LANES] = kv[:, h * B_NOPE:(h + 1) * B_NOPE].astype(BF16)
        k_ref[:, h * B_QK + LANES:(h + 1) * B_QK] = kr
        vt_ref[h, :B_DV] = kv[:, nope_w + h * B_DV:nope_w + (h + 1) * B_DV].T.astype(BF16)
        vt_ref[h, B_DV:] = _ones_rows(tm)


def _prep_b(zb, qn, kvn, wuq, wukv, tabs, B, S, tm):
    nb = S // tm
    tab_spec = pl.BlockSpec((tm, LANES), lambda b, i: (i, 0))
    full = lambda a: pl.BlockSpec(a.shape, lambda b, i: (0,) * a.ndim)
    return pl.pallas_call(
        _prep_b_kernel,
        grid=(B, nb),
        in_specs=[pl.BlockSpec((tm, ZB_COLS), lambda b, i: (b * nb + i, 0)),
                  full(qn), full(kvn), full(wuq), full(wukv), tab_spec, tab_spec, tab_spec],
        out_specs=[
            pl.BlockSpec((None, B_HEADS, B_QK, tm), lambda b, i: (b, 0, 0, i)),
            pl.BlockSpec((None, tm, B_HEADS * B_QK), lambda b, i: (b, i, 0)),
            pl.BlockSpec((None, B_HEADS, B_DV + V_AUG, tm), lambda b, i: (b, 0, 0, i)),
        ],
        out_shape=[
            jax.ShapeDtypeStruct((B, B_HEADS, B_QK, S), BF16),
            jax.ShapeDtypeStruct((B, S, B_HEADS * B_QK), BF16),
            jax.ShapeDtypeStruct((B, B_HEADS, B_DV + V_AUG, S), BF16),
        ],
        compiler_params=_cparams("parallel", "parallel"),
        name="prep_b",
    )(zb, qn, kvn, wuq, wukv, *tabs)


def _attn_kernel(lam_ref, qt_ref, k_ref, vt_ref, g_ref, o_ref, *scratch,
                 n_maps, n_q, n_grp, tq, tk, sub, nk, post_scale):
    chains = [(mi, qi) for mi in range(n_maps) for qi in range(n_q)]
    nch = len(chains)
    dv = o_ref.shape[1]
    nsub = tk // sub
    gq = n_q * tq
    s_sc = (scratch[0:nch], scratch[nch:2 * nch])
    mx_sc = (scratch[2 * nch:3 * nch], scratch[3 * nch:4 * nch])
    m_sc = scratch[4 * nch:5 * nch]
    acc_sc = scratch[5 * nch:6 * nch]

    def reset():
        for c in range(nch):
            m_sc[c][...] = jnp.full(m_sc[c].shape, -jnp.inf, F32)
            acc_sc[c][...] = jnp.zeros(acc_sc[c].shape, F32)

    def q_start(g, qi):
        return pl.multiple_of(g * gq + qi * tq, tq)

    def scores_sub(g, j, slot, r):
        kt = k_ref[pl.ds(pl.multiple_of(j * tk + r * sub, sub), sub), :]
        for c, (mi, qi) in enumerate(chains):
            s = jnp.dot(kt, qt_ref[mi, :, pl.ds(q_start(g, qi), tq)], preferred_element_type=F32)
            s_sc[slot][c][r * sub:(r + 1) * sub, :] = s
            mx = jnp.max(s, axis=0, keepdims=True)
            mx_sc[slot][c][...] = mx if r == 0 else jnp.maximum(mx_sc[slot][c][...], mx)

    def consume_sub(g, j, slot, r, stats):
        vt = vt_ref[:, pl.ds(pl.multiple_of(j * tk + r * sub, sub), sub)]
        for c in range(nch):
            m_new, alpha = stats[c]
            p = jnp.exp2(s_sc[slot][c][r * sub:(r + 1) * sub, :] - m_new).astype(BF16)
            pv = jnp.dot(vt, p, preferred_element_type=F32)
            acc = acc_sc[c]
            acc[...] = (alpha * acc[...] if r == 0 else acc[...]) + pv

    def tile(g, j, slot, nxt):
        stats = []
        for c in range(nch):
            m_old = m_sc[c][...]
            m_new = jnp.maximum(m_old, mx_sc[slot][c][...])
            stats.append((m_new, jnp.exp2(m_old - m_new)))
            m_sc[c][...] = m_new
        for r in range(nsub):
            if nxt is not None:
                scores_sub(nxt[0], nxt[1], 1 - slot, r)
            consume_sub(g, j, slot, r, stats)

    def finalize(g):
        for qi in range(n_q):
            a0 = acc_sc[qi]
            o = a0[:dv] / a0[dv:dv + 1]
            if n_maps == 2:
                a1 = acc_sc[n_q + qi]
                o = o - lam_ref[0] * (a1[:dv] / a1[dv:dv + 1])
                o = o * lax.rsqrt(jnp.mean(o * o, axis=0, keepdims=True) + NORM_EPS) * g_ref[...] * post_scale
            o_ref[pl.ds(q_start(g, qi), tq), :] = o.T.astype(o_ref.dtype)

    def group(g, last):
        def looped_pair(t, carry):
            tile(g, 2 * t, 0, (g, 2 * t + 1))
            tile(g, 2 * t + 1, 1, (g, 2 * t + 2))
            return carry

        lax.fori_loop(0, (nk - 1) // 2, looped_pair, 0)
        if nk % 2 == 0:
            tile(g, nk - 2, 0, (g, nk - 1))
        tile(g, nk - 1, (nk - 1) % 2, None if last else (g + 1, 0))
        finalize(g)
        if not last:
            reset()

    def looped_group(g, carry):
        group(g, False)
        return carry

    reset()
    for r in range(nsub):
        scores_sub(0, 0, 0, r)
    lax.fori_loop(0, n_grp - 1, looped_group, 0)
    group(n_grp - 1, True)


def _attention(lam, qt, k, vt, g, *, n_maps, n_q, dk, dv, tq, tk, post_scale):
    B, H = qt.shape[0], qt.shape[1]
    S = k.shape[1]
    nk = S // tk
    gq = n_q * tq
    n_grp = min(S // gq, ATTN_STEP_QUERIES // gq) if nk % 2 == 0 else 1
    bq = n_grp * gq
    nch = n_maps * n_q
    kern = functools.partial(_attn_kernel, n_maps=n_maps, n_q=n_q, n_grp=n_grp, tq=tq, tk=tk,
                             sub=min(tk, KEY_SUB), nk=nk, post_scale=post_scale)
    return pl.pallas_call(
        kern,
        grid=(B, H, S // bq),
        in_specs=[
            pl.BlockSpec(memory_space=pltpu.SMEM),
            pl.BlockSpec((None, None, n_maps, dk, bq), lambda b, h, i: (b, h, 0, 0, i)),
            pl.BlockSpec((None, S, dk), lambda b, h, i: (b, 0, h)),
            pl.BlockSpec((None, None, dv + V_AUG, S), lambda b, h, i: (b, h, 0, 0)),
            pl.BlockSpec((dv, 1), lambda b, h, i: (0, 0)),
        ],
        out_specs=pl.BlockSpec((None, bq, dv), lambda b, h, i: (b, i, h)),
        out_shape=jax.ShapeDtypeStruct((B, S, H * dv), BF16),
        scratch_shapes=(
            [pltpu.VMEM((tk, tq), F32)] * (2 * nch)
            + [pltpu.VMEM((1, tq), F32)] * (3 * nch)
            + [pltpu.VMEM((dv + V_AUG, tq), F32)] * nch
        ),
        compiler_params=_cparams("parallel", "parallel", "arbitrary"),
        name="attn_diff" if n_maps == 2 else "attn_mla",
    )(lam, qt, k, vt, g)


def _ret_common(q_ref, k_ref, cos_ref, sa_ref, sb_ref):
    cos, sa, sb = cos_ref[...], sa_ref[...], sb_ref[...]
    q = _rope_lanes(q_ref[...], cos, sa, sb, C_DK // 2)
    k = _rope_lanes(k_ref[...], cos, sa, sb, C_DK // 2) * (C_DK ** -0.5)
    return q, k


def _head_masks():
    r = lax.broadcasted_iota(jnp.int32, (2 * C_DK, 2 * C_DV), 0)
    c = lax.broadcasted_iota(jnp.int32, (2 * C_DK, 2 * C_DV), 1)
    return (r < C_DK) == (c < C_DV)


def _ret_bwd_kernel(lg_ref, q_ref, k_ref, v_ref, cos_ref, sa_ref, sb_ref, o_ref, r_sc, *, nc):
    C = RET_CHUNK

    @pl.when(pl.program_id(2) == 0)
    def _():
        r_sc[...] = jnp.zeros(r_sc.shape, F32)

    q, k = _ret_common(q_ref, k_ref, cos_ref, sa_ref, sb_ref)
    lg = lg_ref[...]
    idx = lax.broadcasted_iota(jnp.int32, (C, 2 * C_DK), 0).astype(F32)
    q_dec = jnp.exp(lg * (C - idx))
    k_dec = jnp.exp(lg * idx)
    decay = jnp.exp(lg * C).T
    bd = _head_masks()

    for t in range(nc):
        off = (nc - 1 - t) * C
        qc = (q[off:off + C] * q_dec).astype(BF16)
        kc = (k[off:off + C] * k_dec)
        vc = v_ref[off:off + C, :].astype(BF16)
        r = r_sc[...]
        o_ref[off:off + C, :] = jnp.dot(qc, r.astype(BF16), preferred_element_type=F32)
        u = jnp.dot(kc.T.astype(BF16), vc, preferred_element_type=F32)
        r_sc[...] = r * decay + jnp.where(bd, u, 0.0)


def _ret_fwd_kernel(lgf_ref, lgb_ref, q_ref, k_ref, v_ref, gate_ref, xb_ref, cos_ref, sa_ref, sb_ref,
                    nrm_ref, o_ref, r_sc, *, nc):
    C = RET_CHUNK

    @pl.when(pl.program_id(2) == 0)
    def _():
        r_sc[...] = jnp.zeros(r_sc.shape, F32)

    q, k = _ret_common(q_ref, k_ref, cos_ref, sa_ref, sb_ref)
    lgf, lgb = lgf_ref[...], lgb_ref[...]
    idx = lax.broadcasted_iota(jnp.int32, (C, 2 * C_DK), 0).astype(F32)
    q_dec = jnp.exp(lgf * (idx + 1.0))
    k_dec = jnp.exp(lgf * (C - 1.0 - idx))
    decay = jnp.exp(lgf * C).T
    bd = _head_masks()
    lane = lax.broadcasted_iota(jnp.int32, (C, 2 * C_DK), 1)
    ii = lax.broadcasted_iota(jnp.int32, (C, C), 0)
    jj = lax.broadcasted_iota(jnp.int32, (C, C), 1)
    dist = (ii - jj).astype(F32)
    dmats = []
    for e in range(2):
        gf = lgf[:, e * C_DK:e * C_DK + 1]
        gb = lgb[:, e * C_DK:e * C_DK + 1]
        dmats.append(jnp.where(dist >= 0, jnp.exp(gf * jnp.maximum(dist, 0.0)),
                               jnp.exp(gb * jnp.maximum(-dist, 0.0))))
    g = nrm_ref[...]

    for t in range(nc):
        off = t * C
        qr = q[off:off + C]
        kr = k[off:off + C]
        vc = v_ref[off:off + C, :].astype(BF16)
        r = r_sc[...]
        cross = jnp.dot((qr * q_dec).astype(BF16), r.astype(BF16), preferred_element_type=F32)
        kb = kr.astype(BF16)
        for e in range(2):
            qe = jnp.where((lane < C_DK) == (e == 0), qr, 0.0).astype(BF16)
            s = lax.dot_general(qe, kb, (((1,), (1,)), ((), ())), preferred_element_type=F32)
            inner = jnp.dot((s * dmats[e]).astype(BF16), vc[:, e * C_DV:(e + 1) * C_DV],
                            preferred_element_type=F32)
            ret = inner + cross[:, e * C_DV:(e + 1) * C_DV] + xb_ref[off:off + C, e * C_DV:(e + 1) * C_DV]
            oc = _rms(ret, g)
            gt = gate_ref[off:off + C, e * C_DV:(e + 1) * C_DV]
            gated = (gt * (1.0 / (1.0 + jnp.exp(-gt)))) * oc
            o_ref[off:off + C, e * C_DV:(e + 1) * C_DV] = gated.astype(o_ref.dtype)
        u = jnp.dot((kr * k_dec).T.astype(BF16), vc, preferred_element_type=F32)
        r_sc[...] = r * decay + jnp.where(bd, u, 0.0)


def _retention(zc, lgf, lgb, nrm, tabs, B, S, ts):
    ns = S // ts
    nc = ts // RET_CHUNK
    pairs = C_HEADS // 2
    qk_blocks = C_HEADS * C_DK // LANES
    lg_spec = pl.BlockSpec((None, 1, LANES), lambda b, p, i: (p, 0, 0))

    def specs(rev):
        pos = (lambda i: ns - 1 - i) if rev else (lambda i: i)
        return dict(
            q=pl.BlockSpec((ts, LANES), lambda b, p, i: (b * ns + pos(i), p)),
            k=pl.BlockSpec((ts, LANES), lambda b, p, i: (b * ns + pos(i), qk_blocks + p)),
            v=pl.BlockSpec((ts, 2 * C_DV), lambda b, p, i: (b * ns + pos(i), qk_blocks + p)),
            gate=pl.BlockSpec((ts, 2 * C_DV), lambda b, p, i: (b * ns + pos(i), qk_blocks + pairs + p)),
            out=pl.BlockSpec((ts, 2 * C_DV), lambda b, p, i: (b * ns + pos(i), p)),
            tab=pl.BlockSpec((ts, LANES), lambda b, p, i: (pos(i), 0)),
        )

    sb_ = specs(True)
    xb = pl.pallas_call(
        functools.partial(_ret_bwd_kernel, nc=nc),
        grid=(B, pairs, ns),
        in_specs=[lg_spec, sb_["q"], sb_["k"], sb_["v"], sb_["tab"], sb_["tab"], sb_["tab"]],
        out_specs=sb_["out"],
        out_shape=jax.ShapeDtypeStruct((B * S, C_HEADS * C_DV), F32),
        scratch_shapes=[pltpu.VMEM((2 * C_DK, 2 * C_DV), F32)],
        compiler_params=_cparams("parallel", "parallel", "arbitrary"),
        name="ret_bwd",
    )(lgb, zc, zc, zc, *tabs)

    sf = specs(False)
    return pl.pallas_call(
        functools.partial(_ret_fwd_kernel, nc=nc),
        grid=(B, pairs, ns),
        in_specs=[lg_spec, lg_spec, sf["q"], sf["k"], sf["v"], sf["gate"], sf["out"],
                  sf["tab"], sf["tab"], sf["tab"], pl.BlockSpec((1, C_DV), lambda b, p, i: (0, 0))],
        out_specs=sf["out"],
        out_shape=jax.ShapeDtypeStruct((B * S, C_HEADS * C_DV), BF16),
        scratch_shapes=[pltpu.VMEM((2 * C_DK, 2 * C_DV), F32)],
        compiler_params=_cparams("parallel", "parallel", "arbitrary"),
        name="ret_fwd",
    )(lgf, lgb, zc, zc, zc, zc, xb, *tabs, nrm)


def _out_proj_kernel(x_ref, oa_ref, ob_ref, oc_ref, w_ref, o_ref):
    na, nb = oa_ref.shape[1], ob_ref.shape[1]
    acc = jnp.dot(oa_ref[...], w_ref[:na], preferred_element_type=F32)
    acc += jnp.dot(ob_ref[...], w_ref[na:na + nb], preferred_element_type=F32)
    acc += jnp.dot(oc_ref[...], w_ref[na + nb:], preferred_element_type=F32)
    o_ref[...] = x_ref[...] + acc


def _out_proj(x, oa, ob, oc, w, tm):
    T, D = x.shape
    row = lambda a: pl.BlockSpec((tm, a.shape[1]), lambda i: (i, 0))
    return pl.pallas_call(
        _out_proj_kernel,
        grid=(T // tm,),
        in_specs=[row(x), row(oa), row(ob), row(oc),
                  pl.BlockSpec(w.shape, lambda i: (0, 0), pipeline_mode=pl.Buffered(1))],
        out_specs=row(x),
        out_shape=jax.ShapeDtypeStruct((T, D), F32),
        compiler_params=_cparams("parallel"),
        name="out_proj",
    )(x, oa, ob, oc, w)


def _ffn_kernel(x_ref, xp_ref, xn_ref, g_ref, wg_ref, wu_ref, cw_ref, cb_ref, wd_ref, gf_ref, o_ref,
                h_sc, acc_sc, *, seq_len, final_norm):
    tm = x_ref.shape[0]
    halo = SUBLANES
    i, j = pl.program_id(0), pl.program_id(1)

    @pl.when(j == 0)
    def _():
        g = g_ref[...]
        h_sc[:halo] = _rms(xp_ref[...], g).astype(BF16)
        h_sc[halo:halo + tm] = _rms(x_ref[...], g).astype(BF16)
        h_sc[halo + tm:] = _rms(xn_ref[...], g).astype(BF16)
        acc_sc[...] = jnp.zeros(acc_sc.shape, F32)

    gate = jnp.dot(h_sc[...], wg_ref[...], preferred_element_type=F32)
    up = jnp.dot(h_sc[halo:halo + tm], wu_ref[...], preferred_element_type=F32)
    ext = tm + 2 * halo
    left = pltpu.roll(gate, 1, 0)[halo:halo + tm]
    right = pltpu.roll(gate, ext - 1, 0)[halo:halo + tm]
    pos = lax.rem(i * tm, seq_len) + lax.broadcasted_iota(jnp.int32, (tm, 1), 0)
    left = jnp.where(pos != 0, left, 0.0)
    right = jnp.where(pos != seq_len - 1, right, 0.0)
    cw = cw_ref[...]
    gc = cb_ref[...] + left * cw[0:1] + gate[halo:halo + tm] * cw[1:2] + right * cw[2:3]
    act = (gc * (1.0 / (1.0 + jnp.exp(-gc)))) * up
    acc_sc[...] += jnp.dot(act.astype(BF16), wd_ref[...], preferred_element_type=F32)

    @pl.when(j == pl.num_programs(1) - 1)
    def _():
        y = x_ref[...] + acc_sc[...]
        if final_norm:
            y = _rms(y, gf_ref[...])
        o_ref[...] = y


def _ffn(x, g, wg, wu, cw, cb, wd, gf, *, seq_len, tm, tf, final_norm):
    T, D = x.shape
    F = wg.shape[1]
    hb = tm // SUBLANES
    last = T // SUBLANES - 1
    kern = functools.partial(_ffn_kernel, seq_len=seq_len, final_norm=final_norm)
    return pl.pallas_call(
        kern,
        grid=(T // tm, F // tf),
        in_specs=[
            pl.BlockSpec((tm, D), lambda i, j: (i, 0)),
            pl.BlockSpec((SUBLANES, D), lambda i, j: (jnp.maximum(i * hb - 1, 0), 0)),
            pl.BlockSpec((SUBLANES, D), lambda i, j: (jnp.minimum((i + 1) * hb, last), 0)),
            pl.BlockSpec((1, D), lambda i, j: (0, 0)),
            pl.BlockSpec((D, tf), lambda i, j: (0, j)),
            pl.BlockSpec((D, tf), lambda i, j: (0, j)),
            pl.BlockSpec((CONV_WIDTH, tf), lambda i, j: (0, j)),
            pl.BlockSpec((1, tf), lambda i, j: (0, j)),
            pl.BlockSpec((tf, D), lambda i, j: (j, 0)),
            pl.BlockSpec((1, D), lambda i, j: (0, 0)),
        ],
        out_specs=pl.BlockSpec((tm, D), lambda i, j: (i, 0)),
        out_shape=jax.ShapeDtypeStruct((T, D), F32),
        scratch_shapes=[pltpu.VMEM((tm + 2 * SUBLANES, D), BF16), pltpu.VMEM((tm, D), F32)],
        compiler_params=_cparams("parallel", "arbitrary"),
        name="ffn",
    )(x, x, x, g, wg, wu, cw, cb, wd, gf)


def _rope_tables(S, theta, n_rot, group):
    half = n_rot // 2
    pos = jnp.arange(S, dtype=F32)
    inv = jnp.power(jnp.float32(theta), -jnp.arange(half, dtype=F32) * 2.0 / n_rot)
    ang = pos[:, None] * inv[None, :]
    cos, sin = jnp.cos(ang), jnp.sin(ang)
    ones = jnp.ones((S, group - n_rot), F32)
    zeros = jnp.zeros((S, half), F32)
    zrest = jnp.zeros((S, group - n_rot), F32)
    reps = LANES // group
    c = jnp.tile(jnp.concatenate([cos, cos, ones], axis=1), (1, reps))
    sa = jnp.tile(jnp.concatenate([-sin, zeros, zrest], axis=1), (1, reps))
    sb = jnp.tile(jnp.concatenate([zeros, sin, zrest], axis=1), (1, reps))
    return c, sa, sb


def _pick(n, pref):
    t = min(n, pref)
    while n % t:
        t //= 2
    return t


def _layer_weights(l, p):
    d = p["w_in"].shape[1]
    cut = ZA_COLS + B_Q_RANK + B_KV_RANK + B_ROPE
    w_in = jnp.concatenate([p["w_in"][l][:, :cut], jnp.zeros((d, B_ROPE), F32), p["w_in"][l][:, cut:]], axis=1)
    wuq = p["w_uq"][l].reshape(B_Q_RANK, B_HEADS, B_NOPE + B_ROPE)
    wuq = jnp.concatenate([wuq[:, :, :B_NOPE].reshape(B_Q_RANK, -1), wuq[:, :, B_NOPE:].reshape(B_Q_RANK, -1)], axis=1)
    wukv = p["w_ukv"][l].reshape(B_KV_RANK, B_HEADS, B_NOPE + B_DV)
    wukv = jnp.concatenate([wukv[:, :, :B_NOPE].reshape(B_KV_RANK, -1), wukv[:, :, B_NOPE:].reshape(B_KV_RANK, -1)], axis=1)
    lp = p["diff_lambda"][l].astype(F32)
    lam_init = 0.8 - 0.6 * math.exp(-0.3 * l)
    lam = jnp.exp(jnp.sum(lp[0] * lp[1])) - jnp.exp(jnp.sum(lp[2] * lp[3])) + lam_init

    def lane_lg(dec):
        lg = jax.nn.log_sigmoid(dec.astype(F32))
        return jnp.repeat(lg, C_DK).reshape(C_HEADS // 2, 1, LANES)

    return dict(
        norm_mix=p["norm_mix"][l][None], w_in=w_in.astype(BF16),
        lam=lam.reshape(1), lam_init=lam_init, diff_norm=p["diff_norm"][l][:, None],
        mla_q_norm=p["mla_q_norm"][l][None], mla_kv_norm=p["mla_kv_norm"][l][None],
        w_uq=wuq.astype(BF16), w_ukv=wukv.astype(BF16),
        lgf=lane_lg(p["ret_decay_fwd"][l]), lgb=lane_lg(p["ret_decay_bwd"][l]), ret_norm=p["ret_norm"][l][None],
        w_o=p["w_o"][l].astype(BF16), norm_ffn=p["norm_ffn"][l][None],
        w_gate=p["w_gate"][l].astype(BF16), w_up=p["w_up"][l].astype(BF16),
        conv_w=p["conv_w"][l], conv_b=p["conv_b"][l][None], w_down=p["w_down"][l].astype(BF16),
    )


def _trunk(x3, layers, norm_final):
    B, S, D = x3.shape
    T = B * S
    x = x3.reshape(T, D)
    tabs_a = _rope_tables(S, ROPE_THETA, A_ROT, A_DK)
    tabs_b = _rope_tables(S, ROPE_THETA, B_ROPE, B_ROPE)
    tabs_c = _rope_tables(S, RET_THETA, C_DK, C_DK)
    tm_in = _pick(T, 256)
    tm_prep = _pick(S, 512)
    tq, tk = _pick(S, 512), _pick(S, 1024)
    nq_b = 2 if S % (2 * tq) == 0 else 1
    ts = _pick(S, 1024)
    tm_out = _pick(T, 512)
    tm_ffn, tf = _pick(S, 512), _pick(layers[0]["w_gate"].shape[1], 512)
    no_lam = jnp.zeros((1,), F32)
    no_norm = jnp.ones((B_DV, 1), F32)
    for l, w in enumerate(layers):
        za, zb, zc = _in_proj(x, w["norm_mix"], w["w_in"], tm_in)
        qt, k, vt = _prep_a(za, tabs_a, B, S, tm_prep)
        oa = _attention(w["lam"], qt, k, vt, w["diff_norm"], n_maps=2, n_q=1, dk=2 * A_DK, dv=A_DV, tq=tq, tk=tk,
                        post_scale=1.0 - w["lam_init"]).reshape(T, -1)
        qt, k, vt = _prep_b(zb, w["mla_q_norm"], w["mla_kv_norm"], w["w_uq"], w["w_ukv"], tabs_b, B, S, tm_prep)
        ob = _attention(no_lam, qt.reshape(B, B_HEADS, 1, B_QK, S), k, vt, no_norm, n_maps=1, n_q=nq_b, dk=B_QK,
                        dv=B_DV, tq=tq, tk=tk, post_scale=1.0).reshape(T, -1)
        oc = _retention(zc, w["lgf"], w["lgb"], w["ret_norm"], tabs_c, B, S, ts)
        x = _out_proj(x, oa, ob, oc, w["w_o"], tm_out)
        x = _ffn(x, w["norm_ffn"], w["w_gate"], w["w_up"], w["conv_w"], w["conv_b"], w["w_down"], norm_final,
                 seq_len=S, tm=tm_ffn, tf=tf, final_norm=(l == len(layers) - 1))
    return x.reshape(B, S, D)


def kernel(x_prompt, x_sample, norm_mix, w_in, diff_lambda, diff_norm, mla_q_norm, mla_kv_norm, w_uq, w_ukv,
           ret_decay_fwd, ret_decay_bwd, ret_norm, w_o, norm_ffn, w_gate, w_up, conv_w, conv_b, w_down, norm_final):
    p = dict(norm_mix=norm_mix, w_in=w_in, diff_lambda=diff_lambda, diff_norm=diff_norm, mla_q_norm=mla_q_norm,
             mla_kv_norm=mla_kv_norm, w_uq=w_uq, w_ukv=w_ukv, ret_decay_fwd=ret_decay_fwd,
             ret_decay_bwd=ret_decay_bwd, ret_norm=ret_norm, w_o=w_o, norm_ffn=norm_ffn, w_gate=w_gate, w_up=w_up,
             conv_w=conv_w, conv_b=conv_b, w_down=w_down)
    layers = [_layer_weights(l, p) for l in range(norm_mix.shape[0])]
    gf = norm_final[None]
    return (_trunk(x_prompt, layers, gf), _trunk(x_sample, layers, gf))
```

```python
import functools
import math

import jax
import jax.numpy as jnp
from jax import lax
from jax.experimental import pallas as pl
from jax.experimental.pallas import tpu as pltpu

F32 = jnp.float32
BF16 = jnp.bfloat16

NORM_EPS = 1e-5
ROPE_THETA = 500000.0
RET_THETA = 10000.0
RET_CHUNK = 128
A_HEADS, A_DK, A_DV = 4, 64, 128
A_ROT = A_DK // 4
B_HEADS, B_Q_RANK, B_KV_RANK, B_NOPE, B_ROPE, B_DV = 6, 512, 256, 128, 64, 128
C_HEADS, C_DK, C_DV = 6, 64, 128
CONV_WIDTH = 3

LANES = 128
SUBLANES = 8
ZA_COLS = 3 * A_HEADS * 2 * A_DK
ZB_COLS = B_Q_RANK + B_KV_RANK + 2 * B_ROPE
ZC_COLS = 2 * C_HEADS * C_DK + 2 * C_HEADS * C_DV
B_QK = 2 * LANES
V_AUG = 2 * SUBLANES
KEY_SUB = 256
ATTN_STEP_QUERIES = 2048
NORM_ROWS = 2 * SUBLANES
VMEM_LIMIT = 56 * 1024 * 1024
LOG2E = math.log2(math.e)


def _cparams(*sem):
    return pltpu.CompilerParams(dimension_semantics=sem, vmem_limit_bytes=VMEM_LIMIT)


def _rms(x, g):
    return x * lax.rsqrt(jnp.mean(x * x, axis=-1, keepdims=True) + NORM_EPS) * g


def _ones_rows(n):
    return jnp.where(lax.broadcasted_iota(jnp.int32, (V_AUG, n), 0) == 0, 1.0, 0.0).astype(BF16)


def _rope_lanes(x, cos, sin_a, sin_b, half):
    return (x * cos + pltpu.roll(x, LANES - half, 1) * sin_a + pltpu.roll(x, half, 1) * sin_b)


def _in_proj_kernel(x_ref, g_ref, w_ref, za_ref, zb_ref, zc_ref):
    h = _rms(x_ref[...], g_ref[...]).astype(BF16)
    za_ref[...] = jnp.dot(h, w_ref[:, :ZA_COLS], preferred_element_type=F32)
    zb_ref[...] = jnp.dot(h, w_ref[:, ZA_COLS:ZA_COLS + ZB_COLS], preferred_element_type=F32)
    zc_ref[...] = jnp.dot(h, w_ref[:, ZA_COLS + ZB_COLS:], preferred_element_type=F32)


def _in_proj(x, g, w, tm):
    T, D = x.shape
    n = w.shape[1]
    return pl.pallas_call(
        _in_proj_kernel,
        grid=(T // tm,),
        in_specs=[
            pl.BlockSpec((tm, D), lambda i: (i, 0)),
            pl.BlockSpec((1, D), lambda i: (0, 0)),
            pl.BlockSpec((D, n), lambda i: (0, 0), pipeline_mode=pl.Buffered(1)),
        ],
        out_specs=[
            pl.BlockSpec((tm, ZA_COLS), lambda i: (i, 0)),
            pl.BlockSpec((tm, ZB_COLS), lambda i: (i, 0)),
            pl.BlockSpec((tm, ZC_COLS), lambda i: (i, 0)),
        ],
        out_shape=[
            jax.ShapeDtypeStruct((T, ZA_COLS), F32),
            jax.ShapeDtypeStruct((T, ZB_COLS), F32),
            jax.ShapeDtypeStruct((T, ZC_COLS), F32),
        ],
        compiler_params=_cparams("parallel"),
        name="in_proj",
    )(x, g, w)


def _prep_a_kernel(z_ref, cos_ref, sa_ref, sb_ref, qt_ref, k_ref, vt_ref):
    cos, sa, sb = cos_ref[...], sa_ref[...], sb_ref[...]
    tm = z_ref.shape[0]
    row = lax.broadcasted_iota(jnp.int32, (LANES, tm), 0)
    hw = 2 * A_DK
    for h in range(A_HEADS):
        q = _rope_lanes(z_ref[:, h * hw:(h + 1) * hw], cos, sa, sb, A_ROT // 2) * (A_DK ** -0.5 * LOG2E)
        qt = q.T
        qt_ref[h, 0] = jnp.where(row < A_DK, qt, 0.0).astype(BF16)
        qt_ref[h, 1] = jnp.where(row >= A_DK, qt, 0.0).astype(BF16)
        k = _rope_lanes(z_ref[:, (A_HEADS + h) * hw:(A_HEADS + h + 1) * hw], cos, sa, sb, A_ROT // 2)
        k_ref[:, h * hw:(h + 1) * hw] = k.astype(BF16)
        v = z_ref[:, (2 * A_HEADS + h) * hw:(2 * A_HEADS + h + 1) * hw]
        vt_ref[h, :A_DV] = v.T.astype(BF16)
        vt_ref[h, A_DV:] = _ones_rows(tm)


def _prep_a(za, tabs, B, S, tm):
    nb = S // tm
    row_map = lambda b, i: (b * nb + i, 0)
    tab_spec = pl.BlockSpec((tm, LANES), lambda b, i: (i, 0))
    return pl.pallas_call(
        _prep_a_kernel,
        grid=(B, nb),
        in_specs=[pl.BlockSpec((tm, ZA_COLS), row_map), tab_spec, tab_spec, tab_spec],
        out_specs=[
            pl.BlockSpec((None, A_HEADS, 2, LANES, tm), lambda b, i: (b, 0, 0, 0, i)),
            pl.BlockSpec((None, tm, A_HEADS * LANES), lambda b, i: (b, i, 0)),
            pl.BlockSpec((None, A_HEADS, A_DV + V_AUG, tm), lambda b, i: (b, 0, 0, i)),
        ],
        out_shape=[
            jax.ShapeDtypeStruct((B, A_HEADS, 2, LANES, S), BF16),
            jax.ShapeDtypeStruct((B, S, A_HEADS * LANES), BF16),
            jax.ShapeDtypeStruct((B, A_HEADS, A_DV + V_AUG, S), BF16),
        ],
        compiler_params=_cparams("parallel", "parallel"),
        name="prep_a",
    )(za, *tabs)


def _prep_b_kernel(z_ref, qn_ref, kvn_ref, wuq_ref, wukv_ref, cos_ref, sa_ref, sb_ref,
                   qt_ref, k_ref, vt_ref):
    cos, sa, sb = cos_ref[...], sa_ref[...], sb_ref[...]
    tm = z_ref.shape[0]
    half = B_ROPE // 2
    scale = (B_NOPE + B_ROPE) ** -0.5 * LOG2E
    lane = lax.broadcasted_iota(jnp.int32, (tm, LANES), 1)
    nope_w = B_HEADS * B_NOPE

    cq = _rms(z_ref[:, :B_Q_RANK], qn_ref[...]).astype(BF16)
    q = jnp.dot(cq, wuq_ref[...], preferred_element_type=F32) * scale
    ckv = _rms(z_ref[:, B_Q_RANK:B_Q_RANK + B_KV_RANK], kvn_ref[...]).astype(BF16)
    kv = jnp.dot(ckv, wukv_ref[...], preferred_element_type=F32)
    kr = _rope_lanes(z_ref[:, B_Q_RANK + B_KV_RANK:ZB_COLS], cos, sa, sb, half).astype(BF16)

    for c in range(B_HEADS // 2):
        qr = _rope_lanes(q[:, nope_w + c * LANES:nope_w + (c + 1) * LANES], cos, sa, sb, half)
        for e in range(2):
            h = 2 * c + e
            tail = qr if e == 0 else pltpu.roll(qr, B_ROPE, 1)
            tail = jnp.where(lane < B_ROPE, tail, 0.0)
            qt_ref[h, :LANES] = q[:, h * B_NOPE:(h + 1) * B_NOPE].T.astype(BF16)
            qt_ref[h, LANES:] = tail.T.astype(BF16)
    for h in range(B_HEADS):
        k_ref[:, h * B_QK:h * B_QK + LANES] = kv[:, h * B_NOPE:(h + 1) * B_NOPE].astype(BF16)
        k_ref[:, h * B_QK + LANES:(h + 1) * B_QK] = kr
        vt_ref[h, :B_DV] = kv[:, nope_w + h * B_DV:nope_w + (h + 1) * B_DV].T.astype(BF16)
        vt_ref[h, B_DV:] = _ones_rows(tm)


def _prep_b(zb, qn, kvn, wuq, wukv, tabs, B, S, tm):
    nb = S // tm
    tab_spec = pl.BlockSpec((tm, LANES), lambda b, i: (i, 0))
    full = lambda a: pl.BlockSpec(a.shape, lambda b, i: (0,) * a.ndim)
    return pl.pallas_call(
        _prep_b_kernel,
        grid=(B, nb),
        in_specs=[pl.BlockSpec((tm, ZB_COLS), lambda b, i: (b * nb + i, 0)),
                  full(qn), full(kvn), full(wuq), full(wukv), tab_spec, tab_spec, tab_spec],
        out_specs=[
            pl.BlockSpec((None, B_HEADS, B_QK, tm), lambda b, i: (b, 0, 0, i)),
            pl.BlockSpec((None, tm, B_HEADS * B_QK), lambda b, i: (b, i, 0)),
            pl.BlockSpec((None, B_HEADS, B_DV + V_AUG, tm), lambda b, i: (b, 0, 0, i)),
        ],
        out_shape=[
            jax.ShapeDtypeStruct((B, B_HEADS, B_QK, S), BF16),
            jax.ShapeDtypeStruct((B, S, B_HEADS * B_QK), BF16),
            jax.ShapeDtypeStruct((B, B_HEADS, B_DV + V_AUG, S), BF16),
        ],
        compiler_params=_cparams("parallel", "parallel"),
        name="prep_b",
    )(zb, qn, kvn, wuq, wukv, *tabs)


def _attn_kernel(lam_ref, qt_ref, k_ref, vt_ref, g_ref, o_ref, *scratch,
                 n_maps, n_q, n_grp, tq, tk, sub, nk, post_scale):
    chains = [(mi, qi) for mi in range(n_maps) for qi in range(n_q)]
    nch = len(chains)
    dv = o_ref.shape[1]
    nsub = tk // sub
    gq = n_q * tq
    s_sc = (scratch[0:nch], scratch[nch:2 * nch])
    mx_sc = (scratch[2 * nch:3 * nch], scratch[3 * nch:4 * nch])
    m_sc = scratch[4 * nch:5 * nch]
    acc_sc = scratch[5 * nch:6 * nch]

    def reset():
        for c in range(nch):
            m_sc[c][...] = jnp.full(m_sc[c].shape, -jnp.inf, F32)
            acc_sc[c][...] = jnp.zeros(acc_sc[c].shape, F32)

    def q_start(g, qi):
        return pl.multiple_of(g * gq + qi * tq, tq)

    def scores_sub(g, j, slot, r):
        kt = k_ref[pl.ds(pl.multiple_of(j * tk + r * sub, sub), sub), :]
        for c, (mi, qi) in enumerate(chains):
            s = jnp.dot(kt, qt_ref[mi, :, pl.ds(q_start(g, qi), tq)], preferred_element_type=F32)
            s_sc[slot][c][r * sub:(r + 1) * sub, :] = s
            mx = jnp.max(s, axis=0, keepdims=True)
            mx_sc[slot][c][...] = mx if r == 0 else jnp.maximum(mx_sc[slot][c][...], mx)

    def consume_sub(g, j, slot, r, stats):
        vt = vt_ref[:, pl.ds(pl.multiple_of(j * tk + r * sub, sub), sub)]
        for c in range(nch):
            m_new, alpha = stats[c]
            p = jnp.exp2(s_sc[slot][c][r * sub:(r + 1) * sub, :] - m_new).astype(BF16)
            pv = jnp.dot(vt, p, preferred_element_type=F32)
            acc = acc_sc[c]
            acc[...] = (alpha * acc[...] if r == 0 else acc[...]) + pv

    def tile(g, j, slot, nxt):
        stats = []
        for c in range(nch):
            m_old = m_sc[c][...]
            m_new = jnp.maximum(m_old, mx_sc[slot][c][...])
            stats.append((m_new, jnp.exp2(m_old - m_new)))
            m_sc[c][...] = m_new
        for r in range(nsub):
            if nxt is not None:
                scores_sub(nxt[0], nxt[1], 1 - slot, r)
            consume_sub(g, j, slot, r, stats)

    def finalize(g):
        for qi in range(n_q):
            a0 = acc_sc[qi]
            o = a0[:dv] / a0[dv:dv + 1]
            if n_maps == 2:
                a1 = acc_sc[n_q + qi]
                o = o - lam_ref[0] * (a1[:dv] / a1[dv:dv + 1])
                o = o * lax.rsqrt(jnp.mean(o * o, axis=0, keepdims=True) + NORM_EPS) * g_ref[...] * post_scale
            o_ref[pl.ds(q_start(g, qi), tq), :] = o.T.astype(o_ref.dtype)

    def group(g, last):
        def looped_pair(t, carry):
            tile(g, 2 * t, 0, (g, 2 * t + 1))
            tile(g, 2 * t + 1, 1, (g, 2 * t + 2))
            return carry

        lax.fori_loop(0, (nk - 1) // 2, looped_pair, 0)
        if nk % 2 == 0:
            tile(g, nk - 2, 0, (g, nk - 1))
        tile(g, nk - 1, (nk - 1) % 2, None if last else (g + 1, 0))
        finalize(g)
        if not last:
            reset()

    def looped_group(g, carry):
        group(g, False)
        return carry

    reset()
    for r in range(nsub):
        scores_sub(0, 0, 0, r)
    lax.fori_loop(0, n_grp - 1, looped_group, 0)
    group(n_grp - 1, True)


def _attention(lam, qt, k, vt, g, *, n_maps, n_q, dk, dv, tq, tk, post_scale):
    B, H = qt.shape[0], qt.shape[1]
    S = k.shape[1]
    nk = S // tk
    gq = n_q * tq
    n_grp = min(S // gq, ATTN_STEP_QUERIES // gq) if nk % 2 == 0 else 1
    bq = n_grp * gq
    nch = n_maps * n_q
    kern = functools.partial(_attn_kernel, n_maps=n_maps, n_q=n_q, n_grp=n_grp, tq=tq, tk=tk,
                             sub=min(tk, KEY_SUB), nk=nk, post_scale=post_scale)
    return pl.pallas_call(
        kern,
        grid=(B, H, S // bq),
        in_specs=[
            pl.BlockSpec(memory_space=pltpu.SMEM),
            pl.BlockSpec((None, None, n_maps, dk, bq), lambda b, h, i: (b, h, 0, 0, i)),
            pl.BlockSpec((None, S, dk), lambda b, h, i: (b, 0, h)),
            pl.BlockSpec((None, None, dv + V_AUG, S), lambda b, h, i: (b, h, 0, 0)),
            pl.BlockSpec((dv, 1), lambda b, h, i: (0, 0)),
        ],
        out_specs=pl.BlockSpec((None, bq, dv), lambda b, h, i: (b, i, h)),
        out_shape=jax.ShapeDtypeStruct((B, S, H * dv), BF16),
        scratch_shapes=(
            [pltpu.VMEM((tk, tq), F32)] * (2 * nch)
            + [pltpu.VMEM((1, tq), F32)] * (3 * nch)
            + [pltpu.VMEM((dv + V_AUG, tq), F32)] * nch
        ),
        compiler_params=_cparams("parallel", "parallel", "arbitrary"),
        name="attn_diff" if n_maps == 2 else "attn_mla",
    )(lam, qt, k, vt, g)


def _ret_rope(x_ref, cos_ref, ss_ref, off):
    x = x_ref[off:off + RET_CHUNK, :]
    return x * cos_ref[off:off + RET_CHUNK, :] + pltpu.roll(x, LANES // 2, 1) * ss_ref[off:off + RET_CHUNK, :]


def _lane_head(shape, axis):
    return (lax.broadcasted_iota(jnp.int32, shape, axis) // (C_DK // 2)) % 2


def _head_masks():
    r = _lane_head((2 * C_DK, 2 * C_DV), 0)
    c = lax.broadcasted_iota(jnp.int32, (2 * C_DK, 2 * C_DV), 1) // C_DV
    return r == c


def _ret_bwd_kernel(lg_ref, q_ref, k_ref, v_ref, cos_ref, ss_ref, o_ref, r_sc, *, nc):
    C = RET_CHUNK

    @pl.when(pl.program_id(2) == 0)
    def _():
        r_sc[...] = jnp.zeros(r_sc.shape, F32)

    lg = lg_ref[...]
    idx = lax.broadcasted_iota(jnp.int32, (C, 2 * C_DK), 0).astype(F32)
    q_dec = jnp.exp(lg * (C - idx))
    k_dec = jnp.exp(lg * idx) * (C_DK ** -0.5)
    decay = jnp.exp(lg * C).T
    bd = _head_masks()

    for t in range(nc):
        off = (nc - 1 - t) * C
        qc = (_ret_rope(q_ref, cos_ref, ss_ref, off) * q_dec).astype(BF16)
        kc = _ret_rope(k_ref, cos_ref, ss_ref, off) * k_dec
        vc = v_ref[off:off + C, :].astype(BF16)
        r = r_sc[...]
        o_ref[off:off + C, :] = jnp.dot(qc, r.astype(BF16), preferred_element_type=F32)
        u = jnp.dot(kc.T.astype(BF16), vc, preferred_element_type=F32)
        r_sc[...] = r * decay + jnp.where(bd, u, 0.0)


def _ret_fwd_kernel(lgf_ref, lgb_ref, q_ref, k_ref, v_ref, gate_ref, xb_ref, cos_ref, ss_ref,
                    nrm_ref, o_ref, r_sc, *, nc):
    C = RET_CHUNK

    @pl.when(pl.program_id(2) == 0)
    def _():
        r_sc[...] = jnp.zeros(r_sc.shape, F32)

    lgf, lgb = lgf_ref[...], lgb_ref[...]
    idx = lax.broadcasted_iota(jnp.int32, (C, 2 * C_DK), 0).astype(F32)
    q_dec = jnp.exp(lgf * (idx + 1.0))
    k_dec = jnp.exp(lgf * (C - 1.0 - idx))
    decay = jnp.exp(lgf * C).T
    bd = _head_masks()
    lane_head = _lane_head((C, 2 * C_DK), 1)
    ii = lax.broadcasted_iota(jnp.int32, (C, C), 0)
    jj = lax.broadcasted_iota(jnp.int32, (C, C), 1)
    dist = (ii - jj).astype(F32)
    dmats = []
    for e in range(2):
        lane_e = e * (C_DK // 2)
        gf = lgf[:, lane_e:lane_e + 1]
        gb = lgb[:, lane_e:lane_e + 1]
        dmats.append(jnp.where(dist >= 0, jnp.exp(gf * jnp.maximum(dist, 0.0)),
                               jnp.exp(gb * jnp.maximum(-dist, 0.0))))
    g = nrm_ref[...]

    def chunk_inputs(t):
        off = t * C
        qr = _ret_rope(q_ref, cos_ref, ss_ref, off)
        kr = _ret_rope(k_ref, cos_ref, ss_ref, off) * (C_DK ** -0.5)
        kb = kr.astype(BF16)
        ss = [lax.dot_general(jnp.where(lane_head == e, qr, 0.0).astype(BF16), kb, (((1,), (1,)), ((), ())),
                              preferred_element_type=F32) for e in range(2)]
        return qr, kr, ss

    nxt = chunk_inputs(0)
    for t in range(nc):
        off = t * C
        qr, kr, ss = nxt
        vc = v_ref[off:off + C, :].astype(BF16)
        r = r_sc[...]
        cross = jnp.dot((qr * q_dec).astype(BF16), r.astype(BF16), preferred_element_type=F32)
        if t + 1 < nc:
            nxt = chunk_inputs(t + 1)
        for e in range(2):
            inner = jnp.dot((ss[e] * dmats[e]).astype(BF16), vc[:, e * C_DV:(e + 1) * C_DV],
                            preferred_element_type=F32)
            ret = inner + cross[:, e * C_DV:(e + 1) * C_DV] + xb_ref[off:off + C, e * C_DV:(e + 1) * C_DV]
            oc = _rms(ret, g)
            gt = gate_ref[off:off + C, e * C_DV:(e + 1) * C_DV]
            gated = (gt * (1.0 / (1.0 + jnp.exp(-gt)))) * oc
            o_ref[off:off + C, e * C_DV:(e + 1) * C_DV] = gated.astype(o_ref.dtype)
        u = jnp.dot((kr * k_dec).T.astype(BF16), vc, preferred_element_type=F32)
        r_sc[...] = r * decay + jnp.where(bd, u, 0.0)


def _retention(zc, lgf, lgb, nrm, tabs, B, S, ts):
    ns = S // ts
    nc = ts // RET_CHUNK
    pairs = C_HEADS // 2
    qk_blocks = C_HEADS * C_DK // LANES
    lg_spec = pl.BlockSpec((None, 1, LANES), lambda b, p, i: (p, 0, 0))

    def specs(rev):
        pos = (lambda i: ns - 1 - i) if rev else (lambda i: i)
        return dict(
            q=pl.BlockSpec((ts, LANES), lambda b, p, i: (b * ns + pos(i), p)),
            k=pl.BlockSpec((ts, LANES), lambda b, p, i: (b * ns + pos(i), qk_blocks + p)),
            v=pl.BlockSpec((ts, 2 * C_DV), lambda b, p, i: (b * ns + pos(i), qk_blocks + p)),
            gate=pl.BlockSpec((ts, 2 * C_DV), lambda b, p, i: (b * ns + pos(i), qk_blocks + pairs + p)),
            out=pl.BlockSpec((ts, 2 * C_DV), lambda b, p, i: (b * ns + pos(i), p)),
            tab=pl.BlockSpec((ts, LANES), lambda b, p, i: (pos(i), 0)),
        )

    sb_ = specs(True)
    xb = pl.pallas_call(
        functools.partial(_ret_bwd_kernel, nc=nc),
        grid=(B, pairs, ns),
        in_specs=[lg_spec, sb_["q"], sb_["k"], sb_["v"], sb_["tab"], sb_["tab"]],
        out_specs=sb_["out"],
        out_shape=jax.ShapeDtypeStruct((B * S, C_HEADS * C_DV), F32),
        scratch_shapes=[pltpu.VMEM((2 * C_DK, 2 * C_DV), F32)],
        compiler_params=_cparams("parallel", "parallel", "arbitrary"),
        name="ret_bwd",
    )(lgb, zc, zc, zc, *tabs)

    sf = specs(False)
    return pl.pallas_call(
        functools.partial(_ret_fwd_kernel, nc=nc),
        grid=(B, pairs, ns),
        in_specs=[lg_spec, lg_spec, sf["q"], sf["k"], sf["v"], sf["gate"], sf["out"],
                  sf["tab"], sf["tab"], pl.BlockSpec((1, C_DV), lambda b, p, i: (0, 0))],
        out_specs=sf["out"],
        out_shape=jax.ShapeDtypeStruct((B * S, C_HEADS * C_DV), BF16),
        scratch_shapes=[pltpu.VMEM((2 * C_DK, 2 * C_DV), F32)],
        compiler_params=_cparams("parallel", "parallel", "arbitrary"),
        name="ret_fwd",
    )(lgf, lgb, zc, zc, zc, zc, xb, *tabs, nrm)


def _out_proj_kernel(x_ref, oa_ref, ob_ref, oc_ref, w_ref, o_ref):
    na, nb = oa_ref.shape[1], ob_ref.shape[1]
    acc = jnp.dot(oa_ref[...], w_ref[:na], preferred_element_type=F32)
    acc += jnp.dot(ob_ref[...], w_ref[na:na + nb], preferred_element_type=F32)
    acc += jnp.dot(oc_ref[...], w_ref[na + nb:], preferred_element_type=F32)
    o_ref[...] = x_ref[...] + acc


def _out_proj(x, oa, ob, oc, w, tm):
    T, D = x.shape
    row = lambda a: pl.BlockSpec((tm, a.shape[1]), lambda i: (i, 0))
    return pl.pallas_call(
        _out_proj_kernel,
        grid=(T // tm,),
        in_specs=[row(x), row(oa), row(ob), row(oc),
                  pl.BlockSpec(w.shape, lambda i: (0, 0), pipeline_mode=pl.Buffered(1))],
        out_specs=row(x),
        out_shape=jax.ShapeDtypeStruct((T, D), F32),
        compiler_params=_cparams("parallel"),
        name="out_proj",
    )(x, oa, ob, oc, w)


def _ffn_kernel(x_ref, xp_ref, xn_ref, g_ref, wg_ref, wu_ref, cw_ref, cb_ref, wd_ref, gf_ref, o_ref,
                h_sc, acc_sc, *, seq_len, final_norm):
    tm = x_ref.shape[0]
    halo = SUBLANES
    ext = tm + 2 * halo
    i, j = pl.program_id(0), pl.program_id(1)

    @pl.when(j == 0)
    def _():
        g = g_ref[...]

        def norm_rows(c, carry):
            rows = pl.ds(pl.multiple_of(c * NORM_ROWS, NORM_ROWS), NORM_ROWS)
            h_sc[rows, :] = _rms(x_ref[rows, :], g).astype(BF16)
            return carry

        lax.fori_loop(0, tm // NORM_ROWS, norm_rows, 0, unroll=4)
        h_sc[tm:tm + halo] = _rms(xn_ref[...], g).astype(BF16)
        h_sc[tm + halo:] = _rms(xp_ref[...], g).astype(BF16)
        acc_sc[...] = jnp.zeros(acc_sc.shape, F32)

    gate = jnp.dot(h_sc[...], wg_ref[...], preferred_element_type=F32)
    up = jnp.dot(h_sc[:tm], wu_ref[...], preferred_element_type=F32)
    left = pltpu.roll(gate, 1, 0)[:tm]
    right = pltpu.roll(gate, ext - 1, 0)[:tm]
    pos = lax.rem(i * tm, seq_len) + lax.broadcasted_iota(jnp.int32, (tm, 1), 0)
    left = jnp.where(pos != 0, left, 0.0)
    right = jnp.where(pos != seq_len - 1, right, 0.0)
    cw = cw_ref[...]
    gc = cb_ref[...] + left * cw[0:1] + gate[:tm] * cw[1:2] + right * cw[2:3]
    act = (gc * (1.0 / (1.0 + jnp.exp(-gc)))) * up
    acc_sc[...] += jnp.dot(act.astype(BF16), wd_ref[...], preferred_element_type=F32)

    @pl.when(j == pl.num_programs(1) - 1)
    def _():
        y = x_ref[...] + acc_sc[...]
        if final_norm:
            y = _rms(y, gf_ref[...])
        o_ref[...] = y


def _ffn(x, g, wg, wu, cw, cb, wd, gf, *, seq_len, tm, tf, final_norm):
    T, D = x.shape
    F = wg.shape[1]
    hb = tm // SUBLANES
    last = T // SUBLANES - 1
    kern = functools.partial(_ffn_kernel, seq_len=seq_len, final_norm=final_norm)
    return pl.pallas_call(
        kern,
        grid=(T // tm, F // tf),
        in_specs=[
            pl.BlockSpec((tm, D), lambda i, j: (i, 0)),
            pl.BlockSpec((SUBLANES, D), lambda i, j: (jnp.maximum(i * hb - 1, 0), 0)),
            pl.BlockSpec((SUBLANES, D), lambda i, j: (jnp.minimum((i + 1) * hb, last), 0)),
            pl.BlockSpec((1, D), lambda i, j: (0, 0)),
            pl.BlockSpec((D, tf), lambda i, j: (0, j)),
            pl.BlockSpec((D, tf), lambda i, j: (0, j)),
            pl.BlockSpec((CONV_WIDTH, tf), lambda i, j: (0, j)),
            pl.BlockSpec((1, tf), lambda i, j: (0, j)),
            pl.BlockSpec((tf, D), lambda i, j: (j, 0)),
            pl.BlockSpec((1, D), lambda i, j: (0, 0)),
        ],
        out_specs=pl.BlockSpec((tm, D), lambda i, j: (i, 0)),
        out_shape=jax.ShapeDtypeStruct((T, D), F32),
        scratch_shapes=[pltpu.VMEM((tm + 2 * SUBLANES, D), BF16), pltpu.VMEM((tm, D), F32)],
        compiler_params=_cparams("parallel", "arbitrary"),
        name="ffn",
    )(x, x, x, g, wg, wu, cw, cb, wd, gf)


def _rope_tables(S, theta, n_rot, group):
    half = n_rot // 2
    pos = jnp.arange(S, dtype=F32)
    inv = jnp.power(jnp.float32(theta), -jnp.arange(half, dtype=F32) * 2.0 / n_rot)
    ang = pos[:, None] * inv[None, :]
    cos, sin = jnp.cos(ang), jnp.sin(ang)
    ones = jnp.ones((S, group - n_rot), F32)
    zeros = jnp.zeros((S, half), F32)
    zrest = jnp.zeros((S, group - n_rot), F32)
    reps = LANES // group
    c = jnp.tile(jnp.concatenate([cos, cos, ones], axis=1), (1, reps))
    sa = jnp.tile(jnp.concatenate([-sin, zeros, zrest], axis=1), (1, reps))
    sb = jnp.tile(jnp.concatenate([zeros, sin, zrest], axis=1), (1, reps))
    return c, sa, sb


def _ret_rope_tables(S):
    half = C_DK // 2
    pos = jnp.arange(S, dtype=F32)
    inv = jnp.power(jnp.float32(RET_THETA), -jnp.arange(half, dtype=F32) * 2.0 / C_DK)
    ang = pos[:, None] * inv[None, :]
    cos, sin = jnp.cos(ang), jnp.sin(ang)
    return jnp.tile(cos, (1, 4)), jnp.concatenate([-sin, -sin, sin, sin], axis=1)


def _pick(n, pref):
    t = min(n, pref)
    while n % t:
        t //= 2
    return t


def _layer_weights(l, p):
    d = p["w_in"].shape[1]
    cut = ZA_COLS + B_Q_RANK + B_KV_RANK + B_ROPE
    nqk = 2 * C_HEADS * C_DK
    w_cqk = p["w_in"][l][:, cut:cut + nqk].reshape(d, 2, C_HEADS // 2, 2, 2, C_DK // 2)
    w_cqk = w_cqk.transpose(0, 1, 2, 4, 3, 5).reshape(d, nqk)
    w_in = jnp.concatenate([p["w_in"][l][:, :cut], jnp.zeros((d, B_ROPE), F32), w_cqk,
                            p["w_in"][l][:, cut + nqk:]], axis=1)
    wuq = p["w_uq"][l].reshape(B_Q_RANK, B_HEADS, B_NOPE + B_ROPE)
    wuq = jnp.concatenate([wuq[:, :, :B_NOPE].reshape(B_Q_RANK, -1), wuq[:, :, B_NOPE:].reshape(B_Q_RANK, -1)], axis=1)
    wukv = p["w_ukv"][l].reshape(B_KV_RANK, B_HEADS, B_NOPE + B_DV)
    wukv = jnp.concatenate([wukv[:, :, :B_NOPE].reshape(B_KV_RANK, -1), wukv[:, :, B_NOPE:].reshape(B_KV_RANK, -1)], axis=1)
    lp = p["diff_lambda"][l].astype(F32)
    lam_init = 0.8 - 0.6 * math.exp(-0.3 * l)
    lam = jnp.exp(jnp.sum(lp[0] * lp[1])) - jnp.exp(jnp.sum(lp[2] * lp[3])) + lam_init

    def lane_lg(dec):
        lg = jax.nn.log_sigmoid(dec.astype(F32)).reshape(C_HEADS // 2, 2)
        return jnp.tile(jnp.repeat(lg, C_DK // 2, axis=1), (1, 2)).reshape(C_HEADS // 2, 1, LANES)

    return dict(
        norm_mix=p["norm_mix"][l][None], w_in=w_in.astype(BF16),
        lam=lam.reshape(1), lam_init=lam_init, diff_norm=p["diff_norm"][l][:, None],
        mla_q_norm=p["mla_q_norm"][l][None], mla_kv_norm=p["mla_kv_norm"][l][None],
        w_uq=wuq.astype(BF16), w_ukv=wukv.astype(BF16),
        lgf=lane_lg(p["ret_decay_fwd"][l]), lgb=lane_lg(p["ret_decay_bwd"][l]), ret_norm=p["ret_norm"][l][None],
        w_o=p["w_o"][l].astype(BF16), norm_ffn=p["norm_ffn"][l][None],
        w_gate=p["w_gate"][l].astype(BF16), w_up=p["w_up"][l].astype(BF16),
        conv_w=p["conv_w"][l], conv_b=p["conv_b"][l][None], w_down=p["w_down"][l].astype(BF16),
    )


def _trunk(x3, layers, norm_final):
    B, S, D = x3.shape
    T = B * S
    x = x3.reshape(T, D)
    tabs_a = _rope_tables(S, ROPE_THETA, A_ROT, A_DK)
    tabs_b = _rope_tables(S, ROPE_THETA, B_ROPE, B_ROPE)
    tabs_c = _ret_rope_tables(S)
    tm_in = _pick(T, 256)
    tm_prep = _pick(S, 512)
    tq, tk = _pick(S, 512), _pick(S, 1024)
    nq_b = 2 if S % (2 * tq) == 0 else 1
    ts = _pick(S, 1024)
    tm_out = _pick(T, 512)
    tm_ffn, tf = _pick(S, 512), _pick(layers[0]["w_gate"].shape[1], 512)
    no_lam = jnp.zeros((1,), F32)
    no_norm = jnp.ones((B_DV, 1), F32)
    for l, w in enumerate(layers):
        za, zb, zc = _in_proj(x, w["norm_mix"], w["w_in"], tm_in)
        qt, k, vt = _prep_a(za, tabs_a, B, S, tm_prep)
        oa = _attention(w["lam"], qt, k, vt, w["diff_norm"], n_maps=2, n_q=1, dk=2 * A_DK, dv=A_DV, tq=tq, tk=tk,
                        post_scale=1.0 - w["lam_init"]).reshape(T, -1)
        qt, k, vt = _prep_b(zb, w["mla_q_norm"], w["mla_kv_norm"], w["w_uq"], w["w_ukv"], tabs_b, B, S, tm_prep)
        ob = _attention(no_lam, qt.reshape(B, B_HEADS, 1, B_QK, S), k, vt, no_norm, n_maps=1, n_q=nq_b, dk=B_QK,
                        dv=B_DV, tq=tq, tk=tk, post_scale=1.0).reshape(T, -1)
        oc = _retention(zc, w["lgf"], w["lgb"], w["ret_norm"], tabs_c, B, S, ts)
        x = _out_proj(x, oa, ob, oc, w["w_o"], tm_out)
        x = _ffn(x, w["norm_ffn"], w["w_gate"], w["w_up"], w["conv_w"], w["conv_b"], w["w_down"], norm_final,
                 seq_len=S, tm=tm_ffn, tf=tf, final_norm=(l == len(layers) - 1))
    return x.reshape(B, S, D)


def kernel(x_prompt, x_sample, norm_mix, w_in, diff_lambda, diff_norm, mla_q_norm, mla_kv_norm, w_uq, w_ukv,
           ret_decay_fwd, ret_decay_bwd, ret_norm, w_o, norm_ffn, w_gate, w_up, conv_w, conv_b, w_down, norm_final):
    p = dict(norm_mix=norm_mix, w_in=w_in, diff_lambda=diff_lambda, diff_norm=diff_norm, mla_q_norm=mla_q_norm,
             mla_kv_norm=mla_kv_norm, w_uq=w_uq, w_ukv=w_ukv, ret_decay_fwd=ret_decay_fwd,
             ret_decay_bwd=ret_decay_bwd, ret_norm=ret_norm, w_o=w_o, norm_ffn=norm_ffn, w_gate=w_gate, w_up=w_up,
             conv_w=conv_w, conv_b=conv_b, w_down=w_down)
    layers = [_layer_weights(l, p) for l in range(norm_mix.shape[0])]
    gf = norm_final[None]
    return (_trunk(x_prompt, layers, gf), _trunk(x_sample, layers, gf))
```

```python
import functools
import math

import jax
import jax.numpy as jnp
from jax import lax
from jax.experimental import pallas as pl
from jax.experimental.pallas import tpu as pltpu

F32 = jnp.float32
BF16 = jnp.bfloat16

NORM_EPS = 1e-5
ROPE_THETA = 500000.0
RET_THETA = 10000.0
RET_CHUNK = 128
A_HEADS, A_DK, A_DV = 4, 64, 128
A_ROT = A_DK // 4
B_HEADS, B_Q_RANK, B_KV_RANK, B_NOPE, B_ROPE, B_DV = 6, 512, 256, 128, 64, 128
C_HEADS, C_DK, C_DV = 6, 64, 128
CONV_WIDTH = 3

LANES = 128
SUBLANES = 8
ZA_COLS = 3 * A_HEADS * 2 * A_DK
ZB_COLS = B_Q_RANK + B_KV_RANK + 2 * B_ROPE
ZC_COLS = 2 * C_HEADS * C_DK + 2 * C_HEADS * C_DV
B_QK = 2 * LANES
V_AUG = 2 * SUBLANES
KEY_SUB = 256
ATTN_STEP_QUERIES = 2048
NORM_ROWS = 2 * SUBLANES
VMEM_LIMIT = 56 * 1024 * 1024
LOG2E = math.log2(math.e)


def _cparams(*sem):
    return pltpu.CompilerParams(dimension_semantics=sem, vmem_limit_bytes=VMEM_LIMIT)


def _rms(x, g):
    return x * lax.rsqrt(jnp.mean(x * x, axis=-1, keepdims=True) + NORM_EPS) * g


def _ones_rows(n):
    return jnp.where(lax.broadcasted_iota(jnp.int32, (V_AUG, n), 0) == 0, 1.0, 0.0).astype(BF16)


def _rope_lanes(x, cos, sin_a, sin_b, half):
    return (x * cos + pltpu.roll(x, LANES - half, 1) * sin_a + pltpu.roll(x, half, 1) * sin_b)


def _in_proj_kernel(x_ref, g_ref, w_ref, za_ref, zb_ref, zc_ref):
    h = _rms(x_ref[...], g_ref[...]).astype(BF16)
    za_ref[...] = jnp.dot(h, w_ref[:, :ZA_COLS], preferred_element_type=F32)
    zb_ref[...] = jnp.dot(h, w_ref[:, ZA_COLS:ZA_COLS + ZB_COLS], preferred_element_type=F32)
    zc_ref[...] = jnp.dot(h, w_ref[:, ZA_COLS + ZB_COLS:], preferred_element_type=F32)


def _in_proj(x, g, w, tm):
    T, D = x.shape
    n = w.shape[1]
    return pl.pallas_call(
        _in_proj_kernel,
        grid=(T // tm,),
        in_specs=[
            pl.BlockSpec((tm, D), lambda i: (i, 0)),
            pl.BlockSpec((1, D), lambda i: (0, 0)),
            pl.BlockSpec((D, n), lambda i: (0, 0), pipeline_mode=pl.Buffered(1)),
        ],
        out_specs=[
            pl.BlockSpec((tm, ZA_COLS), lambda i: (i, 0)),
            pl.BlockSpec((tm, ZB_COLS), lambda i: (i, 0)),
            pl.BlockSpec((tm, ZC_COLS), lambda i: (i, 0)),
        ],
        out_shape=[
            jax.ShapeDtypeStruct((T, ZA_COLS), F32),
            jax.ShapeDtypeStruct((T, ZB_COLS), F32),
            jax.ShapeDtypeStruct((T, ZC_COLS), F32),
        ],
        compiler_params=_cparams("parallel"),
        name="in_proj",
    )(x, g, w)


def _prep_a_kernel(z_ref, cos_ref, sa_ref, sb_ref, qt_ref, k_ref, vt_ref):
    cos, sa, sb = cos_ref[...], sa_ref[...], sb_ref[...]
    tm = z_ref.shape[0]
    row = lax.broadcasted_iota(jnp.int32, (LANES, tm), 0)
    hw = 2 * A_DK
    for h in range(A_HEADS):
        q = _rope_lanes(z_ref[:, h * hw:(h + 1) * hw], cos, sa, sb, A_ROT // 2) * (A_DK ** -0.5 * LOG2E)
        qt = q.T
        qt_ref[h, 0] = jnp.where(row < A_DK, qt, 0.0).astype(BF16)
        qt_ref[h, 1] = jnp.where(row >= A_DK, qt, 0.0).astype(BF16)
        k = _rope_lanes(z_ref[:, (A_HEADS + h) * hw:(A_HEADS + h + 1) * hw], cos, sa, sb, A_ROT // 2)
        k_ref[:, h * hw:(h + 1) * hw] = k.astype(BF16)
        v = z_ref[:, (2 * A_HEADS + h) * hw:(2 * A_HEADS + h + 1) * hw]
        vt_ref[h, :A_DV] = v.T.astype(BF16)
        vt_ref[h, A_DV:] = _ones_rows(tm)


def _prep_a(za, tabs, B, S, tm):
    nb = S // tm
    row_map = lambda b, i: (b * nb + i, 0)
    tab_spec = pl.BlockSpec((tm, LANES), lambda b, i: (i, 0))
    return pl.pallas_call(
        _prep_a_kernel,
        grid=(B, nb),
        in_specs=[pl.BlockSpec((tm, ZA_COLS), row_map), tab_spec, tab_spec, tab_spec],
        out_specs=[
            pl.BlockSpec((None, A_HEADS, 2, LANES, tm), lambda b, i: (b, 0, 0, 0, i)),
            pl.BlockSpec((None, tm, A_HEADS * LANES), lambda b, i: (b, i, 0)),
            pl.BlockSpec((None, A_HEADS, A_DV + V_AUG, tm), lambda b, i: (b, 0, 0, i)),
        ],
        out_shape=[
            jax.ShapeDtypeStruct((B, A_HEADS, 2, LANES, S), BF16),
            jax.ShapeDtypeStruct((B, S, A_HEADS * LANES), BF16),
            jax.ShapeDtypeStruct((B, A_HEADS, A_DV + V_AUG, S), BF16),
        ],
        compiler_params=_cparams("parallel", "parallel"),
        name="prep_a",
    )(za, *tabs)


def _prep_b_kernel(z_ref, qn_ref, kvn_ref, wuq_ref, wukv_ref, cos_ref, sa_ref, sb_ref,
                   qt_ref, k_ref, vt_ref):
    cos, sa, sb = cos_ref[...], sa_ref[...], sb_ref[...]
    tm = z_ref.shape[0]
    half = B_ROPE // 2
    scale = (B_NOPE + B_ROPE) ** -0.5 * LOG2E
    lane = lax.broadcasted_iota(jnp.int32, (tm, LANES), 1)
    nope_w = B_HEADS * B_NOPE

    cq = _rms(z_ref[:, :B_Q_RANK], qn_ref[...]).astype(BF16)
    q = jnp.dot(cq, wuq_ref[...], preferred_element_type=F32) * scale
    ckv = _rms(z_ref[:, B_Q_RANK:B_Q_RANK + B_KV_RANK], kvn_ref[...]).astype(BF16)
    kv = jnp.dot(ckv, wukv_ref[...], preferred_element_type=F32)
    kr = _rope_lanes(z_ref[:, B_Q_RANK + B_KV_RANK:ZB_COLS], cos, sa, sb, half).astype(BF16)

    for c in range(B_HEADS // 2):
        qr = _rope_lanes(q[:, nope_w + c * LANES:nope_w + (c + 1) * LANES], cos, sa, sb, half)
        for e in range(2):
            h = 2 * c + e
            tail = qr if e == 0 else pltpu.roll(qr, B_ROPE, 1)
            tail = jnp.where(lane < B_ROPE, tail, 0.0)
            qt_ref[h, :LANES] = q[:, h * B_NOPE:(h + 1) * B_NOPE].T.astype(BF16)
            qt_ref[h, LANES:] = tail.T.astype(BF16)
    for h in range(B_HEADS):
        k_ref[:, h * B_QK:h * B_QK + LANES] = kv[:, h * B_NOPE:(h + 1) * B_NOPE].astype(BF16)
        k_ref[:, h * B_QK + LANES:(h + 1) * B_QK] = kr
        vt_ref[h, :B_DV] = kv[:, nope_w + h * B_DV:nope_w + (h + 1) * B_DV].T.astype(BF16)
        vt_ref[h, B_DV:] = _ones_rows(tm)


def _prep_b(zb, qn, kvn, wuq, wukv, tabs, B, S, tm):
    nb = S // tm
    tab_spec = pl.BlockSpec((tm, LANES), lambda b, i: (i, 0))
    full = lambda a: pl.BlockSpec(a.shape, lambda b, i: (0,) * a.ndim)
    return pl.pallas_call(
        _prep_b_kernel,
        grid=(B, nb),
        in_specs=[pl.BlockSpec((tm, ZB_COLS), lambda b, i: (b * nb + i, 0)),
                  full(qn), full(kvn), full(wuq), full(wukv), tab_spec, tab_spec, tab_spec],
        out_specs=[
            pl.BlockSpec((None, B_HEADS, B_QK, tm), lambda b, i: (b, 0, 0, i)),
            pl.BlockSpec((None, tm, B_HEADS * B_QK), lambda b, i: (b, i, 0)),
            pl.BlockSpec((None, B_HEADS, B_DV + V_AUG, tm), lambda b, i: (b, 0, 0, i)),
        ],
        out_shape=[
            jax.ShapeDtypeStruct((B, B_HEADS, B_QK, S), BF16),
            jax.ShapeDtypeStruct((B, S, B_HEADS * B_QK), BF16),
            jax.ShapeDtypeStruct((B, B_HEADS, B_DV + V_AUG, S), BF16),
        ],
        compiler_params=_cparams("parallel", "parallel"),
        name="prep_b",
    )(zb, qn, kvn, wuq, wukv, *tabs)


def _attn_kernel(lam_ref, qt_ref, k_ref, vt_ref, g_ref, o_ref, *scratch,
                 n_maps, n_q, n_grp, tq, tk, sub, nk, post_scale):
    chains = [(mi, qi) for mi in range(n_maps) for qi in range(n_q)]
    nch = len(chains)
    dv = o_ref.shape[1]
    nsub = tk // sub
    gq = n_q * tq
    s_sc = (scratch[0:nch], scratch[nch:2 * nch])
    mx_sc = (scratch[2 * nch:3 * nch], scratch[3 * nch:4 * nch])
    m_sc = scratch[4 * nch:5 * nch]
    acc_sc = scratch[5 * nch:6 * nch]

    def reset():
        for c in range(nch):
            m_sc[c][...] = jnp.full(m_sc[c].shape, -jnp.inf, F32)
            acc_sc[c][...] = jnp.zeros(acc_sc[c].shape, F32)

    def q_start(g, qi):
        return pl.multiple_of(g * gq + qi * tq, tq)

    def scores_sub(g, j, slot, r):
        kt = k_ref[pl.ds(pl.multiple_of(j * tk + r * sub, sub), sub), :]
        for c, (mi, qi) in enumerate(chains):
            s = jnp.dot(kt, qt_ref[mi, :, pl.ds(q_start(g, qi), tq)], preferred_element_type=F32)
            s_sc[slot][c][r * sub:(r + 1) * sub, :] = s
            mx = jnp.max(s, axis=0, keepdims=True)
            mx_sc[slot][c][...] = mx if r == 0 else jnp.maximum(mx_sc[slot][c][...], mx)

    def consume_sub(g, j, slot, r, stats):
        vt = vt_ref[:, pl.ds(pl.multiple_of(j * tk + r * sub, sub), sub)]
        for c in range(nch):
            m_new, alpha = stats[c]
            p = jnp.exp2(s_sc[slot][c][r * sub:(r + 1) * sub, :] - m_new).astype(BF16)
            pv = jnp.dot(vt, p, preferred_element_type=F32)
            acc = acc_sc[c]
            acc[...] = (alpha * acc[...] if r == 0 else acc[...]) + pv

    def tile(g, j, slot, nxt):
        stats = []
        for c in range(nch):
            m_old = m_sc[c][...]
            m_new = jnp.maximum(m_old, mx_sc[slot][c][...])
            stats.append((m_new, jnp.exp2(m_old - m_new)))
            m_sc[c][...] = m_new
        for r in range(nsub):
            if nxt is not None:
                scores_sub(nxt[0], nxt[1], 1 - slot, r)
            consume_sub(g, j, slot, r, stats)

    def finalize(g):
        for qi in range(n_q):
            a0 = acc_sc[qi]
            o = a0[:dv] / a0[dv:dv + 1]
            if n_maps == 2:
                a1 = acc_sc[n_q + qi]
                o = o - lam_ref[0] * (a1[:dv] / a1[dv:dv + 1])
                o = o * lax.rsqrt(jnp.mean(o * o, axis=0, keepdims=True) + NORM_EPS) * g_ref[...] * post_scale
            o_ref[pl.ds(q_start(g, qi), tq), :] = o.T.astype(o_ref.dtype)

    def group(g, last):
        def looped_pair(t, carry):
            tile(g, 2 * t, 0, (g, 2 * t + 1))
            tile(g, 2 * t + 1, 1, (g, 2 * t + 2))
            return carry

        lax.fori_loop(0, (nk - 1) // 2, looped_pair, 0)
        if nk % 2 == 0:
            tile(g, nk - 2, 0, (g, nk - 1))
        tile(g, nk - 1, (nk - 1) % 2, None if last else (g + 1, 0))
        finalize(g)
        if not last:
            reset()

    def looped_group(g, carry):
        group(g, False)
        return carry

    reset()
    for r in range(nsub):
        scores_sub(0, 0, 0, r)
    lax.fori_loop(0, n_grp - 1, looped_group, 0)
    group(n_grp - 1, True)


def _attention(lam, qt, k, vt, g, *, n_maps, n_q, dk, dv, tq, tk, post_scale):
    B, H = qt.shape[0], qt.shape[1]
    S = k.shape[1]
    nk = S // tk
    gq = n_q * tq
    n_grp = min(S // gq, ATTN_STEP_QUERIES // gq) if nk % 2 == 0 else 1
    bq = n_grp * gq
    nch = n_maps * n_q
    kern = functools.partial(_attn_kernel, n_maps=n_maps, n_q=n_q, n_grp=n_grp, tq=tq, tk=tk,
                             sub=min(tk, KEY_SUB), nk=nk, post_scale=post_scale)
    return pl.pallas_call(
        kern,
        grid=(B, H, S // bq),
        in_specs=[
            pl.BlockSpec(memory_space=pltpu.SMEM),
            pl.BlockSpec((None, None, n_maps, dk, bq), lambda b, h, i: (b, h, 0, 0, i)),
            pl.BlockSpec((None, S, dk), lambda b, h, i: (b, 0, h)),
            pl.BlockSpec((None, None, dv + V_AUG, S), lambda b, h, i: (b, h, 0, 0)),
            pl.BlockSpec((dv, 1), lambda b, h, i: (0, 0)),
        ],
        out_specs=pl.BlockSpec((None, bq, dv), lambda b, h, i: (b, i, h)),
        out_shape=jax.ShapeDtypeStruct((B, S, H * dv), BF16),
        scratch_shapes=(
            [pltpu.VMEM((tk, tq), F32)] * (2 * nch)
            + [pltpu.VMEM((1, tq), F32)] * (3 * nch)
            + [pltpu.VMEM((dv + V_AUG, tq), F32)] * nch
        ),
        compiler_params=_cparams("parallel", "parallel", "arbitrary"),
        name="attn_diff" if n_maps == 2 else "attn_mla",
    )(lam, qt, k, vt, g)


def _ret_rope(x_ref, cos_ref, ss_ref, off):
    x = x_ref[off:off + RET_CHUNK, :]
    return x * cos_ref[off:off + RET_CHUNK, :] + pltpu.roll(x, LANES // 2, 1) * ss_ref[off:off + RET_CHUNK, :]


def _lane_head(shape, axis):
    return (lax.broadcasted_iota(jnp.int32, shape, axis) // (C_DK // 2)) % 2


def _head_masks():
    r = _lane_head((2 * C_DK, 2 * C_DV), 0)
    c = lax.broadcasted_iota(jnp.int32, (2 * C_DK, 2 * C_DV), 1) // C_DV
    return r == c


def _pair_views(p, qk_refs, v_refs):
    qk = [r.at[:, p * LANES:(p + 1) * LANES] for r in qk_refs]
    vs = [r.at[:, p * 2 * C_DV:(p + 1) * 2 * C_DV] for r in v_refs]
    return qk, vs


def _ret_bwd_kernel(lg_ref, q_ref, k_ref, v_ref, cos_ref, ss_ref, o_ref, r_sc, *, nc):
    C = RET_CHUNK
    pairs = C_HEADS // 2

    @pl.when(pl.program_id(1) == 0)
    def _():
        r_sc[...] = jnp.zeros(r_sc.shape, F32)

    idx = lax.broadcasted_iota(jnp.int32, (C, 2 * C_DK), 0).astype(F32)
    bd = _head_masks()
    consts = []
    for p in range(pairs):
        lg = lg_ref[p]
        consts.append((jnp.exp(lg * (C - idx)),
                       jnp.exp(lg * idx) * (C_DK ** -0.5),
                       jnp.exp(lg * C).T))

    for t in range(nc):
        off = (nc - 1 - t) * C
        for p in range(pairs):
            (q_p, k_p), (v_p, o_p) = _pair_views(p, (q_ref, k_ref), (v_ref, o_ref))
            q_dec, k_dec, decay = consts[p]
            qc = (_ret_rope(q_p, cos_ref, ss_ref, off) * q_dec).astype(BF16)
            kc = _ret_rope(k_p, cos_ref, ss_ref, off) * k_dec
            vc = v_p[off:off + C, :].astype(BF16)
            r = r_sc[p]
            o_p[off:off + C, :] = jnp.dot(qc, r.astype(BF16), preferred_element_type=F32)
            u = jnp.dot(kc.T.astype(BF16), vc, preferred_element_type=F32)
            r_sc[p] = r * decay + jnp.where(bd, u, 0.0)


def _ret_fwd_kernel(lgf_ref, lgb_ref, q_ref, k_ref, v_ref, gate_ref, xb_ref, cos_ref, ss_ref,
                    nrm_ref, o_ref, r_sc, *, nc):
    C = RET_CHUNK
    pairs = C_HEADS // 2

    @pl.when(pl.program_id(1) == 0)
    def _():
        r_sc[...] = jnp.zeros(r_sc.shape, F32)

    idx = lax.broadcasted_iota(jnp.int32, (C, 2 * C_DK), 0).astype(F32)
    bd = _head_masks()
    lane_head = _lane_head((C, 2 * C_DK), 1)
    ii = lax.broadcasted_iota(jnp.int32, (C, C), 0)
    jj = lax.broadcasted_iota(jnp.int32, (C, C), 1)
    dist = (ii - jj).astype(F32)
    g = nrm_ref[...]
    consts = []
    for p in range(pairs):
        lgf, lgb = lgf_ref[p], lgb_ref[p]
        dmats = []
        for e in range(2):
            lane_e = e * (C_DK // 2)
            gf = lgf[:, lane_e:lane_e + 1]
            gb = lgb[:, lane_e:lane_e + 1]
            dmats.append(jnp.where(dist >= 0, jnp.exp(gf * jnp.maximum(dist, 0.0)),
                                   jnp.exp(gb * jnp.maximum(-dist, 0.0))))
        consts.append((jnp.exp(lgf * (idx + 1.0)), jnp.exp(lgf * (C - 1.0 - idx)), jnp.exp(lgf * C).T, dmats))

    def chunk_inputs(p, t):
        (q_p, k_p), _ = _pair_views(p, (q_ref, k_ref), ())
        off = t * C
        qr = _ret_rope(q_p, cos_ref, ss_ref, off)
        kr = _ret_rope(k_p, cos_ref, ss_ref, off) * (C_DK ** -0.5)
        kb = kr.astype(BF16)
        ss = [lax.dot_general(jnp.where(lane_head == e, qr, 0.0).astype(BF16), kb, (((1,), (1,)), ((), ())),
                              preferred_element_type=F32) for e in range(2)]
        return qr, kr, ss

    nxt = [chunk_inputs(p, 0) for p in range(pairs)]
    for p in range(pairs):
        for t in range(nc):
            off = t * C
            _, (v_p, gate_p, xb_p, o_p) = _pair_views(p, (), (v_ref, gate_ref, xb_ref, o_ref))
            q_dec, k_dec, decay, dmats = consts[p]
            qr, kr, ss = nxt[p]
            vc = v_p[off:off + C, :].astype(BF16)
            r = r_sc[p]
            cross = jnp.dot((qr * q_dec).astype(BF16), r.astype(BF16), preferred_element_type=F32)
            if t + 1 < nc:
                nxt[p] = chunk_inputs(p, t + 1)
            for e in range(2):
                inner = jnp.dot((ss[e] * dmats[e]).astype(BF16), vc[:, e * C_DV:(e + 1) * C_DV],
                                preferred_element_type=F32)
                ret = inner + cross[:, e * C_DV:(e + 1) * C_DV] + xb_p[off:off + C, e * C_DV:(e + 1) * C_DV]
                oc = _rms(ret, g)
                gt = gate_p[off:off + C, e * C_DV:(e + 1) * C_DV]
                gated = (gt * (1.0 / (1.0 + jnp.exp(-gt)))) * oc
                o_p[off:off + C, e * C_DV:(e + 1) * C_DV] = gated.astype(o_ref.dtype)
            u = jnp.dot((kr * k_dec).T.astype(BF16), vc, preferred_element_type=F32)
            r_sc[p] = r * decay + jnp.where(bd, u, 0.0)


def _retention(zc, lgf, lgb, nrm, tabs, B, S, ts):
    ns = S // ts
    nc = ts // RET_CHUNK
    pairs = C_HEADS // 2
    qk_w, v_w = C_HEADS * C_DK, C_HEADS * C_DV
    lg_spec = pl.BlockSpec((pairs, 1, LANES), lambda b, i: (0, 0, 0))
    state = pltpu.VMEM((pairs, 2 * C_DK, 2 * C_DV), F32)

    def specs(rev):
        pos = (lambda i: ns - 1 - i) if rev else (lambda i: i)
        return dict(
            q=pl.BlockSpec((ts, qk_w), lambda b, i: (b * ns + pos(i), 0)),
            k=pl.BlockSpec((ts, qk_w), lambda b, i: (b * ns + pos(i), 1)),
            v=pl.BlockSpec((ts, v_w), lambda b, i: (b * ns + pos(i), 1)),
            gate=pl.BlockSpec((ts, v_w), lambda b, i: (b * ns + pos(i), 2)),
            out=pl.BlockSpec((ts, v_w), lambda b, i: (b * ns + pos(i), 0)),
            tab=pl.BlockSpec((ts, LANES), lambda b, i: (pos(i), 0)),
        )

    sb_ = specs(True)
    xb = pl.pallas_call(
        functools.partial(_ret_bwd_kernel, nc=nc),
        grid=(B, ns),
        in_specs=[lg_spec, sb_["q"], sb_["k"], sb_["v"], sb_["tab"], sb_["tab"]],
        out_specs=sb_["out"],
        out_shape=jax.ShapeDtypeStruct((B * S, v_w), F32),
        scratch_shapes=[state],
        compiler_params=_cparams("parallel", "arbitrary"),
        name="ret_bwd",
    )(lgb, zc, zc, zc, *tabs)

    sf = specs(False)
    return pl.pallas_call(
        functools.partial(_ret_fwd_kernel, nc=nc),
        grid=(B, ns),
        in_specs=[lg_spec, lg_spec, sf["q"], sf["k"], sf["v"], sf["gate"], sf["out"],
                  sf["tab"], sf["tab"], pl.BlockSpec((1, C_DV), lambda b, i: (0, 0))],
        out_specs=sf["out"],
        out_shape=jax.ShapeDtypeStruct((B * S, v_w), BF16),
        scratch_shapes=[state],
        compiler_params=_cparams("parallel", "arbitrary"),
        name="ret_fwd",
    )(lgf, lgb, zc, zc, zc, zc, xb, *tabs, nrm)


def _out_proj_kernel(x_ref, oa_ref, ob_ref, oc_ref, w_ref, o_ref):
    na, nb = oa_ref.shape[1], ob_ref.shape[1]
    acc = jnp.dot(oa_ref[...], w_ref[:na], preferred_element_type=F32)
    acc += jnp.dot(ob_ref[...], w_ref[na:na + nb], preferred_element_type=F32)
    acc += jnp.dot(oc_ref[...], w_ref[na + nb:], preferred_element_type=F32)
    o_ref[...] = x_ref[...] + acc


def _out_proj(x, oa, ob, oc, w, tm):
    T, D = x.shape
    row = lambda a: pl.BlockSpec((tm, a.shape[1]), lambda i: (i, 0))
    return pl.pallas_call(
        _out_proj_kernel,
        grid=(T // tm,),
        in_specs=[row(x), row(oa), row(ob), row(oc),
                  pl.BlockSpec(w.shape, lambda i: (0, 0), pipeline_mode=pl.Buffered(1))],
        out_specs=row(x),
        out_shape=jax.ShapeDtypeStruct((T, D), F32),
        compiler_params=_cparams("parallel"),
        name="out_proj",
    )(x, oa, ob, oc, w)


def _ffn_kernel(x_ref, xp_ref, xn_ref, g_ref, wg_ref, wu_ref, cw_ref, cb_ref, wd_ref, gf_ref, o_ref,
                h_sc, acc_sc, *, seq_len, final_norm):
    tm = x_ref.shape[0]
    halo = SUBLANES
    ext = tm + 2 * halo
    i, j = pl.program_id(0), pl.program_id(1)

    @pl.when(j == 0)
    def _():
        g = g_ref[...]

        def norm_rows(c, carry):
            rows = pl.ds(pl.multiple_of(c * NORM_ROWS, NORM_ROWS), NORM_ROWS)
            h_sc[rows, :] = _rms(x_ref[rows, :], g).astype(BF16)
            return carry

        lax.fori_loop(0, tm // NORM_ROWS, norm_rows, 0, unroll=16)
        h_sc[tm:tm + halo] = _rms(xn_ref[...], g).astype(BF16)
        h_sc[tm + halo:] = _rms(xp_ref[...], g).astype(BF16)
        acc_sc[...] = jnp.zeros(acc_sc.shape, F32)

    gate = jnp.dot(h_sc[...], wg_ref[...], preferred_element_type=F32)
    up = jnp.dot(h_sc[:tm], wu_ref[...], preferred_element_type=F32)
    left = pltpu.roll(gate, 1, 0)[:tm]
    right = pltpu.roll(gate, ext - 1, 0)[:tm]
    pos = lax.rem(i * tm, seq_len) + lax.broadcasted_iota(jnp.int32, (tm, 1), 0)
    left = jnp.where(pos != 0, left, 0.0)
    right = jnp.where(pos != seq_len - 1, right, 0.0)
    cw = cw_ref[...]
    gc = cb_ref[...] + left * cw[0:1] + gate[:tm] * cw[1:2] + right * cw[2:3]
    act = (gc * (1.0 / (1.0 + jnp.exp(-gc)))) * up
    acc_sc[...] += jnp.dot(act.astype(BF16), wd_ref[...], preferred_element_type=F32)

    @pl.when(j == pl.num_programs(1) - 1)
    def _():
        y = x_ref[...] + acc_sc[...]
        if final_norm:
            y = _rms(y, gf_ref[...])
        o_ref[...] = y


def _ffn(x, g, wg, wu, cw, cb, wd, gf, *, seq_len, tm, tf, final_norm):
    T, D = x.shape
    F = wg.shape[1]
    hb = tm // SUBLANES
    last = T // SUBLANES - 1
    kern = functools.partial(_ffn_kernel, seq_len=seq_len, final_norm=final_norm)
    return pl.pallas_call(
        kern,
        grid=(T // tm, F // tf),
        in_specs=[
            pl.BlockSpec((tm, D), lambda i, j: (i, 0)),
            pl.BlockSpec((SUBLANES, D), lambda i, j: (jnp.maximum(i * hb - 1, 0), 0)),
            pl.BlockSpec((SUBLANES, D), lambda i, j: (jnp.minimum((i + 1) * hb, last), 0)),
            pl.BlockSpec((1, D), lambda i, j: (0, 0)),
            pl.BlockSpec((D, tf), lambda i, j: (0, j)),
            pl.BlockSpec((D, tf), lambda i, j: (0, j)),
            pl.BlockSpec((CONV_WIDTH, tf), lambda i, j: (0, j)),
            pl.BlockSpec((1, tf), lambda i, j: (0, j)),
            pl.BlockSpec((tf, D), lambda i, j: (j, 0)),
            pl.BlockSpec((1, D), lambda i, j: (0, 0)),
        ],
        out_specs=pl.BlockSpec((tm, D), lambda i, j: (i, 0)),
        out_shape=jax.ShapeDtypeStruct((T, D), F32),
        scratch_shapes=[pltpu.VMEM((tm + 2 * SUBLANES, D), BF16), pltpu.VMEM((tm, D), F32)],
        compiler_params=_cparams("parallel", "arbitrary"),
        name="ffn",
    )(x, x, x, g, wg, wu, cw, cb, wd, gf)


def _rope_tables(S, theta, n_rot, group):
    half = n_rot // 2
    pos = jnp.arange(S, dtype=F32)
    inv = jnp.power(jnp.float32(theta), -jnp.arange(half, dtype=F32) * 2.0 / n_rot)
    ang = pos[:, None] * inv[None, :]
    cos, sin = jnp.cos(ang), jnp.sin(ang)
    ones = jnp.ones((S, group - n_rot), F32)
    zeros = jnp.zeros((S, half), F32)
    zrest = jnp.zeros((S, group - n_rot), F32)
    reps = LANES // group
    c = jnp.tile(jnp.concatenate([cos, cos, ones], axis=1), (1, reps))
    sa = jnp.tile(jnp.concatenate([-sin, zeros, zrest], axis=1), (1, reps))
    sb = jnp.tile(jnp.concatenate([zeros, sin, zrest], axis=1), (1, reps))
    return c, sa, sb


def _ret_rope_tables(S):
    half = C_DK // 2
    pos = jnp.arange(S, dtype=F32)
    inv = jnp.power(jnp.float32(RET_THETA), -jnp.arange(half, dtype=F32) * 2.0 / C_DK)
    ang = pos[:, None] * inv[None, :]
    cos, sin = jnp.cos(ang), jnp.sin(ang)
    return jnp.tile(cos, (1, 4)), jnp.concatenate([-sin, -sin, sin, sin], axis=1)


def _pick(n, pref):
    t = min(n, pref)
    while n % t:
        t //= 2
    return t


def _layer_weights(l, p):
    d = p["w_in"].shape[1]
    cut = ZA_COLS + B_Q_RANK + B_KV_RANK + B_ROPE
    nqk = 2 * C_HEADS * C_DK
    w_cqk = p["w_in"][l][:, cut:cut + nqk].reshape(d, 2, C_HEADS // 2, 2, 2, C_DK // 2)
    w_cqk = w_cqk.transpose(0, 1, 2, 4, 3, 5).reshape(d, nqk)
    w_in = jnp.concatenate([p["w_in"][l][:, :cut], jnp.zeros((d, B_ROPE), F32), w_cqk,
                            p["w_in"][l][:, cut + nqk:]], axis=1)
    wuq = p["w_uq"][l].reshape(B_Q_RANK, B_HEADS, B_NOPE + B_ROPE)
    wuq = jnp.concatenate([wuq[:, :, :B_NOPE].reshape(B_Q_RANK, -1), wuq[:, :, B_NOPE:].reshape(B_Q_RANK, -1)], axis=1)
    wukv = p["w_ukv"][l].reshape(B_KV_RANK, B_HEADS, B_NOPE + B_DV)
    wukv = jnp.concatenate([wukv[:, :, :B_NOPE].reshape(B_KV_RANK, -1), wukv[:, :, B_NOPE:].reshape(B_KV_RANK, -1)], axis=1)
    lp = p["diff_lambda"][l].astype(F32)
    lam_init = 0.8 - 0.6 * math.exp(-0.3 * l)
    lam = jnp.exp(jnp.sum(lp[0] * lp[1])) - jnp.exp(jnp.sum(lp[2] * lp[3])) + lam_init

    def lane_lg(dec):
        lg = jax.nn.log_sigmoid(dec.astype(F32)).reshape(C_HEADS // 2, 2)
        return jnp.tile(jnp.repeat(lg, C_DK // 2, axis=1), (1, 2)).reshape(C_HEADS // 2, 1, LANES)

    return dict(
        norm_mix=p["norm_mix"][l][None], w_in=w_in.astype(BF16),
        lam=lam.reshape(1), lam_init=lam_init, diff_norm=p["diff_norm"][l][:, None],
        mla_q_norm=p["mla_q_norm"][l][None], mla_kv_norm=p["mla_kv_norm"][l][None],
        w_uq=wuq.astype(BF16), w_ukv=wukv.astype(BF16),
        lgf=lane_lg(p["ret_decay_fwd"][l]), lgb=lane_lg(p["ret_decay_bwd"][l]), ret_norm=p["ret_norm"][l][None],
        w_o=p["w_o"][l].astype(BF16), norm_ffn=p["norm_ffn"][l][None],
        w_gate=p["w_gate"][l].astype(BF16), w_up=p["w_up"][l].astype(BF16),
        conv_w=p["conv_w"][l], conv_b=p["conv_b"][l][None], w_down=p["w_down"][l].astype(BF16),
    )


def _trunk(x3, layers, norm_final):
    B, S, D = x3.shape
    T = B * S
    x = x3.reshape(T, D)
    tabs_a = _rope_tables(S, ROPE_THETA, A_ROT, A_DK)
    tabs_b = _rope_tables(S, ROPE_THETA, B_ROPE, B_ROPE)
    tabs_c = _ret_rope_tables(S)
    tm_in = _pick(T, 256)
    tm_prep = _pick(S, 512)
    tq, tk = _pick(S, 512), _pick(S, 1024)
    nq_b = 2 if S % (2 * tq) == 0 else 1
    ts = _pick(S, 1024)
    tm_out = _pick(T, 512)
    tm_ffn, tf = _pick(S, 512), _pick(layers[0]["w_gate"].shape[1], 512)
    no_lam = jnp.zeros((1,), F32)
    no_norm = jnp.ones((B_DV, 1), F32)
    for l, w in enumerate(layers):
        za, zb, zc = _in_proj(x, w["norm_mix"], w["w_in"], tm_in)
        qt, k, vt = _prep_a(za, tabs_a, B, S, tm_prep)
        oa = _attention(w["lam"], qt, k, vt, w["diff_norm"], n_maps=2, n_q=1, dk=2 * A_DK, dv=A_DV, tq=tq, tk=tk,
                        post_scale=1.0 - w["lam_init"]).reshape(T, -1)
        qt, k, vt = _prep_b(zb, w["mla_q_norm"], w["mla_kv_norm"], w["w_uq"], w["w_ukv"], tabs_b, B, S, tm_prep)
        ob = _attention(no_lam, qt.reshape(B, B_HEADS, 1, B_QK, S), k, vt, no_norm, n_maps=1, n_q=nq_b, dk=B_QK,
                        dv=B_DV, tq=tq, tk=tk, post_scale=1.0).reshape(T, -1)
        oc = _retention(zc, w["lgf"], w["lgb"], w["ret_norm"], tabs_c, B, S, ts)
        x = _out_proj(x, oa, ob, oc, w["w_o"], tm_out)
        x = _ffn(x, w["norm_ffn"], w["w_gate"], w["w_up"], w["conv_w"], w["conv_b"], w["w_down"], norm_final,
                 seq_len=S, tm=tm_ffn, tf=tf, final_norm=(l == len(layers) - 1))
    return x.reshape(B, S, D)


def kernel(x_prompt, x_sample, norm_mix, w_in, diff_lambda, diff_norm, mla_q_norm, mla_kv_norm, w_uq, w_ukv,
           ret_decay_fwd, ret_decay_bwd, ret_norm, w_o, norm_ffn, w_gate, w_up, conv_w, conv_b, w_down, norm_final):
    p = dict(norm_mix=norm_mix, w_in=w_in, diff_lambda=diff_lambda, diff_norm=diff_norm, mla_q_norm=mla_q_norm,
             mla_kv_norm=mla_kv_norm, w_uq=w_uq, w_ukv=w_ukv, ret_decay_fwd=ret_decay_fwd,
             ret_decay_bwd=ret_decay_bwd, ret_norm=ret_norm, w_o=w_o, norm_ffn=norm_ffn, w_gate=w_gate, w_up=w_up,
             conv_w=conv_w, conv_b=conv_b, w_down=w_down)
    layers = [_layer_weights(l, p) for l in range(norm_mix.shape[0])]
    gf = norm_final[None]
    return (_trunk(x_prompt, layers, gf), _trunk(x_sample, layers, gf))
```

```python
import functools
import math

import jax
import jax.numpy as jnp
from jax import lax
from jax.experimental import pallas as pl
from jax.experimental.pallas import tpu as pltpu

F32 = jnp.float32
BF16 = jnp.bfloat16

NORM_EPS = 1e-5
ROPE_THETA = 500000.0
RET_THETA = 10000.0
RET_CHUNK = 128
A_HEADS, A_DK, A_DV = 4, 64, 128
A_ROT = A_DK // 4
B_HEADS, B_Q_RANK, B_KV_RANK, B_NOPE, B_ROPE, B_DV = 6, 512, 256, 128, 64, 128
C_HEADS, C_DK, C_DV = 6, 64, 128
CONV_WIDTH = 3

LANES = 128
SUBLANES = 8
ZA_COLS = 3 * A_HEADS * 2 * A_DK
ZB_COLS = B_Q_RANK + B_KV_RANK + 2 * B_ROPE
ZC_COLS = 2 * C_HEADS * C_DK + 2 * C_HEADS * C_DV
B_QK = 2 * LANES
V_AUG = 2 * SUBLANES
KEY_SUB = 256
ATTN_STEP_QUERIES = 2048
NORM_ROWS = 2 * SUBLANES
FFN_COLS = 512
VMEM_LIMIT = 56 * 1024 * 1024
LOG2E = math.log2(math.e)


def _cparams(*sem):
    return pltpu.CompilerParams(dimension_semantics=sem, vmem_limit_bytes=VMEM_LIMIT)


def _rms(x, g):
    return x * lax.rsqrt(jnp.mean(x * x, axis=-1, keepdims=True) + NORM_EPS) * g


def _ones_rows(n):
    return jnp.where(lax.broadcasted_iota(jnp.int32, (V_AUG, n), 0) == 0, 1.0, 0.0).astype(BF16)


def _rope_lanes(x, cos, sin_a, sin_b, half):
    return (x * cos + pltpu.roll(x, LANES - half, 1) * sin_a + pltpu.roll(x, half, 1) * sin_b)


def _in_proj_kernel(x_ref, g_ref, w_ref, za_ref, zb_ref, zc_ref):
    h = _rms(x_ref[...], g_ref[...]).astype(BF16)
    za_ref[...] = jnp.dot(h, w_ref[:, :ZA_COLS], preferred_element_type=F32)
    zb_ref[...] = jnp.dot(h, w_ref[:, ZA_COLS:ZA_COLS + ZB_COLS], preferred_element_type=F32)
    zc_ref[...] = jnp.dot(h, w_ref[:, ZA_COLS + ZB_COLS:], preferred_element_type=F32)


def _in_proj(x, g, w, tm):
    T, D = x.shape
    n = w.shape[1]
    return pl.pallas_call(
        _in_proj_kernel,
        grid=(T // tm,),
        in_specs=[
            pl.BlockSpec((tm, D), lambda i: (i, 0)),
            pl.BlockSpec((1, D), lambda i: (0, 0)),
            pl.BlockSpec((D, n), lambda i: (0, 0), pipeline_mode=pl.Buffered(1)),
        ],
        out_specs=[
            pl.BlockSpec((tm, ZA_COLS), lambda i: (i, 0)),
            pl.BlockSpec((tm, ZB_COLS), lambda i: (i, 0)),
            pl.BlockSpec((tm, ZC_COLS), lambda i: (i, 0)),
        ],
        out_shape=[
            jax.ShapeDtypeStruct((T, ZA_COLS), F32),
            jax.ShapeDtypeStruct((T, ZB_COLS), F32),
            jax.ShapeDtypeStruct((T, ZC_COLS), F32),
        ],
        compiler_params=_cparams("parallel"),
        name="in_proj",
    )(x, g, w)


def _prep_a_kernel(z_ref, cos_ref, sa_ref, sb_ref, qt_ref, k_ref, vt_ref):
    cos, sa, sb = cos_ref[...], sa_ref[...], sb_ref[...]
    tm = z_ref.shape[0]
    row = lax.broadcasted_iota(jnp.int32, (LANES, tm), 0)
    hw = 2 * A_DK
    for h in range(A_HEADS):
        q = _rope_lanes(z_ref[:, h * hw:(h + 1) * hw], cos, sa, sb, A_ROT // 2) * (A_DK ** -0.5 * LOG2E)
        qt = q.T
        qt_ref[h, 0] = jnp.where(row < A_DK, qt, 0.0).astype(BF16)
        qt_ref[h, 1] = jnp.where(row >= A_DK, qt, 0.0).astype(BF16)
        k = _rope_lanes(z_ref[:, (A_HEADS + h) * hw:(A_HEADS + h + 1) * hw], cos, sa, sb, A_ROT // 2)
        k_ref[:, h * hw:(h + 1) * hw] = k.astype(BF16)
        v = z_ref[:, (2 * A_HEADS + h) * hw:(2 * A_HEADS + h + 1) * hw]
        vt_ref[h, :A_DV] = v.T.astype(BF16)
        vt_ref[h, A_DV:] = _ones_rows(tm)


def _prep_a(za, tabs, B, S, tm):
    nb = S // tm
    row_map = lambda b, i: (b * nb + i, 0)
    tab_spec = pl.BlockSpec((tm, LANES), lambda b, i: (i, 0))
    return pl.pallas_call(
        _prep_a_kernel,
        grid=(B, nb),
        in_specs=[pl.BlockSpec((tm, ZA_COLS), row_map), tab_spec, tab_spec, tab_spec],
        out_specs=[
            pl.BlockSpec((None, A_HEADS, 2, LANES, tm), lambda b, i: (b, 0, 0, 0, i)),
            pl.BlockSpec((None, tm, A_HEADS * LANES), lambda b, i: (b, i, 0)),
            pl.BlockSpec((None, A_HEADS, A_DV + V_AUG, tm), lambda b, i: (b, 0, 0, i)),
        ],
        out_shape=[
            jax.ShapeDtypeStruct((B, A_HEADS, 2, LANES, S), BF16),
            jax.ShapeDtypeStruct((B, S, A_HEADS * LANES), BF16),
            jax.ShapeDtypeStruct((B, A_HEADS, A_DV + V_AUG, S), BF16),
        ],
        compiler_params=_cparams("parallel", "parallel"),
        name="prep_a",
    )(za, *tabs)


def _prep_b_kernel(z_ref, qn_ref, kvn_ref, wuq_ref, wukv_ref, cos_ref, sa_ref, sb_ref,
                   qt_ref, k_ref, vt_ref):
    cos, sa, sb = cos_ref[...], sa_ref[...], sb_ref[...]
    tm = z_ref.shape[0]
    half = B_ROPE // 2
    scale = (B_NOPE + B_ROPE) ** -0.5 * LOG2E
    lane = lax.broadcasted_iota(jnp.int32, (tm, LANES), 1)
    nope_w = B_HEADS * B_NOPE

    cq = _rms(z_ref[:, :B_Q_RANK], qn_ref[...]).astype(BF16)
    q = jnp.dot(cq, wuq_ref[...], preferred_element_type=F32) * scale
    ckv = _rms(z_ref[:, B_Q_RANK:B_Q_RANK + B_KV_RANK], kvn_ref[...]).astype(BF16)
    kv = jnp.dot(ckv, wukv_ref[...], preferred_element_type=F32)
    kr = _rope_lanes(z_ref[:, B_Q_RANK + B_KV_RANK:ZB_COLS], cos, sa, sb, half).astype(BF16)

    for c in range(B_HEADS // 2):
        qr = _rope_lanes(q[:, nope_w + c * LANES:nope_w + (c + 1) * LANES], cos, sa, sb, half)
        for e in range(2):
            h = 2 * c + e
            tail = qr if e == 0 else pltpu.roll(qr, B_ROPE, 1)
            tail = jnp.where(lane < B_ROPE, tail, 0.0)
            qt_ref[h, :LANES] = q[:, h * B_NOPE:(h + 1) * B_NOPE].T.astype(BF16)
            qt_ref[h, LANES:] = tail.T.astype(BF16)
    for h in range(B_HEADS):
        k_ref[:, h * B_QK:h * B_QK + LANES] = kv[:, h * B_NOPE:(h + 1) * B_NOPE].astype(BF16)
        k_ref[:, h * B_QK + LANES:(h + 1) * B_QK] = kr
        vt_ref[h, :B_DV] = kv[:, nope_w + h * B_DV:nope_w + (h + 1) * B_DV].T.astype(BF16)
        vt_ref[h, B_DV:] = _ones_rows(tm)


def _prep_b(zb, qn, kvn, wuq, wukv, tabs, B, S, tm):
    nb = S // tm
    tab_spec = pl.BlockSpec((tm, LANES), lambda b, i: (i, 0))
    full = lambda a: pl.BlockSpec(a.shape, lambda b, i: (0,) * a.ndim)
    return pl.pallas_call(
        _prep_b_kernel,
        grid=(B, nb),
        in_specs=[pl.BlockSpec((tm, ZB_COLS), lambda b, i: (b * nb + i, 0)),
                  full(qn), full(kvn), full(wuq), full(wukv), tab_spec, tab_spec, tab_spec],
        out_specs=[
            pl.BlockSpec((None, B_HEADS, B_QK, tm), lambda b, i: (b, 0, 0, i)),
            pl.BlockSpec((None, tm, B_HEADS * B_QK), lambda b, i: (b, i, 0)),
            pl.BlockSpec((None, B_HEADS, B_DV + V_AUG, tm), lambda b, i: (b, 0, 0, i)),
        ],
        out_shape=[
            jax.ShapeDtypeStruct((B, B_HEADS, B_QK, S), BF16),
            jax.ShapeDtypeStruct((B, S, B_HEADS * B_QK), BF16),
            jax.ShapeDtypeStruct((B, B_HEADS, B_DV + V_AUG, S), BF16),
        ],
        compiler_params=_cparams("parallel", "parallel"),
        name="prep_b",
    )(zb, qn, kvn, wuq, wukv, *tabs)


def _attn_kernel(lam_ref, qt_ref, k_ref, vt_ref, g_ref, o_ref, *scratch,
                 n_maps, n_q, n_grp, tq, tk, sub, nk, post_scale):
    chains = [(mi, qi) for mi in range(n_maps) for qi in range(n_q)]
    nch = len(chains)
    dv = o_ref.shape[1]
    nsub = tk // sub
    gq = n_q * tq
    s_sc = (scratch[0:nch], scratch[nch:2 * nch])
    mx_sc = (scratch[2 * nch:3 * nch], scratch[3 * nch:4 * nch])
    m_sc = scratch[4 * nch:5 * nch]
    acc_sc = scratch[5 * nch:6 * nch]

    def reset():
        for c in range(nch):
            m_sc[c][...] = jnp.full(m_sc[c].shape, -jnp.inf, F32)
            acc_sc[c][...] = jnp.zeros(acc_sc[c].shape, F32)

    def q_start(g, qi):
        return pl.multiple_of(g * gq + qi * tq, tq)

    def scores_sub(g, j, slot, r):
        kt = k_ref[pl.ds(pl.multiple_of(j * tk + r * sub, sub), sub), :]
        for c, (mi, qi) in enumerate(chains):
            s = jnp.dot(kt, qt_ref[mi, :, pl.ds(q_start(g, qi), tq)], preferred_element_type=F32)
            s_sc[slot][c][r * sub:(r + 1) * sub, :] = s
            mx = jnp.max(s, axis=0, keepdims=True)
            mx_sc[slot][c][...] = mx if r == 0 else jnp.maximum(mx_sc[slot][c][...], mx)

    def consume_sub(g, j, slot, r, stats):
        vt = vt_ref[:, pl.ds(pl.multiple_of(j * tk + r * sub, sub), sub)]
        for c in range(nch):
            m_new, alpha = stats[c]
            p = jnp.exp2(s_sc[slot][c][r * sub:(r + 1) * sub, :] - m_new).astype(BF16)
            pv = jnp.dot(vt, p, preferred_element_type=F32)
            acc = acc_sc[c]
            acc[...] = (alpha * acc[...] if r == 0 else acc[...]) + pv

    def tile(g, j, slot, nxt):
        stats = []
        for c in range(nch):
            m_old = m_sc[c][...]
            m_new = jnp.maximum(m_old, mx_sc[slot][c][...])
            stats.append((m_new, jnp.exp2(m_old - m_new)))
            m_sc[c][...] = m_new
        for r in range(nsub):
            if nxt is not None:
                scores_sub(nxt[0], nxt[1], 1 - slot, r)
            consume_sub(g, j, slot, r, stats)

    def finalize(g):
        for qi in range(n_q):
            a0 = acc_sc[qi]
            o = a0[:dv] / a0[dv:dv + 1]
            if n_maps == 2:
                a1 = acc_sc[n_q + qi]
                o = o - lam_ref[0] * (a1[:dv] / a1[dv:dv + 1])
                o = o * lax.rsqrt(jnp.mean(o * o, axis=0, keepdims=True) + NORM_EPS) * g_ref[...] * post_scale
            o_ref[pl.ds(q_start(g, qi), tq), :] = o.T.astype(o_ref.dtype)

    def group(g, last):
        def looped_pair(t, carry):
            tile(g, 2 * t, 0, (g, 2 * t + 1))
            tile(g, 2 * t + 1, 1, (g, 2 * t + 2))
            return carry

        lax.fori_loop(0, (nk - 1) // 2, looped_pair, 0)
        if nk % 2 == 0:
            tile(g, nk - 2, 0, (g, nk - 1))
        tile(g, nk - 1, (nk - 1) % 2, None if last else (g + 1, 0))
        finalize(g)
        if not last:
            reset()

    def looped_group(g, carry):
        group(g, False)
        return carry

    reset()
    for r in range(nsub):
        scores_sub(0, 0, 0, r)
    lax.fori_loop(0, n_grp - 1, looped_group, 0)
    group(n_grp - 1, True)


def _attention(lam, qt, k, vt, g, *, n_maps, n_q, dk, dv, tq, tk, post_scale):
    B, H = qt.shape[0], qt.shape[1]
    S = k.shape[1]
    nk = S // tk
    gq = n_q * tq
    n_grp = min(S // gq, ATTN_STEP_QUERIES // gq) if nk % 2 == 0 else 1
    bq = n_grp * gq
    nch = n_maps * n_q
    kern = functools.partial(_attn_kernel, n_maps=n_maps, n_q=n_q, n_grp=n_grp, tq=tq, tk=tk,
                             sub=min(tk, KEY_SUB), nk=nk, post_scale=post_scale)
    return pl.pallas_call(
        kern,
        grid=(B, H, S // bq),
        in_specs=[
            pl.BlockSpec(memory_space=pltpu.SMEM),
            pl.BlockSpec((None, None, n_maps, dk, bq), lambda b, h, i: (b, h, 0, 0, i)),
            pl.BlockSpec((None, S, dk), lambda b, h, i: (b, 0, h)),
            pl.BlockSpec((None, None, dv + V_AUG, S), lambda b, h, i: (b, h, 0, 0)),
            pl.BlockSpec((dv, 1), lambda b, h, i: (0, 0)),
        ],
        out_specs=pl.BlockSpec((None, bq, dv), lambda b, h, i: (b, i, h)),
        out_shape=jax.ShapeDtypeStruct((B, S, H * dv), BF16),
        scratch_shapes=(
            [pltpu.VMEM((tk, tq), F32)] * (2 * nch)
            + [pltpu.VMEM((1, tq), F32)] * (3 * nch)
            + [pltpu.VMEM((dv + V_AUG, tq), F32)] * nch
        ),
        compiler_params=_cparams("parallel", "parallel", "arbitrary"),
        name="attn_diff" if n_maps == 2 else "attn_mla",
    )(lam, qt, k, vt, g)


def _ret_rope(x_ref, cos_ref, ss_ref, off):
    x = x_ref[off:off + RET_CHUNK, :]
    return x * cos_ref[off:off + RET_CHUNK, :] + pltpu.roll(x, LANES // 2, 1) * ss_ref[off:off + RET_CHUNK, :]


def _lane_head(shape, axis):
    return (lax.broadcasted_iota(jnp.int32, shape, axis) // (C_DK // 2)) % 2


def _head_masks():
    r = _lane_head((2 * C_DK, 2 * C_DV), 0)
    c = lax.broadcasted_iota(jnp.int32, (2 * C_DK, 2 * C_DV), 1) // C_DV
    return r == c


def _pair_views(p, qk_refs, v_refs):
    qk = [r.at[:, p * LANES:(p + 1) * LANES] for r in qk_refs]
    vs = [r.at[:, p * 2 * C_DV:(p + 1) * 2 * C_DV] for r in v_refs]
    return qk, vs


def _ret_bwd_kernel(lg_ref, q_ref, k_ref, v_ref, cos_ref, ss_ref, o_ref, r_sc, *, nc):
    C = RET_CHUNK
    pairs = C_HEADS // 2

    @pl.when(pl.program_id(1) == 0)
    def _():
        r_sc[...] = jnp.zeros(r_sc.shape, F32)

    idx = lax.broadcasted_iota(jnp.int32, (C, 2 * C_DK), 0).astype(F32)
    bd = _head_masks()
    consts = []
    for p in range(pairs):
        lg = lg_ref[p]
        consts.append((jnp.exp(lg * (C - idx)),
                       jnp.exp(lg * idx) * (C_DK ** -0.5),
                       jnp.exp(lg * C).T))

    for t in range(nc):
        off = (nc - 1 - t) * C
        for p in range(pairs):
            (q_p, k_p), (v_p, o_p) = _pair_views(p, (q_ref, k_ref), (v_ref, o_ref))
            q_dec, k_dec, decay = consts[p]
            qc = (_ret_rope(q_p, cos_ref, ss_ref, off) * q_dec).astype(BF16)
            kc = _ret_rope(k_p, cos_ref, ss_ref, off) * k_dec
            vc = v_p[off:off + C, :].astype(BF16)
            r = r_sc[p]
            o_p[off:off + C, :] = jnp.dot(qc, r.astype(BF16), preferred_element_type=F32)
            u = jnp.dot(kc.T.astype(BF16), vc, preferred_element_type=F32)
            r_sc[p] = r * decay + jnp.where(bd, u, 0.0)


def _ret_fwd_kernel(lgf_ref, lgb_ref, q_ref, k_ref, v_ref, gate_ref, xb_ref, cos_ref, ss_ref,
                    nrm_ref, o_ref, r_sc, *, nc):
    C = RET_CHUNK
    pairs = C_HEADS // 2

    @pl.when(pl.program_id(1) == 0)
    def _():
        r_sc[...] = jnp.zeros(r_sc.shape, F32)

    idx = lax.broadcasted_iota(jnp.int32, (C, 2 * C_DK), 0).astype(F32)
    bd = _head_masks()
    lane_head = _lane_head((C, 2 * C_DK), 1)
    ii = lax.broadcasted_iota(jnp.int32, (C, C), 0)
    jj = lax.broadcasted_iota(jnp.int32, (C, C), 1)
    dist = (ii - jj).astype(F32)
    g = nrm_ref[...]
    consts = []
    for p in range(pairs):
        lgf, lgb = lgf_ref[p], lgb_ref[p]
        dmats = []
        for e in range(2):
            lane_e = e * (C_DK // 2)
            gf = lgf[:, lane_e:lane_e + 1]
            gb = lgb[:, lane_e:lane_e + 1]
            dmats.append(jnp.where(dist >= 0, jnp.exp(gf * jnp.maximum(dist, 0.0)),
                                   jnp.exp(gb * jnp.maximum(-dist, 0.0))))
        consts.append((jnp.exp(lgf * (idx + 1.0)), jnp.exp(lgf * (C - 1.0 - idx)), jnp.exp(lgf * C).T, dmats))

    def chunk_inputs(p, t):
        (q_p, k_p), _ = _pair_views(p, (q_ref, k_ref), ())
        off = t * C
        qr = _ret_rope(q_p, cos_ref, ss_ref, off)
        kr = _ret_rope(k_p, cos_ref, ss_ref, off) * (C_DK ** -0.5)
        kb = kr.astype(BF16)
        ss = [lax.dot_general(jnp.where(lane_head == e, qr, 0.0).astype(BF16), kb, (((1,), (1,)), ((), ())),
                              preferred_element_type=F32) for e in range(2)]
        return qr, kr, ss

    nxt = [chunk_inputs(p, 0) for p in range(pairs)]
    for p in range(pairs):
        for t in range(nc):
            off = t * C
            _, (v_p, gate_p, xb_p, o_p) = _pair_views(p, (), (v_ref, gate_ref, xb_ref, o_ref))
            q_dec, k_dec, decay, dmats = consts[p]
            qr, kr, ss = nxt[p]
            vc = v_p[off:off + C, :].astype(BF16)
            r = r_sc[p]
            cross = jnp.dot((qr * q_dec).astype(BF16), r.astype(BF16), preferred_element_type=F32)
            if t + 1 < nc:
                nxt[p] = chunk_inputs(p, t + 1)
            for e in range(2):
                inner = jnp.dot((ss[e] * dmats[e]).astype(BF16), vc[:, e * C_DV:(e + 1) * C_DV],
                                preferred_element_type=F32)
                ret = inner + cross[:, e * C_DV:(e + 1) * C_DV] + xb_p[off:off + C, e * C_DV:(e + 1) * C_DV]
                oc = _rms(ret, g)
                gt = gate_p[off:off + C, e * C_DV:(e + 1) * C_DV]
                gated = (gt * (1.0 / (1.0 + jnp.exp(-gt)))) * oc
                o_p[off:off + C, e * C_DV:(e + 1) * C_DV] = gated.astype(o_ref.dtype)
            u = jnp.dot((kr * k_dec).T.astype(BF16), vc, preferred_element_type=F32)
            r_sc[p] = r * decay + jnp.where(bd, u, 0.0)


def _retention(zc, lgf, lgb, nrm, tabs, B, S, ts):
    ns = S // ts
    nc = ts // RET_CHUNK
    pairs = C_HEADS // 2
    qk_w, v_w = C_HEADS * C_DK, C_HEADS * C_DV
    lg_spec = pl.BlockSpec((pairs, 1, LANES), lambda b, i: (0, 0, 0))
    state = pltpu.VMEM((pairs, 2 * C_DK, 2 * C_DV), F32)

    def specs(rev):
        pos = (lambda i: ns - 1 - i) if rev else (lambda i: i)
        return dict(
            q=pl.BlockSpec((ts, qk_w), lambda b, i: (b * ns + pos(i), 0)),
            k=pl.BlockSpec((ts, qk_w), lambda b, i: (b * ns + pos(i), 1)),
            v=pl.BlockSpec((ts, v_w), lambda b, i: (b * ns + pos(i), 1)),
            gate=pl.BlockSpec((ts, v_w), lambda b, i: (b * ns + pos(i), 2)),
            out=pl.BlockSpec((ts, v_w), lambda b, i: (b * ns + pos(i), 0)),
            tab=pl.BlockSpec((ts, LANES), lambda b, i: (pos(i), 0)),
        )

    sb_ = specs(True)
    xb = pl.pallas_call(
        functools.partial(_ret_bwd_kernel, nc=nc),
        grid=(B, ns),
        in_specs=[lg_spec, sb_["q"], sb_["k"], sb_["v"], sb_["tab"], sb_["tab"]],
        out_specs=sb_["out"],
        out_shape=jax.ShapeDtypeStruct((B * S, v_w), F32),
        scratch_shapes=[state],
        compiler_params=_cparams("parallel", "arbitrary"),
        name="ret_bwd",
    )(lgb, zc, zc, zc, *tabs)

    sf = specs(False)
    return pl.pallas_call(
        functools.partial(_ret_fwd_kernel, nc=nc),
        grid=(B, ns),
        in_specs=[lg_spec, lg_spec, sf["q"], sf["k"], sf["v"], sf["gate"], sf["out"],
                  sf["tab"], sf["tab"], pl.BlockSpec((1, C_DV), lambda b, i: (0, 0))],
        out_specs=sf["out"],
        out_shape=jax.ShapeDtypeStruct((B * S, v_w), BF16),
        scratch_shapes=[state],
        compiler_params=_cparams("parallel", "arbitrary"),
        name="ret_fwd",
    )(lgf, lgb, zc, zc, zc, zc, xb, *tabs, nrm)


def _out_proj_kernel(x_ref, oa_ref, ob_ref, oc_ref, w_ref, o_ref):
    na, nb = oa_ref.shape[1], ob_ref.shape[1]
    acc = jnp.dot(oa_ref[...], w_ref[:na], preferred_element_type=F32)
    acc += jnp.dot(ob_ref[...], w_ref[na:na + nb], preferred_element_type=F32)
    acc += jnp.dot(oc_ref[...], w_ref[na + nb:], preferred_element_type=F32)
    o_ref[...] = x_ref[...] + acc


def _out_proj(x, oa, ob, oc, w, tm):
    T, D = x.shape
    row = lambda a: pl.BlockSpec((tm, a.shape[1]), lambda i: (i, 0))
    return pl.pallas_call(
        _out_proj_kernel,
        grid=(T // tm,),
        in_specs=[row(x), row(oa), row(ob), row(oc),
                  pl.BlockSpec(w.shape, lambda i: (0, 0), pipeline_mode=pl.Buffered(1))],
        out_specs=row(x),
        out_shape=jax.ShapeDtypeStruct((T, D), F32),
        compiler_params=_cparams("parallel"),
        name="out_proj",
    )(x, oa, ob, oc, w)


def _ffn_kernel(x_ref, xp_ref, xn_ref, g_ref, wg_ref, wu_ref, cw_ref, cb_ref, wd_ref, gf_ref, o_ref,
                h_sc, acc_sc, *, seq_len, final_norm):
    tm = x_ref.shape[0]
    halo = SUBLANES
    ext = tm + 2 * halo
    i, j = pl.program_id(0), pl.program_id(1)

    @pl.when(j == 0)
    def _():
        g = g_ref[...]

        def norm_rows(c, carry):
            rows = pl.ds(pl.multiple_of(c * NORM_ROWS, NORM_ROWS), NORM_ROWS)
            h_sc[rows, :] = _rms(x_ref[rows, :], g).astype(BF16)
            return carry

        lax.fori_loop(0, tm // NORM_ROWS, norm_rows, 0, unroll=16)
        h_sc[tm:tm + halo] = _rms(xn_ref[...], g).astype(BF16)
        h_sc[tm + halo:] = _rms(xp_ref[...], g).astype(BF16)
        acc_sc[...] = jnp.zeros(acc_sc.shape, F32)

    gate = jnp.dot(h_sc[...], wg_ref[...], preferred_element_type=F32)
    up = jnp.dot(h_sc[:tm], wu_ref[...], preferred_element_type=F32)
    left = pltpu.roll(gate, 1, 0)[:tm]
    right = pltpu.roll(gate, ext - 1, 0)[:tm]
    pos = lax.rem(i * tm, seq_len) + lax.broadcasted_iota(jnp.int32, (tm, 1), 0)
    left = jnp.where(pos != 0, left, 0.0)
    right = jnp.where(pos != seq_len - 1, right, 0.0)
    cw = cw_ref[...]
    gc = cb_ref[...] + left * cw[0:1] + gate[:tm] * cw[1:2] + right * cw[2:3]
    act = (gc * (1.0 / (1.0 + jnp.exp(-gc)))) * up
    acc_sc[...] += jnp.dot(act.astype(BF16), wd_ref[...], preferred_element_type=F32)

    @pl.when(j == pl.num_programs(1) - 1)
    def _():
        y = x_ref[...] + acc_sc[...]
        if final_norm:
            y = _rms(y, gf_ref[...])
        o_ref[...] = y


def _ffn(x, g, wg, wu, cw, cb, wd, gf, *, seq_len, tm, final_norm):
    T, D = x.shape
    tf = wg.shape[2]
    F = wg.shape[0] * tf
    hb = tm // SUBLANES
    last = T // SUBLANES - 1
    kern = functools.partial(_ffn_kernel, seq_len=seq_len, final_norm=final_norm)
    return pl.pallas_call(
        kern,
        grid=(T // tm, F // tf),
        in_specs=[
            pl.BlockSpec((tm, D), lambda i, j: (i, 0)),
            pl.BlockSpec((SUBLANES, D), lambda i, j: (jnp.maximum(i * hb - 1, 0), 0)),
            pl.BlockSpec((SUBLANES, D), lambda i, j: (jnp.minimum((i + 1) * hb, last), 0)),
            pl.BlockSpec((1, D), lambda i, j: (0, 0)),
            pl.BlockSpec((None, D, tf), lambda i, j: (j, 0, 0)),
            pl.BlockSpec((None, D, tf), lambda i, j: (j, 0, 0)),
            pl.BlockSpec((CONV_WIDTH, tf), lambda i, j: (0, j)),
            pl.BlockSpec((1, tf), lambda i, j: (0, j)),
            pl.BlockSpec((tf, D), lambda i, j: (j, 0)),
            pl.BlockSpec((1, D), lambda i, j: (0, 0)),
        ],
        out_specs=pl.BlockSpec((tm, D), lambda i, j: (i, 0)),
        out_shape=jax.ShapeDtypeStruct((T, D), F32),
        scratch_shapes=[pltpu.VMEM((tm + 2 * SUBLANES, D), BF16), pltpu.VMEM((tm, D), F32)],
        compiler_params=_cparams("parallel", "arbitrary"),
        name="ffn",
    )(x, x, x, g, wg, wu, cw, cb, wd, gf)


def _rope_tables(S, theta, n_rot, group):
    half = n_rot // 2
    pos = jnp.arange(S, dtype=F32)
    inv = jnp.power(jnp.float32(theta), -jnp.arange(half, dtype=F32) * 2.0 / n_rot)
    ang = pos[:, None] * inv[None, :]
    cos, sin = jnp.cos(ang), jnp.sin(ang)
    ones = jnp.ones((S, group - n_rot), F32)
    zeros = jnp.zeros((S, half), F32)
    zrest = jnp.zeros((S, group - n_rot), F32)
    reps = LANES // group
    c = jnp.tile(jnp.concatenate([cos, cos, ones], axis=1), (1, reps))
    sa = jnp.tile(jnp.concatenate([-sin, zeros, zrest], axis=1), (1, reps))
    sb = jnp.tile(jnp.concatenate([zeros, sin, zrest], axis=1), (1, reps))
    return c, sa, sb


def _ret_rope_tables(S):
    half = C_DK // 2
    pos = jnp.arange(S, dtype=F32)
    inv = jnp.power(jnp.float32(RET_THETA), -jnp.arange(half, dtype=F32) * 2.0 / C_DK)
    ang = pos[:, None] * inv[None, :]
    cos, sin = jnp.cos(ang), jnp.sin(ang)
    return jnp.tile(cos, (1, 4)), jnp.concatenate([-sin, -sin, sin, sin], axis=1)


def _pick(n, pref):
    t = min(n, pref)
    while n % t:
        t //= 2
    return t


def _layer_weights(l, p):
    d = p["w_in"].shape[1]
    cut = ZA_COLS + B_Q_RANK + B_KV_RANK + B_ROPE
    nqk = 2 * C_HEADS * C_DK
    w_cqk = p["w_in"][l][:, cut:cut + nqk].reshape(d, 2, C_HEADS // 2, 2, 2, C_DK // 2)
    w_cqk = w_cqk.transpose(0, 1, 2, 4, 3, 5).reshape(d, nqk)
    w_in = jnp.concatenate([p["w_in"][l][:, :cut], jnp.zeros((d, B_ROPE), F32), w_cqk,
                            p["w_in"][l][:, cut + nqk:]], axis=1)
    wuq = p["w_uq"][l].reshape(B_Q_RANK, B_HEADS, B_NOPE + B_ROPE)
    wuq = jnp.concatenate([wuq[:, :, :B_NOPE].reshape(B_Q_RANK, -1), wuq[:, :, B_NOPE:].reshape(B_Q_RANK, -1)], axis=1)
    wukv = p["w_ukv"][l].reshape(B_KV_RANK, B_HEADS, B_NOPE + B_DV)
    wukv = jnp.concatenate([wukv[:, :, :B_NOPE].reshape(B_KV_RANK, -1), wukv[:, :, B_NOPE:].reshape(B_KV_RANK, -1)], axis=1)
    lp = p["diff_lambda"][l].astype(F32)
    lam_init = 0.8 - 0.6 * math.exp(-0.3 * l)
    lam = jnp.exp(jnp.sum(lp[0] * lp[1])) - jnp.exp(jnp.sum(lp[2] * lp[3])) + lam_init

    def col_tiles(w):
        tf = _pick(w.shape[1], FFN_COLS)
        return w.reshape(w.shape[0], w.shape[1] // tf, tf).transpose(1, 0, 2).astype(BF16)

    def lane_lg(dec):
        lg = jax.nn.log_sigmoid(dec.astype(F32)).reshape(C_HEADS // 2, 2)
        return jnp.tile(jnp.repeat(lg, C_DK // 2, axis=1), (1, 2)).reshape(C_HEADS // 2, 1, LANES)

    return dict(
        norm_mix=p["norm_mix"][l][None], w_in=w_in.astype(BF16),
        lam=lam.reshape(1), lam_init=lam_init, diff_norm=p["diff_norm"][l][:, None],
        mla_q_norm=p["mla_q_norm"][l][None], mla_kv_norm=p["mla_kv_norm"][l][None],
        w_uq=wuq.astype(BF16), w_ukv=wukv.astype(BF16),
        lgf=lane_lg(p["ret_decay_fwd"][l]), lgb=lane_lg(p["ret_decay_bwd"][l]), ret_norm=p["ret_norm"][l][None],
        w_o=p["w_o"][l].astype(BF16), norm_ffn=p["norm_ffn"][l][None],
        w_gate=col_tiles(p["w_gate"][l]), w_up=col_tiles(p["w_up"][l]),
        conv_w=p["conv_w"][l], conv_b=p["conv_b"][l][None], w_down=p["w_down"][l].astype(BF16),
    )


def _trunk(x3, layers, norm_final):
    B, S, D = x3.shape
    T = B * S
    x = x3.reshape(T, D)
    tabs_a = _rope_tables(S, ROPE_THETA, A_ROT, A_DK)
    tabs_b = _rope_tables(S, ROPE_THETA, B_ROPE, B_ROPE)
    tabs_c = _ret_rope_tables(S)
    tm_in = _pick(T, 256)
    tm_prep = _pick(S, 512)
    tq, tk = _pick(S, 512), _pick(S, 1024)
    nq_b = 2 if S % (2 * tq) == 0 else 1
    ts = _pick(S, 1024)
    tm_out = _pick(T, 512)
    tm_ffn = _pick(S, 512)
    no_lam = jnp.zeros((1,), F32)
    no_norm = jnp.ones((B_DV, 1), F32)
    for l, w in enumerate(layers):
        za, zb, zc = _in_proj(x, w["norm_mix"], w["w_in"], tm_in)
        qt, k, vt = _prep_a(za, tabs_a, B, S, tm_prep)
        oa = _attention(w["lam"], qt, k, vt, w["diff_norm"], n_maps=2, n_q=1, dk=2 * A_DK, dv=A_DV, tq=tq, tk=tk,
                        post_scale=1.0 - w["lam_init"]).reshape(T, -1)
        qt, k, vt = _prep_b(zb, w["mla_q_norm"], w["mla_kv_norm"], w["w_uq"], w["w_ukv"], tabs_b, B, S, tm_prep)
        ob = _attention(no_lam, qt.reshape(B, B_HEADS, 1, B_QK, S), k, vt, no_norm, n_maps=1, n_q=nq_b, dk=B_QK,
                        dv=B_DV, tq=tq, tk=tk, post_scale=1.0).reshape(T, -1)
        oc = _retention(zc, w["lgf"], w["lgb"], w["ret_norm"], tabs_c, B, S, ts)
        x = _out_proj(x, oa, ob, oc, w["w_o"], tm_out)
        x = _ffn(x, w["norm_ffn"], w["w_gate"], w["w_up"], w["conv_w"], w["conv_b"], w["w_down"], norm_final,
                 seq_len=S, tm=tm_ffn, final_norm=(l == len(layers) - 1))
    return x.reshape(B, S, D)


def kernel(x_prompt, x_sample, norm_mix, w_in, diff_lambda, diff_norm, mla_q_norm, mla_kv_norm, w_uq, w_ukv,
           ret_decay_fwd, ret_decay_bwd, ret_norm, w_o, norm_ffn, w_gate, w_up, conv_w, conv_b, w_down, norm_final):
    p = dict(norm_mix=norm_mix, w_in=w_in, diff_lambda=diff_lambda, diff_norm=diff_norm, mla_q_norm=mla_q_norm,
             mla_kv_norm=mla_kv_norm, w_uq=w_uq, w_ukv=w_ukv, ret_decay_fwd=ret_decay_fwd,
             ret_decay_bwd=ret_decay_bwd, ret_norm=ret_norm, w_o=w_o, norm_ffn=norm_ffn, w_gate=w_gate, w_up=w_up,
             conv_w=conv_w, conv_b=conv_b, w_down=w_down)
    layers = [_layer_weights(l, p) for l in range(norm_mix.shape[0])]
    gf = norm_final[None]
    return (_trunk(x_prompt, layers, gf), _trunk(x_sample, layers, gf))
```

```python
import functools
import math

import jax
import jax.numpy as jnp
from jax import lax
from jax.experimental import pallas as pl
from jax.experimental.pallas import tpu as pltpu

F32 = jnp.float32
BF16 = jnp.bfloat16

NORM_EPS = 1e-5
ROPE_THETA = 500000.0
RET_THETA = 10000.0
RET_CHUNK = 128
A_HEADS, A_DK, A_DV = 4, 64, 128
A_ROT = A_DK // 4
B_HEADS, B_Q_RANK, B_KV_RANK, B_NOPE, B_ROPE, B_DV = 6, 512, 256, 128, 64, 128
C_HEADS, C_DK, C_DV = 6, 64, 128
CONV_WIDTH = 3

LANES = 128
SUBLANES = 8
ZA_COLS = 3 * A_HEADS * 2 * A_DK
ZB_COLS = B_Q_RANK + B_KV_RANK + 2 * B_ROPE
ZC_COLS = 2 * C_HEADS * C_DK + 2 * C_HEADS * C_DV
B_QK = 2 * LANES
V_AUG = 2 * SUBLANES
KEY_SUB = 256
ATTN_STEP_QUERIES = 2048
NORM_ROWS = 2 * SUBLANES
VMEM_LIMIT = 56 * 1024 * 1024
LOG2E = math.log2(math.e)


def _cparams(*sem):
    return pltpu.CompilerParams(dimension_semantics=sem, vmem_limit_bytes=VMEM_LIMIT)


def _rms(x, g):
    return x * lax.rsqrt(jnp.mean(x * x, axis=-1, keepdims=True) + NORM_EPS) * g


def _ones_rows(n):
    return jnp.where(lax.broadcasted_iota(jnp.int32, (V_AUG, n), 0) == 0, 1.0, 0.0).astype(BF16)


def _rope_pairs(x, cos, ss):
    return x * cos + pltpu.roll(x, LANES // 2, 1) * ss


def _lane_owner(shape, axis):
    return (lax.broadcasted_iota(jnp.int32, shape, axis) // (LANES // 4)) % 2


def _in_proj_kernel(x_ref, g_ref, w_ref, za_ref, zb_ref, zc_ref):
    h = _rms(x_ref[...], g_ref[...]).astype(BF16)
    za_ref[...] = jnp.dot(h, w_ref[:, :ZA_COLS], preferred_element_type=F32)
    zb_ref[...] = jnp.dot(h, w_ref[:, ZA_COLS:ZA_COLS + ZB_COLS], preferred_element_type=F32)
    zc_ref[...] = jnp.dot(h, w_ref[:, ZA_COLS + ZB_COLS:], preferred_element_type=F32)


def _in_proj(x, g, w, tm):
    T, D = x.shape
    n = w.shape[1]
    return pl.pallas_call(
        _in_proj_kernel,
        grid=(T // tm,),
        in_specs=[
            pl.BlockSpec((tm, D), lambda i: (i, 0)),
            pl.BlockSpec((1, D), lambda i: (0, 0)),
            pl.BlockSpec((D, n), lambda i: (0, 0), pipeline_mode=pl.Buffered(1)),
        ],
        out_specs=[
            pl.BlockSpec((tm, ZA_COLS), lambda i: (i, 0)),
            pl.BlockSpec((tm, ZB_COLS), lambda i: (i, 0)),
            pl.BlockSpec((tm, ZC_COLS), lambda i: (i, 0)),
        ],
        out_shape=[
            jax.ShapeDtypeStruct((T, ZA_COLS), F32),
            jax.ShapeDtypeStruct((T, ZB_COLS), F32),
            jax.ShapeDtypeStruct((T, ZC_COLS), F32),
        ],
        compiler_params=_cparams("parallel"),
        name="in_proj",
    )(x, g, w)


def _prep_a_kernel(z_ref, cos_ref, ss_ref, qt_ref, k_ref, vt_ref):
    cos, ss = cos_ref[...], ss_ref[...]
    tm = z_ref.shape[0]
    row_map = _lane_owner((LANES, tm), 0)
    hw = 2 * A_DK
    for h in range(A_HEADS):
        q = _rope_pairs(z_ref[:, h * hw:(h + 1) * hw], cos, ss) * (A_DK ** -0.5 * LOG2E)
        qt = q.T
        qt_ref[h, 0] = jnp.where(row_map == 0, qt, 0.0).astype(BF16)
        qt_ref[h, 1] = jnp.where(row_map == 1, qt, 0.0).astype(BF16)
        k = _rope_pairs(z_ref[:, (A_HEADS + h) * hw:(A_HEADS + h + 1) * hw], cos, ss)
        k_ref[:, h * hw:(h + 1) * hw] = k.astype(BF16)
        v = z_ref[:, (2 * A_HEADS + h) * hw:(2 * A_HEADS + h + 1) * hw]
        vt_ref[h, :A_DV] = v.T.astype(BF16)
        vt_ref[h, A_DV:] = _ones_rows(tm)


def _prep_a(za, tabs, B, S, tm):
    nb = S // tm
    row_map = lambda b, i: (b * nb + i, 0)
    tab_spec = pl.BlockSpec((tm, LANES), lambda b, i: (i, 0))
    return pl.pallas_call(
        _prep_a_kernel,
        grid=(B, nb),
        in_specs=[pl.BlockSpec((tm, ZA_COLS), row_map), tab_spec, tab_spec],
        out_specs=[
            pl.BlockSpec((None, A_HEADS, 2, LANES, tm), lambda b, i: (b, 0, 0, 0, i)),
            pl.BlockSpec((None, tm, A_HEADS * LANES), lambda b, i: (b, i, 0)),
            pl.BlockSpec((None, A_HEADS, A_DV + V_AUG, tm), lambda b, i: (b, 0, 0, i)),
        ],
        out_shape=[
            jax.ShapeDtypeStruct((B, A_HEADS, 2, LANES, S), BF16),
            jax.ShapeDtypeStruct((B, S, A_HEADS * LANES), BF16),
            jax.ShapeDtypeStruct((B, A_HEADS, A_DV + V_AUG, S), BF16),
        ],
        compiler_params=_cparams("parallel", "parallel"),
        name="prep_a",
    )(za, *tabs)


def _prep_b_kernel(z_ref, qn_ref, kvn_ref, wuq_ref, wukv_ref, cos_ref, ss_ref,
                   qt_ref, k_ref, vt_ref):
    cos, ss = cos_ref[...], ss_ref[...]
    tm = z_ref.shape[0]
    scale = (B_NOPE + B_ROPE) ** -0.5 * LOG2E
    owner = _lane_owner((tm, LANES), 1)
    nope_w = B_HEADS * B_NOPE

    cq = _rms(z_ref[:, :B_Q_RANK], qn_ref[...]).astype(BF16)
    q = jnp.dot(cq, wuq_ref[...], preferred_element_type=F32) * scale
    ckv = _rms(z_ref[:, B_Q_RANK:B_Q_RANK + B_KV_RANK], kvn_ref[...]).astype(BF16)
    kv = jnp.dot(ckv, wukv_ref[...], preferred_element_type=F32)
    kr_even = _rope_pairs(z_ref[:, B_Q_RANK + B_KV_RANK:ZB_COLS], cos, ss)
    kr = (kr_even.astype(BF16), pltpu.roll(kr_even, LANES // 4, 1).astype(BF16))

    for c in range(B_HEADS // 2):
        qr = _rope_pairs(q[:, nope_w + c * LANES:nope_w + (c + 1) * LANES], cos, ss)
        for e in range(2):
            h = 2 * c + e
            qt_ref[h, :LANES] = q[:, h * B_NOPE:(h + 1) * B_NOPE].T.astype(BF16)
            qt_ref[h, LANES:] = jnp.where(owner == e, qr, 0.0).T.astype(BF16)
    for h in range(B_HEADS):
        k_ref[:, h * B_QK:h * B_QK + LANES] = kv[:, h * B_NOPE:(h + 1) * B_NOPE].astype(BF16)
        k_ref[:, h * B_QK + LANES:(h + 1) * B_QK] = kr[h % 2]
        vt_ref[h, :B_DV] = kv[:, nope_w + h * B_DV:nope_w + (h + 1) * B_DV].T.astype(BF16)
        vt_ref[h, B_DV:] = _ones_rows(tm)


def _prep_b(zb, qn, kvn, wuq, wukv, tabs, B, S, tm):
    nb = S // tm
    tab_spec = pl.BlockSpec((tm, LANES), lambda b, i: (i, 0))
    full = lambda a: pl.BlockSpec(a.shape, lambda b, i: (0,) * a.ndim)
    return pl.pallas_call(
        _prep_b_kernel,
        grid=(B, nb),
        in_specs=[pl.BlockSpec((tm, ZB_COLS), lambda b, i: (b * nb + i, 0)),
                  full(qn), full(kvn), full(wuq), full(wukv), tab_spec, tab_spec],
        out_specs=[
            pl.BlockSpec((None, B_HEADS, B_QK, tm), lambda b, i: (b, 0, 0, i)),
            pl.BlockSpec((None, tm, B_HEADS * B_QK), lambda b, i: (b, i, 0)),
            pl.BlockSpec((None, B_HEADS, B_DV + V_AUG, tm), lambda b, i: (b, 0, 0, i)),
        ],
        out_shape=[
            jax.ShapeDtypeStruct((B, B_HEADS, B_QK, S), BF16),
            jax.ShapeDtypeStruct((B, S, B_HEADS * B_QK), BF16),
            jax.ShapeDtypeStruct((B, B_HEADS, B_DV + V_AUG, S), BF16),
        ],
        compiler_params=_cparams("parallel", "parallel"),
        name="prep_b",
    )(zb, qn, kvn, wuq, wukv, *tabs)


def _attn_kernel(lam_ref, qt_ref, k_ref, vt_ref, g_ref, o_ref, *scratch,
                 n_maps, n_q, n_grp, tq, tk, sub, nk, post_scale):
    chains = [(mi, qi) for mi in range(n_maps) for qi in range(n_q)]
    nch = len(chains)
    dv = o_ref.shape[1]
    nsub = tk // sub
    gq = n_q * tq
    s_sc = (scratch[0:nch], scratch[nch:2 * nch])
    mx_sc = (scratch[2 * nch:3 * nch], scratch[3 * nch:4 * nch])
    m_sc = scratch[4 * nch:5 * nch]
    acc_sc = scratch[5 * nch:6 * nch]

    def reset():
        for c in range(nch):
            m_sc[c][...] = jnp.full(m_sc[c].shape, -jnp.inf, F32)
            acc_sc[c][...] = jnp.zeros(acc_sc[c].shape, F32)

    def q_start(g, qi):
        return pl.multiple_of(g * gq + qi * tq, tq)

    def scores_sub(g, j, slot, r):
        kt = k_ref[pl.ds(pl.multiple_of(j * tk + r * sub, sub), sub), :]
        for c, (mi, qi) in enumerate(chains):
            s = jnp.dot(kt, qt_ref[mi, :, pl.ds(q_start(g, qi), tq)], preferred_element_type=F32)
            s_sc[slot][c][r * sub:(r + 1) * sub, :] = s
            mx = jnp.max(s, axis=0, keepdims=True)
            mx_sc[slot][c][...] = mx if r == 0 else jnp.maximum(mx_sc[slot][c][...], mx)

    def consume_sub(g, j, slot, r, stats):
        vt = vt_ref[:, pl.ds(pl.multiple_of(j * tk + r * sub, sub), sub)]
        for c in range(nch):
            m_new, alpha = stats[c]
            p = jnp.exp2(s_sc[slot][c][r * sub:(r + 1) * sub, :] - m_new).astype(BF16)
            pv = jnp.dot(vt, p, preferred_element_type=F32)
            acc = acc_sc[c]
            acc[...] = (alpha * acc[...] if r == 0 else acc[...]) + pv

    def tile(g, j, slot, nxt):
        stats = []
        for c in range(nch):
            m_old = m_sc[c][...]
            m_new = jnp.maximum(m_old, mx_sc[slot][c][...])
            stats.append((m_new, jnp.exp2(m_old - m_new)))
            m_sc[c][...] = m_new
        for r in range(nsub):
            if nxt is not None:
                scores_sub(nxt[0], nxt[1], 1 - slot, r)
            consume_sub(g, j, slot, r, stats)

    def finalize(g):
        for qi in range(n_q):
            a0 = acc_sc[qi]
            o = a0[:dv] / a0[dv:dv + 1]
            if n_maps == 2:
                a1 = acc_sc[n_q + qi]
                o = o - lam_ref[0] * (a1[:dv] / a1[dv:dv + 1])
                o = o * lax.rsqrt(jnp.mean(o * o, axis=0, keepdims=True) + NORM_EPS) * g_ref[...] * post_scale
            o_ref[pl.ds(q_start(g, qi), tq), :] = o.T.astype(o_ref.dtype)

    def group(g, last):
        def looped_pair(t, carry):
            tile(g, 2 * t, 0, (g, 2 * t + 1))
            tile(g, 2 * t + 1, 1, (g, 2 * t + 2))
            return carry

        lax.fori_loop(0, (nk - 1) // 2, looped_pair, 0)
        if nk % 2 == 0:
            tile(g, nk - 2, 0, (g, nk - 1))
        tile(g, nk - 1, (nk - 1) % 2, None if last else (g + 1, 0))
        finalize(g)
        if not last:
            reset()

    def looped_group(g, carry):
        group(g, False)
        return carry

    reset()
    for r in range(nsub):
        scores_sub(0, 0, 0, r)
    lax.fori_loop(0, n_grp - 1, looped_group, 0)
    group(n_grp - 1, True)


def _attention(lam, qt, k, vt, g, *, n_maps, n_q, dk, dv, tq, tk, post_scale):
    B, H = qt.shape[0], qt.shape[1]
    S = k.shape[1]
    nk = S // tk
    gq = n_q * tq
    n_grp = min(S // gq, ATTN_STEP_QUERIES // gq) if nk % 2 == 0 else 1
    bq = n_grp * gq
    nch = n_maps * n_q
    kern = functools.partial(_attn_kernel, n_maps=n_maps, n_q=n_q, n_grp=n_grp, tq=tq, tk=tk,
                             sub=min(tk, KEY_SUB), nk=nk, post_scale=post_scale)
    return pl.pallas_call(
        kern,
        grid=(B, H, S // bq),
        in_specs=[
            pl.BlockSpec(memory_space=pltpu.SMEM),
            pl.BlockSpec((None, None, n_maps, dk, bq), lambda b, h, i: (b, h, 0, 0, i)),
            pl.BlockSpec((None, S, dk), lambda b, h, i: (b, 0, h)),
            pl.BlockSpec((None, None, dv + V_AUG, S), lambda b, h, i: (b, h, 0, 0)),
            pl.BlockSpec((dv, 1), lambda b, h, i: (0, 0)),
        ],
        out_specs=pl.BlockSpec((None, bq, dv), lambda b, h, i: (b, i, h)),
        out_shape=jax.ShapeDtypeStruct((B, S, H * dv), BF16),
        scratch_shapes=(
            [pltpu.VMEM((tk, tq), F32)] * (2 * nch)
            + [pltpu.VMEM((1, tq), F32)] * (3 * nch)
            + [pltpu.VMEM((dv + V_AUG, tq), F32)] * nch
        ),
        compiler_params=_cparams("parallel", "parallel", "arbitrary"),
        name="attn_diff" if n_maps == 2 else "attn_mla",
    )(lam, qt, k, vt, g)


def _ret_rope(x_ref, cos_ref, ss_ref, off):
    rows = slice(off, off + RET_CHUNK)
    return _rope_pairs(x_ref[rows, :], cos_ref[rows, :], ss_ref[rows, :])


def _head_masks():
    r = _lane_owner((2 * C_DK, 2 * C_DV), 0)
    c = lax.broadcasted_iota(jnp.int32, (2 * C_DK, 2 * C_DV), 1) // C_DV
    return r == c


def _pair_views(p, qk_refs, v_refs):
    qk = [r.at[:, p * LANES:(p + 1) * LANES] for r in qk_refs]
    vs = [r.at[:, p * 2 * C_DV:(p + 1) * 2 * C_DV] for r in v_refs]
    return qk, vs


def _ret_bwd_kernel(lg_ref, q_ref, k_ref, v_ref, cos_ref, ss_ref, o_ref, r_sc, *, nc):
    C = RET_CHUNK
    pairs = C_HEADS // 2

    @pl.when(pl.program_id(1) == 0)
    def _():
        r_sc[...] = jnp.zeros(r_sc.shape, F32)

    idx = lax.broadcasted_iota(jnp.int32, (C, 2 * C_DK), 0).astype(F32)
    bd = _head_masks()
    consts = []
    for p in range(pairs):
        lg = lg_ref[p]
        consts.append((jnp.exp(lg * (C - idx)),
                       jnp.exp(lg * idx) * (C_DK ** -0.5),
                       jnp.exp(lg * C).T))

    for t in range(nc):
        off = (nc - 1 - t) * C
        for p in range(pairs):
            (q_p, k_p), (v_p, o_p) = _pair_views(p, (q_ref, k_ref), (v_ref, o_ref))
            q_dec, k_dec, decay = consts[p]
            qc = (_ret_rope(q_p, cos_ref, ss_ref, off) * q_dec).astype(BF16)
            kc = _ret_rope(k_p, cos_ref, ss_ref, off) * k_dec
            vc = v_p[off:off + C, :].astype(BF16)
            r = r_sc[p]
            o_p[off:off + C, :] = jnp.dot(qc, r.astype(BF16), preferred_element_type=F32)
            u = jnp.dot(kc.T.astype(BF16), vc, preferred_element_type=F32)
            r_sc[p] = r * decay + jnp.where(bd, u, 0.0)


def _ret_fwd_kernel(lgf_ref, lgb_ref, q_ref, k_ref, v_ref, gate_ref, xb_ref, cos_ref, ss_ref,
                    nrm_ref, o_ref, r_sc, *, nc):
    C = RET_CHUNK
    pairs = C_HEADS // 2

    @pl.when(pl.program_id(1) == 0)
    def _():
        r_sc[...] = jnp.zeros(r_sc.shape, F32)

    idx = lax.broadcasted_iota(jnp.int32, (C, 2 * C_DK), 0).astype(F32)
    bd = _head_masks()
    lane_head = _lane_owner((C, 2 * C_DK), 1)
    ii = lax.broadcasted_iota(jnp.int32, (C, C), 0)
    jj = lax.broadcasted_iota(jnp.int32, (C, C), 1)
    dist = (ii - jj).astype(F32)
    g = nrm_ref[...]
    consts = []
    for p in range(pairs):
        lgf, lgb = lgf_ref[p], lgb_ref[p]
        dmats = []
        for e in range(2):
            lane_e = e * (C_DK // 2)
            gf = lgf[:, lane_e:lane_e + 1]
            gb = lgb[:, lane_e:lane_e + 1]
            dmats.append(jnp.where(dist >= 0, jnp.exp(gf * jnp.maximum(dist, 0.0)),
                                   jnp.exp(gb * jnp.maximum(-dist, 0.0))))
        consts.append((jnp.exp(lgf * (idx + 1.0)), jnp.exp(lgf * (C - 1.0 - idx)), jnp.exp(lgf * C).T, dmats))

    def chunk_inputs(p, t):
        (q_p, k_p), _ = _pair_views(p, (q_ref, k_ref), ())
        off = t * C
        qr = _ret_rope(q_p, cos_ref, ss_ref, off)
        kr = _ret_rope(k_p, cos_ref, ss_ref, off) * (C_DK ** -0.5)
        kb = kr.astype(BF16)
        ss = [lax.dot_general(jnp.where(lane_head == e, qr, 0.0).astype(BF16), kb, (((1,), (1,)), ((), ())),
                              preferred_element_type=F32) for e in range(2)]
        return qr, kr, ss

    nxt = [chunk_inputs(p, 0) for p in range(pairs)]
    for p in range(pairs):
        for t in range(nc):
            off = t * C
            _, (v_p, gate_p, xb_p, o_p) = _pair_views(p, (), (v_ref, gate_ref, xb_ref, o_ref))
            q_dec, k_dec, decay, dmats = consts[p]
            qr, kr, ss = nxt[p]
            vc = v_p[off:off + C, :].astype(BF16)
            r = r_sc[p]
            cross = jnp.dot((qr * q_dec).astype(BF16), r.astype(BF16), preferred_element_type=F32)
            if t + 1 < nc:
                nxt[p] = chunk_inputs(p, t + 1)
            for e in range(2):
                inner = jnp.dot((ss[e] * dmats[e]).astype(BF16), vc[:, e * C_DV:(e + 1) * C_DV],
                                preferred_element_type=F32)
                ret = inner + cross[:, e * C_DV:(e + 1) * C_DV] + xb_p[off:off + C, e * C_DV:(e + 1) * C_DV]
                oc = _rms(ret, g)
                gt = gate_p[off:off + C, e * C_DV:(e + 1) * C_DV]
                gated = (gt * (1.0 / (1.0 + jnp.exp(-gt)))) * oc
                o_p[off:off + C, e * C_DV:(e + 1) * C_DV] = gated.astype(o_ref.dtype)
            u = jnp.dot((kr * k_dec).T.astype(BF16), vc, preferred_element_type=F32)
            r_sc[p] = r * decay + jnp.where(bd, u, 0.0)


def _retention(zc, lgf, lgb, nrm, tabs, B, S, ts):
    ns = S // ts
    nc = ts // RET_CHUNK
    pairs = C_HEADS // 2
    qk_w, v_w = C_HEADS * C_DK, C_HEADS * C_DV
    lg_spec = pl.BlockSpec((pairs, 1, LANES), lambda b, i: (0, 0, 0))
    state = pltpu.VMEM((pairs, 2 * C_DK, 2 * C_DV), F32)

    def specs(rev):
        pos = (lambda i: ns - 1 - i) if rev else (lambda i: i)
        return dict(
            q=pl.BlockSpec((ts, qk_w), lambda b, i: (b * ns + pos(i), 0)),
            k=pl.BlockSpec((ts, qk_w), lambda b, i: (b * ns + pos(i), 1)),
            v=pl.BlockSpec((ts, v_w), lambda b, i: (b * ns + pos(i), 1)),
            gate=pl.BlockSpec((ts, v_w), lambda b, i: (b * ns + pos(i), 2)),
            out=pl.BlockSpec((ts, v_w), lambda b, i: (b * ns + pos(i), 0)),
            tab=pl.BlockSpec((ts, LANES), lambda b, i: (pos(i), 0)),
        )

    sb_ = specs(True)
    xb = pl.pallas_call(
        functools.partial(_ret_bwd_kernel, nc=nc),
        grid=(B, ns),
        in_specs=[lg_spec, sb_["q"], sb_["k"], sb_["v"], sb_["tab"], sb_["tab"]],
        out_specs=sb_["out"],
        out_shape=jax.ShapeDtypeStruct((B * S, v_w), F32),
        scratch_shapes=[state],
        compiler_params=_cparams("parallel", "arbitrary"),
        name="ret_bwd",
    )(lgb, zc, zc, zc, *tabs)

    sf = specs(False)
    return pl.pallas_call(
        functools.partial(_ret_fwd_kernel, nc=nc),
        grid=(B, ns),
        in_specs=[lg_spec, lg_spec, sf["q"], sf["k"], sf["v"], sf["gate"], sf["out"],
                  sf["tab"], sf["tab"], pl.BlockSpec((1, C_DV), lambda b, i: (0, 0))],
        out_specs=sf["out"],
        out_shape=jax.ShapeDtypeStruct((B * S, v_w), BF16),
        scratch_shapes=[state],
        compiler_params=_cparams("parallel", "arbitrary"),
        name="ret_fwd",
    )(lgf, lgb, zc, zc, zc, zc, xb, *tabs, nrm)


def _out_proj_kernel(x_ref, oa_ref, ob_ref, oc_ref, w_ref, o_ref):
    na, nb = oa_ref.shape[1], ob_ref.shape[1]
    acc = jnp.dot(oa_ref[...], w_ref[:na], preferred_element_type=F32)
    acc += jnp.dot(ob_ref[...], w_ref[na:na + nb], preferred_element_type=F32)
    acc += jnp.dot(oc_ref[...], w_ref[na + nb:], preferred_element_type=F32)
    o_ref[...] = x_ref[...] + acc


def _out_proj(x, oa, ob, oc, w, tm):
    T, D = x.shape
    row = lambda a: pl.BlockSpec((tm, a.shape[1]), lambda i: (i, 0))
    return pl.pallas_call(
        _out_proj_kernel,
        grid=(T // tm,),
        in_specs=[row(x), row(oa), row(ob), row(oc),
                  pl.BlockSpec(w.shape, lambda i: (0, 0), pipeline_mode=pl.Buffered(1))],
        out_specs=row(x),
        out_shape=jax.ShapeDtypeStruct((T, D), F32),
        compiler_params=_cparams("parallel"),
        name="out_proj",
    )(x, oa, ob, oc, w)


def _ffn_kernel(x_ref, xp_ref, xn_ref, g_ref, wg_ref, wu_ref, cw_ref, cb_ref, wd_ref, gf_ref, o_ref,
                h_sc, acc_sc, *, seq_len, final_norm):
    tm = x_ref.shape[0]
    halo = SUBLANES
    ext = tm + 2 * halo
    i, j = pl.program_id(0), pl.program_id(1)

    @pl.when(j == 0)
    def _():
        g = g_ref[...]

        def norm_rows(c, carry):
            rows = pl.ds(pl.multiple_of(c * NORM_ROWS, NORM_ROWS), NORM_ROWS)
            h_sc[rows, :] = _rms(x_ref[rows, :], g).astype(BF16)
            return carry

        lax.fori_loop(0, tm // NORM_ROWS, norm_rows, 0, unroll=16)
        h_sc[tm:tm + halo] = _rms(xn_ref[...], g).astype(BF16)
        h_sc[tm + halo:] = _rms(xp_ref[...], g).astype(BF16)
        acc_sc[...] = jnp.zeros(acc_sc.shape, F32)

    gate = jnp.dot(h_sc[...], wg_ref[...], preferred_element_type=F32)
    up = jnp.dot(h_sc[:tm], wu_ref[...], preferred_element_type=F32)
    left = pltpu.roll(gate, 1, 0)[:tm]
    right = pltpu.roll(gate, ext - 1, 0)[:tm]
    pos = lax.rem(i * tm, seq_len) + lax.broadcasted_iota(jnp.int32, (tm, 1), 0)
    left = jnp.where(pos != 0, left, 0.0)
    right = jnp.where(pos != seq_len - 1, right, 0.0)
    cw = cw_ref[...]
    gc = cb_ref[...] + left * cw[0:1] + gate[:tm] * cw[1:2] + right * cw[2:3]
    act = (gc * (1.0 / (1.0 + jnp.exp(-gc)))) * up
    acc_sc[...] += jnp.dot(act.astype(BF16), wd_ref[...], preferred_element_type=F32)

    @pl.when(j == pl.num_programs(1) - 1)
    def _():
        y = x_ref[...] + acc_sc[...]
        if final_norm:
            y = _rms(y, gf_ref[...])
        o_ref[...] = y


def _ffn(x, g, wg, wu, cw, cb, wd, gf, *, seq_len, tm, tf, final_norm):
    T, D = x.shape
    F = wg.shape[1]
    hb = tm // SUBLANES
    last = T // SUBLANES - 1
    kern = functools.partial(_ffn_kernel, seq_len=seq_len, final_norm=final_norm)
    return pl.pallas_call(
        kern,
        grid=(T // tm, F // tf),
        in_specs=[
            pl.BlockSpec((tm, D), lambda i, j: (i, 0)),
            pl.BlockSpec((SUBLANES, D), lambda i, j: (jnp.maximum(i * hb - 1, 0), 0)),
            pl.BlockSpec((SUBLANES, D), lambda i, j: (jnp.minimum((i + 1) * hb, last), 0)),
            pl.BlockSpec((1, D), lambda i, j: (0, 0)),
            pl.BlockSpec((D, tf), lambda i, j: (0, j)),
            pl.BlockSpec((D, tf), lambda i, j: (0, j)),
            pl.BlockSpec((CONV_WIDTH, tf), lambda i, j: (0, j)),
            pl.BlockSpec((1, tf), lambda i, j: (0, j)),
            pl.BlockSpec((tf, D), lambda i, j: (j, 0)),
            pl.BlockSpec((1, D), lambda i, j: (0, 0)),
        ],
        out_specs=pl.BlockSpec((tm, D), lambda i, j: (i, 0)),
        out_shape=jax.ShapeDtypeStruct((T, D), F32),
        scratch_shapes=[pltpu.VMEM((tm + 2 * SUBLANES, D), BF16), pltpu.VMEM((tm, D), F32)],
        compiler_params=_cparams("parallel", "arbitrary"),
        name="ffn",
    )(x, x, x, g, wg, wu, cw, cb, wd, gf)


def _rope_tables(S, theta, n_rot):
    half = n_rot // 2
    quarter = LANES // 4
    pos = jnp.arange(S, dtype=F32)
    inv = jnp.power(jnp.float32(theta), -jnp.arange(half, dtype=F32) * 2.0 / n_rot)
    ang = pos[:, None] * inv[None, :]
    cos = jnp.concatenate([jnp.cos(ang), jnp.ones((S, quarter - half), F32)], axis=1)
    sin = jnp.concatenate([jnp.sin(ang), jnp.zeros((S, quarter - half), F32)], axis=1)
    return jnp.tile(cos, (1, 4)), jnp.concatenate([-sin, -sin, sin, sin], axis=1)


def _interleave_pairs(w, n_rot):
    d = w.shape[0]
    half = n_rot // 2
    quarter = LANES // 4
    w = w.reshape(d, -1, 2, LANES // 2)
    rest = w[..., n_rot:]
    split = quarter - half
    first = jnp.concatenate([w[..., :half], rest[..., :split]], axis=-1)
    second = jnp.concatenate([w[..., half:n_rot], rest[..., split:]], axis=-1)
    return jnp.stack([first, second], axis=2).reshape(d, -1)


def _pick(n, pref):
    t = min(n, pref)
    while n % t:
        t //= 2
    return t


def _layer_weights(l, p):
    w = p["w_in"][l]
    d = w.shape[0]
    a_qk = 2 * A_HEADS * 2 * A_DK
    kr0 = ZA_COLS + B_Q_RANK + B_KV_RANK
    c0 = kr0 + B_ROPE
    c_qk = 2 * C_HEADS * C_DK
    w_in = jnp.concatenate([
        _interleave_pairs(w[:, :a_qk], A_ROT), w[:, a_qk:kr0],
        _interleave_pairs(jnp.concatenate([w[:, kr0:c0], jnp.zeros((d, B_ROPE), F32)], axis=1), B_ROPE),
        _interleave_pairs(w[:, c0:c0 + c_qk], C_DK), w[:, c0 + c_qk:]], axis=1)
    wuq = p["w_uq"][l].reshape(B_Q_RANK, B_HEADS, B_NOPE + B_ROPE)
    wuq = jnp.concatenate([wuq[:, :, :B_NOPE].reshape(B_Q_RANK, -1),
                           _interleave_pairs(wuq[:, :, B_NOPE:].reshape(B_Q_RANK, -1), B_ROPE)], axis=1)
    wukv = p["w_ukv"][l].reshape(B_KV_RANK, B_HEADS, B_NOPE + B_DV)
    wukv = jnp.concatenate([wukv[:, :, :B_NOPE].reshape(B_KV_RANK, -1), wukv[:, :, B_NOPE:].reshape(B_KV_RANK, -1)], axis=1)
    lp = p["diff_lambda"][l].astype(F32)
    lam_init = 0.8 - 0.6 * math.exp(-0.3 * l)
    lam = jnp.exp(jnp.sum(lp[0] * lp[1])) - jnp.exp(jnp.sum(lp[2] * lp[3])) + lam_init

    def lane_lg(dec):
        lg = jax.nn.log_sigmoid(dec.astype(F32)).reshape(C_HEADS // 2, 2)
        return jnp.tile(jnp.repeat(lg, C_DK // 2, axis=1), (1, 2)).reshape(C_HEADS // 2, 1, LANES)

    return dict(
        norm_mix=p["norm_mix"][l][None], w_in=w_in.astype(BF16),
        lam=lam.reshape(1), lam_init=lam_init, diff_norm=p["diff_norm"][l][:, None],
        mla_q_norm=p["mla_q_norm"][l][None], mla_kv_norm=p["mla_kv_norm"][l][None],
        w_uq=wuq.astype(BF16), w_ukv=wukv.astype(BF16),
        lgf=lane_lg(p["ret_decay_fwd"][l]), lgb=lane_lg(p["ret_decay_bwd"][l]), ret_norm=p["ret_norm"][l][None],
        w_o=p["w_o"][l].astype(BF16), norm_ffn=p["norm_ffn"][l][None],
        w_gate=p["w_gate"][l].astype(BF16), w_up=p["w_up"][l].astype(BF16),
        conv_w=p["conv_w"][l], conv_b=p["conv_b"][l][None], w_down=p["w_down"][l].astype(BF16),
    )


def _trunk(x3, layers, norm_final):
    B, S, D = x3.shape
    T = B * S
    x = x3.reshape(T, D)
    tabs_a = _rope_tables(S, ROPE_THETA, A_ROT)
    tabs_b = _rope_tables(S, ROPE_THETA, B_ROPE)
    tabs_c = _rope_tables(S, RET_THETA, C_DK)
    tm_in = _pick(T, 256)
    tm_prep = _pick(S, 512)
    tq, tk = _pick(S, 512), _pick(S, 1024)
    nq_b = 2 if S % (2 * tq) == 0 else 1
    ts = _pick(S, 1024)
    tm_out = _pick(T, 512)
    tm_ffn, tf = _pick(S, 512), _pick(layers[0]["w_gate"].shape[1], 512)
    no_lam = jnp.zeros((1,), F32)
    no_norm = jnp.ones((B_DV, 1), F32)
    for l, w in enumerate(layers):
        za, zb, zc = _in_proj(x, w["norm_mix"], w["w_in"], tm_in)
        qt, k, vt = _prep_a(za, tabs_a, B, S, tm_prep)
        oa = _attention(w["lam"], qt, k, vt, w["diff_norm"], n_maps=2, n_q=1, dk=2 * A_DK, dv=A_DV, tq=tq, tk=tk,
                        post_scale=1.0 - w["lam_init"]).reshape(T, -1)
        qt, k, vt = _prep_b(zb, w["mla_q_norm"], w["mla_kv_norm"], w["w_uq"], w["w_ukv"], tabs_b, B, S, tm_prep)
        ob = _attention(no_lam, qt.reshape(B, B_HEADS, 1, B_QK, S), k, vt, no_norm, n_maps=1, n_q=nq_b, dk=B_QK,
                        dv=B_DV, tq=tq, tk=tk, post_scale=1.0).reshape(T, -1)
        oc = _retention(zc, w["lgf"], w["lgb"], w["ret_norm"], tabs_c, B, S, ts)
        x = _out_proj(x, oa, ob, oc, w["w_o"], tm_out)
        x = _ffn(x, w["norm_ffn"], w["w_gate"], w["w_up"], w["conv_w"], w["conv_b"], w["w_down"], norm_final,
                 seq_len=S, tm=tm_ffn, tf=tf, final_norm=(l == len(layers) - 1))
    return x.reshape(B, S, D)


def kernel(x_prompt, x_sample, norm_mix, w_in, diff_lambda, diff_norm, mla_q_norm, mla_kv_norm, w_uq, w_ukv,
           ret_decay_fwd, ret_decay_bwd, ret_norm, w_o, norm_ffn, w_gate, w_up, conv_w, conv_b, w_down, norm_final):
    p = dict(norm_mix=norm_mix, w_in=w_in, diff_lambda=diff_lambda, diff_norm=diff_norm, mla_q_norm=mla_q_norm,
             mla_kv_norm=mla_kv_norm, w_uq=w_uq, w_ukv=w_ukv, ret_decay_fwd=ret_decay_fwd,
             ret_decay_bwd=ret_decay_bwd, ret_norm=ret_norm, w_o=w_o, norm_ffn=norm_ffn, w_gate=w_gate, w_up=w_up,
             conv_w=conv_w, conv_b=conv_b, w_down=w_down)
    layers = [_layer_weights(l, p) for l in range(norm_mix.shape[0])]
    gf = norm_final[None]
    return (_trunk(x_prompt, layers, gf), _trunk(x_sample, layers, gf))
```

```python
import functools
import math

import jax
import jax.numpy as jnp
from jax import lax
from jax.experimental import pallas as pl
from jax.experimental.pallas import tpu as pltpu

F32 = jnp.float32
BF16 = jnp.bfloat16

NORM_EPS = 1e-5
ROPE_THETA = 500000.0
RET_THETA = 10000.0
RET_CHUNK = 128
A_HEADS, A_DK, A_DV = 4, 64, 128
A_ROT = A_DK // 4
B_HEADS, B_Q_RANK, B_KV_RANK, B_NOPE, B_ROPE, B_DV = 6, 512, 256, 128, 64, 128
C_HEADS, C_DK, C_DV = 6, 64, 128
CONV_WIDTH = 3

LANES = 128
SUBLANES = 8
ZA_COLS = 3 * A_HEADS * 2 * A_DK
ZB_COLS = B_Q_RANK + B_KV_RANK + 2 * B_ROPE
ZC_COLS = 2 * C_HEADS * C_DK + 2 * C_HEADS * C_DV
B_QK = 2 * LANES
V_AUG = 2 * SUBLANES
KEY_SUB = 256
ATTN_STEP_QUERIES = 2048
NORM_ROWS = 2 * SUBLANES
VMEM_LIMIT = 56 * 1024 * 1024
LOG2E = math.log2(math.e)


def _cparams(*sem):
    return pltpu.CompilerParams(dimension_semantics=sem, vmem_limit_bytes=VMEM_LIMIT)


def _rms(x, g):
    return x * lax.rsqrt(jnp.mean(x * x, axis=-1, keepdims=True) + NORM_EPS) * g


def _ones_rows(n):
    return jnp.where(lax.broadcasted_iota(jnp.int32, (V_AUG, n), 0) == 0, 1.0, 0.0).astype(BF16)


def _rope_pairs(x, cos, ss):
    return x * cos + pltpu.roll(x, LANES // 2, 1) * ss


def _lane_owner(shape, axis):
    return (lax.broadcasted_iota(jnp.int32, shape, axis) // (LANES // 4)) % 2


def _in_proj_kernel(x_ref, g_ref, w_ref, cos_a, ss_a, qn_ref, kvn_ref, wuq_ref, wukv_ref, cos_b, ss_b,
                    qta_ref, ka_ref, vta_ref, qtb_ref, kb_ref, vtb_ref, zc_ref, za_sc, zb_sc):
    h = _rms(x_ref[...], g_ref[...]).astype(BF16)
    za_sc[...] = jnp.dot(h, w_ref[:, :ZA_COLS], preferred_element_type=F32)
    zb_sc[...] = jnp.dot(h, w_ref[:, ZA_COLS:ZA_COLS + ZB_COLS], preferred_element_type=F32)
    zc_ref[...] = jnp.dot(h, w_ref[:, ZA_COLS + ZB_COLS:], preferred_element_type=F32)
    _prep_a_kernel(za_sc, cos_a, ss_a, qta_ref, ka_ref, vta_ref)
    _prep_b_kernel(zb_sc, qn_ref, kvn_ref, wuq_ref, wukv_ref, cos_b, ss_b, qtb_ref, kb_ref, vtb_ref)


def _in_proj_prep(x, g, w, tabs_a, qn, kvn, wuq, wukv, tabs_b, B, S, tm):
    T, D = x.shape
    nb = S // tm
    rows = lambda n: pl.BlockSpec((tm, n), lambda b, i: (b * nb + i, 0))
    tab = pl.BlockSpec((tm, LANES), lambda b, i: (i, 0))
    full = lambda a: pl.BlockSpec(a.shape, lambda b, i: (0,) * a.ndim, pipeline_mode=pl.Buffered(1))
    heads_t = lambda h, n: pl.BlockSpec((None, h, n, tm), lambda b, i: (b, 0, 0, i))
    return pl.pallas_call(
        _in_proj_kernel,
        grid=(B, nb),
        in_specs=[rows(D), full(g), full(w), tab, tab, full(qn), full(kvn), full(wuq), full(wukv), tab, tab],
        out_specs=[
            pl.BlockSpec((None, A_HEADS, 2, LANES, tm), lambda b, i: (b, 0, 0, 0, i)),
            pl.BlockSpec((None, tm, A_HEADS * LANES), lambda b, i: (b, i, 0)),
            heads_t(A_HEADS, A_DV + V_AUG),
            heads_t(B_HEADS, B_QK),
            pl.BlockSpec((None, tm, B_HEADS * B_QK), lambda b, i: (b, i, 0)),
            heads_t(B_HEADS, B_DV + V_AUG),
            rows(ZC_COLS),
        ],
        out_shape=[
            jax.ShapeDtypeStruct((B, A_HEADS, 2, LANES, S), BF16),
            jax.ShapeDtypeStruct((B, S, A_HEADS * LANES), BF16),
            jax.ShapeDtypeStruct((B, A_HEADS, A_DV + V_AUG, S), BF16),
            jax.ShapeDtypeStruct((B, B_HEADS, B_QK, S), BF16),
            jax.ShapeDtypeStruct((B, S, B_HEADS * B_QK), BF16),
            jax.ShapeDtypeStruct((B, B_HEADS, B_DV + V_AUG, S), BF16),
            jax.ShapeDtypeStruct((T, ZC_COLS), F32),
        ],
        scratch_shapes=[pltpu.VMEM((tm, ZA_COLS), F32), pltpu.VMEM((tm, ZB_COLS), F32)],
        compiler_params=_cparams("parallel", "parallel"),
        name="in_proj",
    )(x, g, w, *tabs_a, qn, kvn, wuq, wukv, *tabs_b)


def _prep_a_kernel(z_ref, cos_ref, ss_ref, qt_ref, k_ref, vt_ref):
    cos, ss = cos_ref[...], ss_ref[...]
    tm = z_ref.shape[0]
    row_map = _lane_owner((LANES, tm), 0)
    hw = 2 * A_DK
    for h in range(A_HEADS):
        q = _rope_pairs(z_ref[:, h * hw:(h + 1) * hw], cos, ss) * (A_DK ** -0.5 * LOG2E)
        qt = q.T
        qt_ref[h, 0] = jnp.where(row_map == 0, qt, 0.0).astype(BF16)
        qt_ref[h, 1] = jnp.where(row_map == 1, qt, 0.0).astype(BF16)
        k = _rope_pairs(z_ref[:, (A_HEADS + h) * hw:(A_HEADS + h + 1) * hw], cos, ss)
        k_ref[:, h * hw:(h + 1) * hw] = k.astype(BF16)
        v = z_ref[:, (2 * A_HEADS + h) * hw:(2 * A_HEADS + h + 1) * hw]
        vt_ref[h, :A_DV] = v.T.astype(BF16)
        vt_ref[h, A_DV:] = _ones_rows(tm)


def _prep_b_kernel(z_ref, qn_ref, kvn_ref, wuq_ref, wukv_ref, cos_ref, ss_ref,
                   qt_ref, k_ref, vt_ref):
    cos, ss = cos_ref[...], ss_ref[...]
    tm = z_ref.shape[0]
    scale = (B_NOPE + B_ROPE) ** -0.5 * LOG2E
    owner = _lane_owner((tm, LANES), 1)
    nope_w = B_HEADS * B_NOPE

    cq = _rms(z_ref[:, :B_Q_RANK], qn_ref[...]).astype(BF16)
    q = jnp.dot(cq, wuq_ref[...], preferred_element_type=F32) * scale
    ckv = _rms(z_ref[:, B_Q_RANK:B_Q_RANK + B_KV_RANK], kvn_ref[...]).astype(BF16)
    kv = jnp.dot(ckv, wukv_ref[...], preferred_element_type=F32)
    kr_even = _rope_pairs(z_ref[:, B_Q_RANK + B_KV_RANK:ZB_COLS], cos, ss)
    kr = (kr_even.astype(BF16), pltpu.roll(kr_even, LANES // 4, 1).astype(BF16))

    for c in range(B_HEADS // 2):
        qr = _rope_pairs(q[:, nope_w + c * LANES:nope_w + (c + 1) * LANES], cos, ss)
        for e in range(2):
            h = 2 * c + e
            qt_ref[h, :LANES] = q[:, h * B_NOPE:(h + 1) * B_NOPE].T.astype(BF16)
            qt_ref[h, LANES:] = jnp.where(owner == e, qr, 0.0).T.astype(BF16)
    for h in range(B_HEADS):
        k_ref[:, h * B_QK:h * B_QK + LANES] = kv[:, h * B_NOPE:(h + 1) * B_NOPE].astype(BF16)
        k_ref[:, h * B_QK + LANES:(h + 1) * B_QK] = kr[h % 2]
        vt_ref[h, :B_DV] = kv[:, nope_w + h * B_DV:nope_w + (h + 1) * B_DV].T.astype(BF16)
        vt_ref[h, B_DV:] = _ones_rows(tm)


def _attn_kernel(lam_ref, qt_ref, k_ref, vt_ref, g_ref, o_ref, *scratch,
                 n_maps, n_q, n_grp, tq, tk, sub, nk, post_scale):
    chains = [(mi, qi) for mi in range(n_maps) for qi in range(n_q)]
    nch = len(chains)
    dv = o_ref.shape[1]
    nsub = tk // sub
    gq = n_q * tq
    s_sc = (scratch[0:nch], scratch[nch:2 * nch])
    mx_sc = (scratch[2 * nch:3 * nch], scratch[3 * nch:4 * nch])
    m_sc = scratch[4 * nch:5 * nch]
    acc_sc = scratch[5 * nch:6 * nch]

    def reset():
        for c in range(nch):
            m_sc[c][...] = jnp.full(m_sc[c].shape, -jnp.inf, F32)
            acc_sc[c][...] = jnp.zeros(acc_sc[c].shape, F32)

    def q_start(g, qi):
        return pl.multiple_of(g * gq + qi * tq, tq)

    def scores_sub(g, j, slot, r):
        kt = k_ref[pl.ds(pl.multiple_of(j * tk + r * sub, sub), sub), :]
        for c, (mi, qi) in enumerate(chains):
            s = jnp.dot(kt, qt_ref[mi, :, pl.ds(q_start(g, qi), tq)], preferred_element_type=F32)
            s_sc[slot][c][r * sub:(r + 1) * sub, :] = s
            mx = jnp.max(s, axis=0, keepdims=True)
            mx_sc[slot][c][...] = mx if r == 0 else jnp.maximum(mx_sc[slot][c][...], mx)

    def consume_sub(g, j, slot, r, stats):
        vt = vt_ref[:, pl.ds(pl.multiple_of(j * tk + r * sub, sub), sub)]
        for c in range(nch):
            m_new, alpha = stats[c]
            p = jnp.exp2(s_sc[slot][c][r * sub:(r + 1) * sub, :] - m_new).astype(BF16)
            pv = jnp.dot(vt, p, preferred_element_type=F32)
            acc = acc_sc[c]
            acc[...] = (alpha * acc[...] if r == 0 else acc[...]) + pv

    def tile(g, j, slot, nxt):
        stats = []
        for c in range(nch):
            m_old = m_sc[c][...]
            m_new = jnp.maximum(m_old, mx_sc[slot][c][...])
            stats.append((m_new, jnp.exp2(m_old - m_new)))
            m_sc[c][...] = m_new
        for r in range(nsub):
            if nxt is not None:
                scores_sub(nxt[0], nxt[1], 1 - slot, r)
            consume_sub(g, j, slot, r, stats)

    def finalize(g):
        for qi in range(n_q):
            a0 = acc_sc[qi]
            o = a0[:dv] / a0[dv:dv + 1]
            if n_maps == 2:
                a1 = acc_sc[n_q + qi]
                o = o - lam_ref[0] * (a1[:dv] / a1[dv:dv + 1])
                o = o * lax.rsqrt(jnp.mean(o * o, axis=0, keepdims=True) + NORM_EPS) * g_ref[...] * post_scale
            o_ref[pl.ds(q_start(g, qi), tq), :] = o.T.astype(o_ref.dtype)

    def group(g, last):
        def looped_pair(t, carry):
            tile(g, 2 * t, 0, (g, 2 * t + 1))
            tile(g, 2 * t + 1, 1, (g, 2 * t + 2))
            return carry

        lax.fori_loop(0, (nk - 1) // 2, looped_pair, 0)
        if nk % 2 == 0:
            tile(g, nk - 2, 0, (g, nk - 1))
        tile(g, nk - 1, (nk - 1) % 2, None if last else (g + 1, 0))
        finalize(g)
        if not last:
            reset()

    def looped_group(g, carry):
        group(g, False)
        return carry

    reset()
    for r in range(nsub):
        scores_sub(0, 0, 0, r)
    lax.fori_loop(0, n_grp - 1, looped_group, 0)
    group(n_grp - 1, True)


def _attention(lam, qt, k, vt, g, *, n_maps, n_q, dk, dv, tq, tk, post_scale):
    B, H = qt.shape[0], qt.shape[1]
    S = k.shape[1]
    nk = S // tk
    gq = n_q * tq
    n_grp = min(S // gq, ATTN_STEP_QUERIES // gq) if nk % 2 == 0 else 1
    bq = n_grp * gq
    nch = n_maps * n_q
    kern = functools.partial(_attn_kernel, n_maps=n_maps, n_q=n_q, n_grp=n_grp, tq=tq, tk=tk,
                             sub=min(tk, KEY_SUB), nk=nk, post_scale=post_scale)
    return pl.pallas_call(
        kern,
        grid=(B, H, S // bq),
        in_specs=[
            pl.BlockSpec(memory_space=pltpu.SMEM),
            pl.BlockSpec((None, None, n_maps, dk, bq), lambda b, h, i: (b, h, 0, 0, i)),
            pl.BlockSpec((None, S, dk), lambda b, h, i: (b, 0, h)),
            pl.BlockSpec((None, None, dv + V_AUG, S), lambda b, h, i: (b, h, 0, 0)),
            pl.BlockSpec((dv, 1), lambda b, h, i: (0, 0)),
        ],
        out_specs=pl.BlockSpec((None, bq, dv), lambda b, h, i: (b, i, h)),
        out_shape=jax.ShapeDtypeStruct((B, S, H * dv), BF16),
        scratch_shapes=(
            [pltpu.VMEM((tk, tq), F32)] * (2 * nch)
            + [pltpu.VMEM((1, tq), F32)] * (3 * nch)
            + [pltpu.VMEM((dv + V_AUG, tq), F32)] * nch
        ),
        compiler_params=_cparams("parallel", "parallel", "arbitrary"),
        name="attn_diff" if n_maps == 2 else "attn_mla",
    )(lam, qt, k, vt, g)


def _ret_rope(x_ref, cos_ref, ss_ref, off):
    rows = slice(off, off + RET_CHUNK)
    return _rope_pairs(x_ref[rows, :], cos_ref[rows, :], ss_ref[rows, :])


def _head_masks():
    r = _lane_owner((2 * C_DK, 2 * C_DV), 0)
    c = lax.broadcasted_iota(jnp.int32, (2 * C_DK, 2 * C_DV), 1) // C_DV
    return r == c


def _pair_views(p, qk_refs, v_refs):
    qk = [r.at[:, p * LANES:(p + 1) * LANES] for r in qk_refs]
    vs = [r.at[:, p * 2 * C_DV:(p + 1) * 2 * C_DV] for r in v_refs]
    return qk, vs


def _ret_bwd_kernel(lg_ref, q_ref, k_ref, v_ref, cos_ref, ss_ref, o_ref, r_sc, *, nc):
    C = RET_CHUNK
    pairs = C_HEADS // 2

    @pl.when(pl.program_id(1) == 0)
    def _():
        r_sc[...] = jnp.zeros(r_sc.shape, F32)

    idx = lax.broadcasted_iota(jnp.int32, (C, 2 * C_DK), 0).astype(F32)
    bd = _head_masks()
    consts = []
    for p in range(pairs):
        lg = lg_ref[p]
        consts.append((jnp.exp(lg * (C - idx)),
                       jnp.exp(lg * idx) * (C_DK ** -0.5),
                       jnp.exp(lg * C).T))

    for t in range(nc):
        off = (nc - 1 - t) * C
        for p in range(pairs):
            (q_p, k_p), (v_p, o_p) = _pair_views(p, (q_ref, k_ref), (v_ref, o_ref))
            q_dec, k_dec, decay = consts[p]
            qc = (_ret_rope(q_p, cos_ref, ss_ref, off) * q_dec).astype(BF16)
            kc = _ret_rope(k_p, cos_ref, ss_ref, off) * k_dec
            vc = v_p[off:off + C, :].astype(BF16)
            r = r_sc[p]
            o_p[off:off + C, :] = jnp.dot(qc, r.astype(BF16), preferred_element_type=F32)
            u = jnp.dot(kc.T.astype(BF16), vc, preferred_element_type=F32)
            r_sc[p] = r * decay + jnp.where(bd, u, 0.0)


def _ret_fwd_kernel(lgf_ref, lgb_ref, q_ref, k_ref, v_ref, gate_ref, xb_ref, cos_ref, ss_ref,
                    nrm_ref, o_ref, r_sc, *, nc):
    C = RET_CHUNK
    pairs = C_HEADS // 2

    @pl.when(pl.program_id(1) == 0)
    def _():
        r_sc[...] = jnp.zeros(r_sc.shape, F32)

    idx = lax.broadcasted_iota(jnp.int32, (C, 2 * C_DK), 0).astype(F32)
    bd = _head_masks()
    lane_head = _lane_owner((C, 2 * C_DK), 1)
    ii = lax.broadcasted_iota(jnp.int32, (C, C), 0)
    jj = lax.broadcasted_iota(jnp.int32, (C, C), 1)
    dist = (ii - jj).astype(F32)
    g = nrm_ref[...]
    consts = []
    for p in range(pairs):
        lgf, lgb = lgf_ref[p], lgb_ref[p]
        dmats = []
        for e in range(2):
            lane_e = e * (C_DK // 2)
            gf = lgf[:, lane_e:lane_e + 1]
            gb = lgb[:, lane_e:lane_e + 1]
            dmats.append(jnp.where(dist >= 0, jnp.exp(gf * jnp.maximum(dist, 0.0)),
                                   jnp.exp(gb * jnp.maximum(-dist, 0.0))))
        consts.append((jnp.exp(lgf * (idx + 1.0)), jnp.exp(lgf * (C - 1.0 - idx)), jnp.exp(lgf * C).T, dmats))

    def chunk_inputs(p, t):
        (q_p, k_p), _ = _pair_views(p, (q_ref, k_ref), ())
        off = t * C
        qr = _ret_rope(q_p, cos_ref, ss_ref, off)
        kr = _ret_rope(k_p, cos_ref, ss_ref, off) * (C_DK ** -0.5)
        kb = kr.astype(BF16)
        ss = [lax.dot_general(jnp.where(lane_head == e, qr, 0.0).astype(BF16), kb, (((1,), (1,)), ((), ())),
                              preferred_element_type=F32) for e in range(2)]
        return qr, kr, ss

    nxt = [chunk_inputs(p, 0) for p in range(pairs)]
    for p in range(pairs):
        for t in range(nc):
            off = t * C
            _, (v_p, gate_p, xb_p, o_p) = _pair_views(p, (), (v_ref, gate_ref, xb_ref, o_ref))
            q_dec, k_dec, decay, dmats = consts[p]
            qr, kr, ss = nxt[p]
            vc = v_p[off:off + C, :].astype(BF16)
            r = r_sc[p]
            cross = jnp.dot((qr * q_dec).astype(BF16), r.astype(BF16), preferred_element_type=F32)
            if t + 1 < nc:
                nxt[p] = chunk_inputs(p, t + 1)
            for e in range(2):
                inner = jnp.dot((ss[e] * dmats[e]).astype(BF16), vc[:, e * C_DV:(e + 1) * C_DV],
                                preferred_element_type=F32)
                ret = inner + cross[:, e * C_DV:(e + 1) * C_DV] + xb_p[off:off + C, e * C_DV:(e + 1) * C_DV]
                oc = _rms(ret, g)
                gt = gate_p[off:off + C, e * C_DV:(e + 1) * C_DV]
                gated = (gt * (1.0 / (1.0 + jnp.exp(-gt)))) * oc
                o_p[off:off + C, e * C_DV:(e + 1) * C_DV] = gated.astype(o_ref.dtype)
            u = jnp.dot((kr * k_dec).T.astype(BF16), vc, preferred_element_type=F32)
            r_sc[p] = r * decay + jnp.where(bd, u, 0.0)


def _retention(zc, lgf, lgb, nrm, tabs, B, S, ts):
    ns = S // ts
    nc = ts // RET_CHUNK
    pairs = C_HEADS // 2
    qk_w, v_w = C_HEADS * C_DK, C_HEADS * C_DV
    lg_spec = pl.BlockSpec((pairs, 1, LANES), lambda b, i: (0, 0, 0))
    state = pltpu.VMEM((pairs, 2 * C_DK, 2 * C_DV), F32)

    def specs(rev):
        pos = (lambda i: ns - 1 - i) if rev else (lambda i: i)
        return dict(
            q=pl.BlockSpec((ts, qk_w), lambda b, i: (b * ns + pos(i), 0)),
            k=pl.BlockSpec((ts, qk_w), lambda b, i: (b * ns + pos(i), 1)),
            v=pl.BlockSpec((ts, v_w), lambda b, i: (b * ns + pos(i), 1)),
            gate=pl.BlockSpec((ts, v_w), lambda b, i: (b * ns + pos(i), 2)),
            out=pl.BlockSpec((ts, v_w), lambda b, i: (b * ns + pos(i), 0)),
            tab=pl.BlockSpec((ts, LANES), lambda b, i: (pos(i), 0)),
        )

    sb_ = specs(True)
    xb = pl.pallas_call(
        functools.partial(_ret_bwd_kernel, nc=nc),
        grid=(B, ns),
        in_specs=[lg_spec, sb_["q"], sb_["k"], sb_["v"], sb_["tab"], sb_["tab"]],
        out_specs=sb_["out"],
        out_shape=jax.ShapeDtypeStruct((B * S, v_w), F32),
        scratch_shapes=[state],
        compiler_params=_cparams("parallel", "arbitrary"),
        name="ret_bwd",
    )(lgb, zc, zc, zc, *tabs)

    sf = specs(False)
    return pl.pallas_call(
        functools.partial(_ret_fwd_kernel, nc=nc),
        grid=(B, ns),
        in_specs=[lg_spec, lg_spec, sf["q"], sf["k"], sf["v"], sf["gate"], sf["out"],
                  sf["tab"], sf["tab"], pl.BlockSpec((1, C_DV), lambda b, i: (0, 0))],
        out_specs=sf["out"],
        out_shape=jax.ShapeDtypeStruct((B * S, v_w), BF16),
        scratch_shapes=[state],
        compiler_params=_cparams("parallel", "arbitrary"),
        name="ret_fwd",
    )(lgf, lgb, zc, zc, zc, zc, xb, *tabs, nrm)


def _out_proj_kernel(x_ref, oa_ref, ob_ref, oc_ref, w_ref, o_ref):
    na, nb = oa_ref.shape[1], ob_ref.shape[1]
    acc = jnp.dot(oa_ref[...], w_ref[:na], preferred_element_type=F32)
    acc += jnp.dot(ob_ref[...], w_ref[na:na + nb], preferred_element_type=F32)
    acc += jnp.dot(oc_ref[...], w_ref[na + nb:], preferred_element_type=F32)
    o_ref[...] = x_ref[...] + acc


def _out_proj(x, oa, ob, oc, w, tm):
    T, D = x.shape
    row = lambda a: pl.BlockSpec((tm, a.shape[1]), lambda i: (i, 0))
    return pl.pallas_call(
        _out_proj_kernel,
        grid=(T // tm,),
        in_specs=[row(x), row(oa), row(ob), row(oc),
                  pl.BlockSpec(w.shape, lambda i: (0, 0), pipeline_mode=pl.Buffered(1))],
        out_specs=row(x),
        out_shape=jax.ShapeDtypeStruct((T, D), F32),
        compiler_params=_cparams("parallel"),
        name="out_proj",
    )(x, oa, ob, oc, w)


def _ffn_kernel(x_ref, xp_ref, xn_ref, g_ref, wg_ref, wu_ref, cw_ref, cb_ref, wd_ref, gf_ref, o_ref,
                h_sc, acc_sc, *, seq_len, final_norm):
    tm = x_ref.shape[0]
    halo = SUBLANES
    ext = tm + 2 * halo
    i, j = pl.program_id(0), pl.program_id(1)

    @pl.when(j == 0)
    def _():
        g = g_ref[...]

        def norm_rows(c, carry):
            rows = pl.ds(pl.multiple_of(c * NORM_ROWS, NORM_ROWS), NORM_ROWS)
            h_sc[rows, :] = _rms(x_ref[rows, :], g).astype(BF16)
            return carry

        lax.fori_loop(0, tm // NORM_ROWS, norm_rows, 0, unroll=16)
        h_sc[tm:tm + halo] = _rms(xn_ref[...], g).astype(BF16)
        h_sc[tm + halo:] = _rms(xp_ref[...], g).astype(BF16)
        acc_sc[...] = jnp.zeros(acc_sc.shape, F32)

    gate = jnp.dot(h_sc[...], wg_ref[...], preferred_element_type=F32)
    up = jnp.dot(h_sc[:tm], wu_ref[...], preferred_element_type=F32)
    left = pltpu.roll(gate, 1, 0)[:tm]
    right = pltpu.roll(gate, ext - 1, 0)[:tm]
    pos = lax.rem(i * tm, seq_len) + lax.broadcasted_iota(jnp.int32, (tm, 1), 0)
    left = jnp.where(pos != 0, left, 0.0)
    right = jnp.where(pos != seq_len - 1, right, 0.0)
    cw = cw_ref[...]
    gc = cb_ref[...] + left * cw[0:1] + gate[:tm] * cw[1:2] + right * cw[2:3]
    act = (gc * (1.0 / (1.0 + jnp.exp(-gc)))) * up
    acc_sc[...] += jnp.dot(act.astype(BF16), wd_ref[...], preferred_element_type=F32)

    @pl.when(j == pl.num_programs(1) - 1)
    def _():
        y = x_ref[...] + acc_sc[...]
        if final_norm:
            y = _rms(y, gf_ref[...])
        o_ref[...] = y


def _ffn(x, g, wg, wu, cw, cb, wd, gf, *, seq_len, tm, tf, final_norm):
    T, D = x.shape
    F = wg.shape[1]
    hb = tm // SUBLANES
    last = T // SUBLANES - 1
    kern = functools.partial(_ffn_kernel, seq_len=seq_len, final_norm=final_norm)
    return pl.pallas_call(
        kern,
        grid=(T // tm, F // tf),
        in_specs=[
            pl.BlockSpec((tm, D), lambda i, j: (i, 0)),
            pl.BlockSpec((SUBLANES, D), lambda i, j: (jnp.maximum(i * hb - 1, 0), 0)),
            pl.BlockSpec((SUBLANES, D), lambda i, j: (jnp.minimum((i + 1) * hb, last), 0)),
            pl.BlockSpec((1, D), lambda i, j: (0, 0)),
            pl.BlockSpec((D, tf), lambda i, j: (0, j)),
            pl.BlockSpec((D, tf), lambda i, j: (0, j)),
            pl.BlockSpec((CONV_WIDTH, tf), lambda i, j: (0, j)),
            pl.BlockSpec((1, tf), lambda i, j: (0, j)),
            pl.BlockSpec((tf, D), lambda i, j: (j, 0)),
            pl.BlockSpec((1, D), lambda i, j: (0, 0)),
        ],
        out_specs=pl.BlockSpec((tm, D), lambda i, j: (i, 0)),
        out_shape=jax.ShapeDtypeStruct((T, D), F32),
        scratch_shapes=[pltpu.VMEM((tm + 2 * SUBLANES, D), BF16), pltpu.VMEM((tm, D), F32)],
        compiler_params=_cparams("parallel", "arbitrary"),
        name="ffn",
    )(x, x, x, g, wg, wu, cw, cb, wd, gf)


def _rope_tables(S, theta, n_rot):
    half = n_rot // 2
    quarter = LANES // 4
    pos = jnp.arange(S, dtype=F32)
    inv = jnp.power(jnp.float32(theta), -jnp.arange(half, dtype=F32) * 2.0 / n_rot)
    ang = pos[:, None] * inv[None, :]
    cos = jnp.concatenate([jnp.cos(ang), jnp.ones((S, quarter - half), F32)], axis=1)
    sin = jnp.concatenate([jnp.sin(ang), jnp.zeros((S, quarter - half), F32)], axis=1)
    return jnp.tile(cos, (1, 4)), jnp.concatenate([-sin, -sin, sin, sin], axis=1)


def _interleave_pairs(w, n_rot):
    d = w.shape[0]
    half = n_rot // 2
    quarter = LANES // 4
    w = w.reshape(d, -1, 2, LANES // 2)
    rest = w[..., n_rot:]
    split = quarter - half
    first = jnp.concatenate([w[..., :half], rest[..., :split]], axis=-1)
    second = jnp.concatenate([w[..., half:n_rot], rest[..., split:]], axis=-1)
    return jnp.stack([first, second], axis=2).reshape(d, -1)


def _pick(n, pref):
    t = min(n, pref)
    while n % t:
        t //= 2
    return t


def _layer_weights(l, p):
    w = p["w_in"][l].astype(BF16)
    d = w.shape[0]
    a_qk = 2 * A_HEADS * 2 * A_DK
    kr0 = ZA_COLS + B_Q_RANK + B_KV_RANK
    c0 = kr0 + B_ROPE
    c_qk = 2 * C_HEADS * C_DK
    w_in = jnp.concatenate([
        _interleave_pairs(w[:, :a_qk], A_ROT), w[:, a_qk:kr0],
        _interleave_pairs(jnp.concatenate([w[:, kr0:c0], jnp.zeros((d, B_ROPE), BF16)], axis=1), B_ROPE),
        _interleave_pairs(w[:, c0:c0 + c_qk], C_DK), w[:, c0 + c_qk:]], axis=1)
    wuq = p["w_uq"][l].astype(BF16).reshape(B_Q_RANK, B_HEADS, B_NOPE + B_ROPE)
    wuq = jnp.concatenate([wuq[:, :, :B_NOPE].reshape(B_Q_RANK, -1),
                           _interleave_pairs(wuq[:, :, B_NOPE:].reshape(B_Q_RANK, -1), B_ROPE)], axis=1)
    wukv = p["w_ukv"][l].astype(BF16).reshape(B_KV_RANK, B_HEADS, B_NOPE + B_DV)
    wukv = jnp.concatenate([wukv[:, :, :B_NOPE].reshape(B_KV_RANK, -1), wukv[:, :, B_NOPE:].reshape(B_KV_RANK, -1)], axis=1)
    lp = p["diff_lambda"][l].astype(F32)
    lam_init = 0.8 - 0.6 * math.exp(-0.3 * l)
    lam = jnp.exp(jnp.sum(lp[0] * lp[1])) - jnp.exp(jnp.sum(lp[2] * lp[3])) + lam_init

    def lane_lg(dec):
        lg = jax.nn.log_sigmoid(dec.astype(F32)).reshape(C_HEADS // 2, 2)
        return jnp.tile(jnp.repeat(lg, C_DK // 2, axis=1), (1, 2)).reshape(C_HEADS // 2, 1, LANES)

    return dict(
        norm_mix=p["norm_mix"][l][None], w_in=w_in,
        lam=lam.reshape(1), lam_init=lam_init, diff_norm=p["diff_norm"][l][:, None],
        mla_q_norm=p["mla_q_norm"][l][None], mla_kv_norm=p["mla_kv_norm"][l][None],
        w_uq=wuq, w_ukv=wukv,
        lgf=lane_lg(p["ret_decay_fwd"][l]), lgb=lane_lg(p["ret_decay_bwd"][l]), ret_norm=p["ret_norm"][l][None],
        w_o=p["w_o"][l].astype(BF16), norm_ffn=p["norm_ffn"][l][None],
        w_gate=p["w_gate"][l].astype(BF16), w_up=p["w_up"][l].astype(BF16),
        conv_w=p["conv_w"][l], conv_b=p["conv_b"][l][None], w_down=p["w_down"][l].astype(BF16),
    )


def _trunk(x3, layers, norm_final):
    B, S, D = x3.shape
    T = B * S
    x = x3.reshape(T, D)
    tabs_a = _rope_tables(S, ROPE_THETA, A_ROT)
    tabs_b = _rope_tables(S, ROPE_THETA, B_ROPE)
    tabs_c = _rope_tables(S, RET_THETA, C_DK)
    tm_in = _pick(S, 256)
    tq, tk = _pick(S, 512), _pick(S, 1024)
    nq_b = 2 if S % (2 * tq) == 0 else 1
    ts = _pick(S, 1024)
    tm_out = _pick(T, 512)
    tm_ffn, tf = _pick(S, 512), _pick(layers[0]["w_gate"].shape[1], 512)
    no_lam = jnp.zeros((1,), F32)
    no_norm = jnp.ones((B_DV, 1), F32)
    for l, w in enumerate(layers):
        qt, k, vt, qt_b, k_b, vt_b, zc = _in_proj_prep(
            x, w["norm_mix"], w["w_in"], tabs_a, w["mla_q_norm"], w["mla_kv_norm"], w["w_uq"], w["w_ukv"], tabs_b,
            B, S, tm_in)
        oa = _attention(w["lam"], qt, k, vt, w["diff_norm"], n_maps=2, n_q=1, dk=2 * A_DK, dv=A_DV, tq=tq, tk=tk,
                        post_scale=1.0 - w["lam_init"]).reshape(T, -1)
        ob = _attention(no_lam, qt_b.reshape(B, B_HEADS, 1, B_QK, S), k_b, vt_b, no_norm, n_maps=1, n_q=nq_b, dk=B_QK,
                        dv=B_DV, tq=tq, tk=tk, post_scale=1.0).reshape(T, -1)
        oc = _retention(zc, w["lgf"], w["lgb"], w["ret_norm"], tabs_c, B, S, ts)
        x = _out_proj(x, oa, ob, oc, w["w_o"], tm_out)
        x = _ffn(x, w["norm_ffn"], w["w_gate"], w["w_up"], w["conv_w"], w["conv_b"], w["w_down"], norm_final,
                 seq_len=S, tm=tm_ffn, tf=tf, final_norm=(l == len(layers) - 1))
    return x.reshape(B, S, D)


def kernel(x_prompt, x_sample, norm_mix, w_in, diff_lambda, diff_norm, mla_q_norm, mla_kv_norm, w_uq, w_ukv,
           ret_decay_fwd, ret_decay_bwd, ret_norm, w_o, norm_ffn, w_gate, w_up, conv_w, conv_b, w_down, norm_final):
    p = dict(norm_mix=norm_mix, w_in=w_in, diff_lambda=diff_lambda, diff_norm=diff_norm, mla_q_norm=mla_q_norm,
             mla_kv_norm=mla_kv_norm, w_uq=w_uq, w_ukv=w_ukv, ret_decay_fwd=ret_decay_fwd,
             ret_decay_bwd=ret_decay_bwd, ret_norm=ret_norm, w_o=w_o, norm_ffn=norm_ffn, w_gate=w_gate, w_up=w_up,
             conv_w=conv_w, conv_b=conv_b, w_down=w_down)
    layers = [_layer_weights(l, p) for l in range(norm_mix.shape[0])]
    gf = norm_final[None]
    return (_trunk(x_prompt, layers, gf), _trunk(x_sample, layers, gf))
```

```python
import functools
import math

import jax
import jax.numpy as jnp
import numpy as np
from jax import lax
from jax.experimental import pallas as pl
from jax.experimental.pallas import tpu as pltpu

F32 = jnp.float32
BF16 = jnp.bfloat16

NORM_EPS = 1e-5
ROPE_THETA = 500000.0
RET_THETA = 10000.0
RET_CHUNK = 128
A_HEADS, A_DK, A_DV = 4, 64, 128
A_ROT = A_DK // 4
B_HEADS, B_Q_RANK, B_KV_RANK, B_NOPE, B_ROPE, B_DV = 6, 512, 256, 128, 64, 128
C_HEADS, C_DK, C_DV = 6, 64, 128
CONV_WIDTH = 3

LANES = 128
SUBLANES = 8
ZA_COLS = 3 * A_HEADS * 2 * A_DK
ZB_COLS = B_Q_RANK + B_KV_RANK + 2 * B_ROPE
ZC_COLS = 2 * C_HEADS * C_DK + 2 * C_HEADS * C_DV
B_QK = 2 * LANES
V_AUG = 2 * SUBLANES
KEY_SUB = 256
ATTN_STEP_QUERIES = 2048
NORM_ROWS = 2 * SUBLANES
VMEM_LIMIT = 56 * 1024 * 1024
LOG2E = math.log2(math.e)


def _cparams(*sem):
    return pltpu.CompilerParams(dimension_semantics=sem, vmem_limit_bytes=VMEM_LIMIT)


def _rms(x, g):
    return x * lax.rsqrt(jnp.mean(x * x, axis=-1, keepdims=True) + NORM_EPS) * g


def _ones_rows(n):
    return jnp.where(lax.broadcasted_iota(jnp.int32, (V_AUG, n), 0) == 0, 1.0, 0.0).astype(BF16)


def _rope_pairs(x, cos, ss):
    return x * cos + pltpu.roll(x, LANES // 2, 1) * ss


def _lane_owner(shape, axis):
    return (lax.broadcasted_iota(jnp.int32, shape, axis) // (LANES // 4)) % 2


def _in_proj_kernel(x_ref, g_ref, w_ref, cos_a, ss_a, qn_ref, kvn_ref, wuq_ref, wukv_ref, cos_b, ss_b,
                    qta_ref, ka_ref, vta_ref, qtb_ref, kb_ref, vtb_ref, zc_ref, za_sc, zb_sc):
    h = _rms(x_ref[...], g_ref[...]).astype(BF16)
    za_sc[...] = jnp.dot(h, w_ref[:, :ZA_COLS], preferred_element_type=F32)
    zb_sc[...] = jnp.dot(h, w_ref[:, ZA_COLS:ZA_COLS + ZB_COLS], preferred_element_type=F32)
    zc_ref[...] = jnp.dot(h, w_ref[:, ZA_COLS + ZB_COLS:], preferred_element_type=F32)
    _prep_a_kernel(za_sc, cos_a, ss_a, qta_ref, ka_ref, vta_ref)
    _prep_b_kernel(zb_sc, qn_ref, kvn_ref, wuq_ref, wukv_ref, cos_b, ss_b, qtb_ref, kb_ref, vtb_ref)


def _in_proj_prep(x, g, w, tabs_a, qn, kvn, wuq, wukv, tabs_b, B, S, tm):
    T, D = x.shape
    nb = S // tm
    rows = lambda n: pl.BlockSpec((tm, n), lambda b, i: (b * nb + i, 0))
    tab = pl.BlockSpec((tm, LANES), lambda b, i: (i, 0))
    full = lambda a: pl.BlockSpec(a.shape, lambda b, i: (0,) * a.ndim, pipeline_mode=pl.Buffered(1))
    heads_t = lambda h, n: pl.BlockSpec((None, h, n, tm), lambda b, i: (b, 0, 0, i))
    return pl.pallas_call(
        _in_proj_kernel,
        grid=(B, nb),
        in_specs=[rows(D), full(g), full(w), tab, tab, full(qn), full(kvn), full(wuq), full(wukv), tab, tab],
        out_specs=[
            pl.BlockSpec((None, A_HEADS, 2, LANES, tm), lambda b, i: (b, 0, 0, 0, i)),
            pl.BlockSpec((None, tm, A_HEADS * LANES), lambda b, i: (b, i, 0)),
            heads_t(A_HEADS, A_DV + V_AUG),
            heads_t(B_HEADS, B_QK),
            pl.BlockSpec((None, tm, B_HEADS * B_QK), lambda b, i: (b, i, 0)),
            heads_t(B_HEADS, B_DV + V_AUG),
            rows(ZC_COLS),
        ],
        out_shape=[
            jax.ShapeDtypeStruct((B, A_HEADS, 2, LANES, S), BF16),
            jax.ShapeDtypeStruct((B, S, A_HEADS * LANES), BF16),
            jax.ShapeDtypeStruct((B, A_HEADS, A_DV + V_AUG, S), BF16),
            jax.ShapeDtypeStruct((B, B_HEADS, B_QK, S), BF16),
            jax.ShapeDtypeStruct((B, S, B_HEADS * B_QK), BF16),
            jax.ShapeDtypeStruct((B, B_HEADS, B_DV + V_AUG, S), BF16),
            jax.ShapeDtypeStruct((T, ZC_COLS), F32),
        ],
        scratch_shapes=[pltpu.VMEM((tm, ZA_COLS), F32), pltpu.VMEM((tm, ZB_COLS), F32)],
        compiler_params=_cparams("parallel", "parallel"),
        name="in_proj",
    )(x, g, w, *tabs_a, qn, kvn, wuq, wukv, *tabs_b)


def _prep_a_kernel(z_ref, cos_ref, ss_ref, qt_ref, k_ref, vt_ref):
    cos, ss = cos_ref[...], ss_ref[...]
    tm = z_ref.shape[0]
    row_map = _lane_owner((LANES, tm), 0)
    hw = 2 * A_DK
    for h in range(A_HEADS):
        q = _rope_pairs(z_ref[:, h * hw:(h + 1) * hw], cos, ss) * (A_DK ** -0.5 * LOG2E)
        qt = q.T
        qt_ref[h, 0] = jnp.where(row_map == 0, qt, 0.0).astype(BF16)
        qt_ref[h, 1] = jnp.where(row_map == 1, qt, 0.0).astype(BF16)
        k = _rope_pairs(z_ref[:, (A_HEADS + h) * hw:(A_HEADS + h + 1) * hw], cos, ss)
        k_ref[:, h * hw:(h + 1) * hw] = k.astype(BF16)
        v = z_ref[:, (2 * A_HEADS + h) * hw:(2 * A_HEADS + h + 1) * hw]
        vt_ref[h, :A_DV] = v.T.astype(BF16)
        vt_ref[h, A_DV:] = _ones_rows(tm)


def _prep_b_kernel(z_ref, qn_ref, kvn_ref, wuq_ref, wukv_ref, cos_ref, ss_ref,
                   qt_ref, k_ref, vt_ref):
    cos, ss = cos_ref[...], ss_ref[...]
    tm = z_ref.shape[0]
    scale = (B_NOPE + B_ROPE) ** -0.5 * LOG2E
    owner = _lane_owner((tm, LANES), 1)
    nope_w = B_HEADS * B_NOPE

    cq = _rms(z_ref[:, :B_Q_RANK], qn_ref[...]).astype(BF16)
    q = jnp.dot(cq, wuq_ref[...], preferred_element_type=F32) * scale
    ckv = _rms(z_ref[:, B_Q_RANK:B_Q_RANK + B_KV_RANK], kvn_ref[...]).astype(BF16)
    kv = jnp.dot(ckv, wukv_ref[...], preferred_element_type=F32)
    kr_even = _rope_pairs(z_ref[:, B_Q_RANK + B_KV_RANK:ZB_COLS], cos, ss)
    kr = (kr_even.astype(BF16), pltpu.roll(kr_even, LANES // 4, 1).astype(BF16))

    for c in range(B_HEADS // 2):
        qr = _rope_pairs(q[:, nope_w + c * LANES:nope_w + (c + 1) * LANES], cos, ss)
        for e in range(2):
            h = 2 * c + e
            qt_ref[h, :LANES] = q[:, h * B_NOPE:(h + 1) * B_NOPE].T.astype(BF16)
            qt_ref[h, LANES:] = jnp.where(owner == e, qr, 0.0).T.astype(BF16)
    for h in range(B_HEADS):
        k_ref[:, h * B_QK:h * B_QK + LANES] = kv[:, h * B_NOPE:(h + 1) * B_NOPE].astype(BF16)
        k_ref[:, h * B_QK + LANES:(h + 1) * B_QK] = kr[h % 2]
        vt_ref[h, :B_DV] = kv[:, nope_w + h * B_DV:nope_w + (h + 1) * B_DV].T.astype(BF16)
        vt_ref[h, B_DV:] = _ones_rows(tm)


def _attn_kernel(lam_ref, qt_ref, k_ref, vt_ref, g_ref, o_ref, *scratch,
                 n_maps, n_q, n_grp, tq, tk, sub, nk, post_scale):
    chains = [(mi, qi) for mi in range(n_maps) for qi in range(n_q)]
    nch = len(chains)
    dv = o_ref.shape[1]
    nsub = tk // sub
    gq = n_q * tq
    s_sc = (scratch[0:nch], scratch[nch:2 * nch])
    mx_sc = (scratch[2 * nch:3 * nch], scratch[3 * nch:4 * nch])
    m_sc = scratch[4 * nch:5 * nch]
    acc_sc = scratch[5 * nch:6 * nch]

    def reset():
        for c in range(nch):
            m_sc[c][...] = jnp.full(m_sc[c].shape, -jnp.inf, F32)
            acc_sc[c][...] = jnp.zeros(acc_sc[c].shape, F32)

    def q_start(g, qi):
        return pl.multiple_of(g * gq + qi * tq, tq)

    def scores_sub(g, j, slot, r):
        kt = k_ref[pl.ds(pl.multiple_of(j * tk + r * sub, sub), sub), :]
        for c, (mi, qi) in enumerate(chains):
            s = jnp.dot(kt, qt_ref[mi, :, pl.ds(q_start(g, qi), tq)], preferred_element_type=F32)
            s_sc[slot][c][r * sub:(r + 1) * sub, :] = s
            mx = jnp.max(s, axis=0, keepdims=True)
            mx_sc[slot][c][...] = mx if r == 0 else jnp.maximum(mx_sc[slot][c][...], mx)

    def consume_sub(g, j, slot, r, stats):
        vt = vt_ref[:, pl.ds(pl.multiple_of(j * tk + r * sub, sub), sub)]
        for c in range(nch):
            m_new, alpha = stats[c]
            p = jnp.exp2(s_sc[slot][c][r * sub:(r + 1) * sub, :] - m_new).astype(BF16)
            pv = jnp.dot(vt, p, preferred_element_type=F32)
            acc = acc_sc[c]
            acc[...] = (alpha * acc[...] if r == 0 else acc[...]) + pv

    def tile(g, j, slot, nxt):
        stats = []
        for c in range(nch):
            m_old = m_sc[c][...]
            m_new = jnp.maximum(m_old, mx_sc[slot][c][...])
            stats.append((m_new, jnp.exp2(m_old - m_new)))
            m_sc[c][...] = m_new
        for r in range(nsub):
            if nxt is not None:
                scores_sub(nxt[0], nxt[1], 1 - slot, r)
            consume_sub(g, j, slot, r, stats)

    def finalize(g):
        for qi in range(n_q):
            a0 = acc_sc[qi]
            o = a0[:dv] / a0[dv:dv + 1]
            if n_maps == 2:
                a1 = acc_sc[n_q + qi]
                o = o - lam_ref[0] * (a1[:dv] / a1[dv:dv + 1])
                o = o * lax.rsqrt(jnp.mean(o * o, axis=0, keepdims=True) + NORM_EPS) * g_ref[...] * post_scale
            o_ref[pl.ds(q_start(g, qi), tq), :] = o.T.astype(o_ref.dtype)

    def group(g, last):
        def looped_pair(t, carry):
            tile(g, 2 * t, 0, (g, 2 * t + 1))
            tile(g, 2 * t + 1, 1, (g, 2 * t + 2))
            return carry

        lax.fori_loop(0, (nk - 1) // 2, looped_pair, 0)
        if nk % 2 == 0:
            tile(g, nk - 2, 0, (g, nk - 1))
        tile(g, nk - 1, (nk - 1) % 2, None if last else (g + 1, 0))
        finalize(g)
        if not last:
            reset()

    def looped_group(g, carry):
        group(g, False)
        return carry

    reset()
    for r in range(nsub):
        scores_sub(0, 0, 0, r)
    lax.fori_loop(0, n_grp - 1, looped_group, 0)
    group(n_grp - 1, True)


def _attention(lam, qt, k, vt, g, *, n_maps, n_q, dk, dv, tq, tk, post_scale):
    B, H = qt.shape[0], qt.shape[1]
    S = k.shape[1]
    nk = S // tk
    gq = n_q * tq
    n_grp = min(S // gq, ATTN_STEP_QUERIES // gq) if nk % 2 == 0 else 1
    bq = n_grp * gq
    nch = n_maps * n_q
    kern = functools.partial(_attn_kernel, n_maps=n_maps, n_q=n_q, n_grp=n_grp, tq=tq, tk=tk,
                             sub=min(tk, KEY_SUB), nk=nk, post_scale=post_scale)
    return pl.pallas_call(
        kern,
        grid=(B, H, S // bq),
        in_specs=[
            pl.BlockSpec(memory_space=pltpu.SMEM),
            pl.BlockSpec((None, None, n_maps, dk, bq), lambda b, h, i: (b, h, 0, 0, i)),
            pl.BlockSpec((None, S, dk), lambda b, h, i: (b, 0, h)),
            pl.BlockSpec((None, None, dv + V_AUG, S), lambda b, h, i: (b, h, 0, 0)),
            pl.BlockSpec((dv, 1), lambda b, h, i: (0, 0)),
        ],
        out_specs=pl.BlockSpec((None, bq, dv), lambda b, h, i: (b, i, h)),
        out_shape=jax.ShapeDtypeStruct((B, S, H * dv), BF16),
        scratch_shapes=(
            [pltpu.VMEM((tk, tq), F32)] * (2 * nch)
            + [pltpu.VMEM((1, tq), F32)] * (3 * nch)
            + [pltpu.VMEM((dv + V_AUG, tq), F32)] * nch
        ),
        compiler_params=_cparams("parallel", "parallel", "arbitrary"),
        name="attn_diff" if n_maps == 2 else "attn_mla",
    )(lam, qt, k, vt, g)


def _ret_rope(x_ref, cos_ref, ss_ref, off):
    rows = slice(off, off + RET_CHUNK)
    return _rope_pairs(x_ref[rows, :], cos_ref[rows, :], ss_ref[rows, :])


def _head_masks():
    r = _lane_owner((2 * C_DK, 2 * C_DV), 0)
    c = lax.broadcasted_iota(jnp.int32, (2 * C_DK, 2 * C_DV), 1) // C_DV
    return r == c


def _pair_views(p, qk_refs, v_refs):
    qk = [r.at[:, p * LANES:(p + 1) * LANES] for r in qk_refs]
    vs = [r.at[:, p * 2 * C_DV:(p + 1) * 2 * C_DV] for r in v_refs]
    return qk, vs


def _ret_bwd_kernel(lg_ref, q_ref, k_ref, v_ref, cos_ref, ss_ref, o_ref, r_sc, *, nc):
    C = RET_CHUNK
    pairs = C_HEADS // 2

    @pl.when(pl.program_id(1) == 0)
    def _():
        r_sc[...] = jnp.zeros(r_sc.shape, F32)

    idx = lax.broadcasted_iota(jnp.int32, (C, 2 * C_DK), 0).astype(F32)
    bd = _head_masks()
    consts = []
    for p in range(pairs):
        lg = lg_ref[p]
        consts.append((jnp.exp(lg * (C - idx)),
                       jnp.exp(lg * idx) * (C_DK ** -0.5),
                       jnp.exp(lg * C).T))

    for t in range(nc):
        off = (nc - 1 - t) * C
        for p in range(pairs):
            (q_p, k_p), (v_p, o_p) = _pair_views(p, (q_ref, k_ref), (v_ref, o_ref))
            q_dec, k_dec, decay = consts[p]
            qc = (_ret_rope(q_p, cos_ref, ss_ref, off) * q_dec).astype(BF16)
            kc = _ret_rope(k_p, cos_ref, ss_ref, off) * k_dec
            vc = v_p[off:off + C, :].astype(BF16)
            r = r_sc[p]
            o_p[off:off + C, :] = jnp.dot(qc, r.astype(BF16), preferred_element_type=F32)
            u = jnp.dot(kc.T.astype(BF16), vc, preferred_element_type=F32)
            r_sc[p] = r * decay + jnp.where(bd, u, 0.0)


def _ret_fwd_kernel(lgf_ref, lgb_ref, q_ref, k_ref, v_ref, gate_ref, xb_ref, cos_ref, ss_ref,
                    nrm_ref, o_ref, r_sc, *, nc):
    C = RET_CHUNK
    pairs = C_HEADS // 2

    @pl.when(pl.program_id(1) == 0)
    def _():
        r_sc[...] = jnp.zeros(r_sc.shape, F32)

    idx = lax.broadcasted_iota(jnp.int32, (C, 2 * C_DK), 0).astype(F32)
    bd = _head_masks()
    lane_head = _lane_owner((C, 2 * C_DK), 1)
    ii = lax.broadcasted_iota(jnp.int32, (C, C), 0)
    jj = lax.broadcasted_iota(jnp.int32, (C, C), 1)
    dist = (ii - jj).astype(F32)
    g = nrm_ref[...]
    consts = []
    for p in range(pairs):
        lgf, lgb = lgf_ref[p], lgb_ref[p]
        dmats = []
        for e in range(2):
            lane_e = e * (C_DK // 2)
            gf = lgf[:, lane_e:lane_e + 1]
            gb = lgb[:, lane_e:lane_e + 1]
            dmats.append(jnp.where(dist >= 0, jnp.exp(gf * jnp.maximum(dist, 0.0)),
                                   jnp.exp(gb * jnp.maximum(-dist, 0.0))))
        consts.append((jnp.exp(lgf * (idx + 1.0)), jnp.exp(lgf * (C - 1.0 - idx)), jnp.exp(lgf * C).T, dmats))

    def chunk_inputs(p, t):
        (q_p, k_p), _ = _pair_views(p, (q_ref, k_ref), ())
        off = t * C
        qr = _ret_rope(q_p, cos_ref, ss_ref, off)
        kr = _ret_rope(k_p, cos_ref, ss_ref, off) * (C_DK ** -0.5)
        kb = kr.astype(BF16)
        ss = [lax.dot_general(jnp.where(lane_head == e, qr, 0.0).astype(BF16), kb, (((1,), (1,)), ((), ())),
                              preferred_element_type=F32) for e in range(2)]
        return qr, kr, ss

    nxt = [chunk_inputs(p, 0) for p in range(pairs)]
    for p in range(pairs):
        for t in range(nc):
            off = t * C
            _, (v_p, gate_p, xb_p, o_p) = _pair_views(p, (), (v_ref, gate_ref, xb_ref, o_ref))
            q_dec, k_dec, decay, dmats = consts[p]
            qr, kr, ss = nxt[p]
            vc = v_p[off:off + C, :].astype(BF16)
            r = r_sc[p]
            cross = jnp.dot((qr * q_dec).astype(BF16), r.astype(BF16), preferred_element_type=F32)
            if t + 1 < nc:
                nxt[p] = chunk_inputs(p, t + 1)
            for e in range(2):
                inner = jnp.dot((ss[e] * dmats[e]).astype(BF16), vc[:, e * C_DV:(e + 1) * C_DV],
                                preferred_element_type=F32)
                ret = inner + cross[:, e * C_DV:(e + 1) * C_DV] + xb_p[off:off + C, e * C_DV:(e + 1) * C_DV]
                oc = _rms(ret, g)
                gt = gate_p[off:off + C, e * C_DV:(e + 1) * C_DV]
                gated = (gt * (1.0 / (1.0 + jnp.exp(-gt)))) * oc
                o_p[off:off + C, e * C_DV:(e + 1) * C_DV] = gated.astype(o_ref.dtype)
            u = jnp.dot((kr * k_dec).T.astype(BF16), vc, preferred_element_type=F32)
            r_sc[p] = r * decay + jnp.where(bd, u, 0.0)


def _retention(zc, lgf, lgb, nrm, tabs, B, S, ts):
    ns = S // ts
    nc = ts // RET_CHUNK
    pairs = C_HEADS // 2
    qk_w, v_w = C_HEADS * C_DK, C_HEADS * C_DV
    lg_spec = pl.BlockSpec((pairs, 1, LANES), lambda b, i: (0, 0, 0))
    state = pltpu.VMEM((pairs, 2 * C_DK, 2 * C_DV), F32)

    def specs(rev):
        pos = (lambda i: ns - 1 - i) if rev else (lambda i: i)
        return dict(
            q=pl.BlockSpec((ts, qk_w), lambda b, i: (b * ns + pos(i), 0)),
            k=pl.BlockSpec((ts, qk_w), lambda b, i: (b * ns + pos(i), 1)),
            v=pl.BlockSpec((ts, v_w), lambda b, i: (b * ns + pos(i), 1)),
            gate=pl.BlockSpec((ts, v_w), lambda b, i: (b * ns + pos(i), 2)),
            out=pl.BlockSpec((ts, v_w), lambda b, i: (b * ns + pos(i), 0)),
            tab=pl.BlockSpec((ts, LANES), lambda b, i: (pos(i), 0)),
        )

    sb_ = specs(True)
    xb = pl.pallas_call(
        functools.partial(_ret_bwd_kernel, nc=nc),
        grid=(B, ns),
        in_specs=[lg_spec, sb_["q"], sb_["k"], sb_["v"], sb_["tab"], sb_["tab"]],
        out_specs=sb_["out"],
        out_shape=jax.ShapeDtypeStruct((B * S, v_w), F32),
        scratch_shapes=[state],
        compiler_params=_cparams("parallel", "arbitrary"),
        name="ret_bwd",
    )(lgb, zc, zc, zc, *tabs)

    sf = specs(False)
    return pl.pallas_call(
        functools.partial(_ret_fwd_kernel, nc=nc),
        grid=(B, ns),
        in_specs=[lg_spec, lg_spec, sf["q"], sf["k"], sf["v"], sf["gate"], sf["out"],
                  sf["tab"], sf["tab"], pl.BlockSpec((1, C_DV), lambda b, i: (0, 0))],
        out_specs=sf["out"],
        out_shape=jax.ShapeDtypeStruct((B * S, v_w), BF16),
        scratch_shapes=[state],
        compiler_params=_cparams("parallel", "arbitrary"),
        name="ret_fwd",
    )(lgf, lgb, zc, zc, zc, zc, xb, *tabs, nrm)


def _out_proj_kernel(x_ref, oa_ref, ob_ref, oc_ref, w_ref, o_ref):
    na, nb = oa_ref.shape[1], ob_ref.shape[1]
    acc = jnp.dot(oa_ref[...], w_ref[:na], preferred_element_type=F32)
    acc += jnp.dot(ob_ref[...], w_ref[na:na + nb], preferred_element_type=F32)
    acc += jnp.dot(oc_ref[...], w_ref[na + nb:], preferred_element_type=F32)
    o_ref[...] = x_ref[...] + acc


def _out_proj(x, oa, ob, oc, w, layer, tm):
    T, D = x.shape
    row = lambda a: pl.BlockSpec((tm, a.shape[1]), lambda i: (i, 0))
    return pl.pallas_call(
        _out_proj_kernel,
        grid=(T // tm,),
        in_specs=[row(x), row(oa), row(ob), row(oc),
                  pl.BlockSpec((None,) + w.shape[1:], lambda i: (layer, 0, 0), pipeline_mode=pl.Buffered(1))],
        out_specs=row(x),
        out_shape=jax.ShapeDtypeStruct((T, D), F32),
        compiler_params=_cparams("parallel"),
        name="out_proj",
    )(x, oa, ob, oc, w)


def _ffn_kernel(x_ref, xp_ref, xn_ref, g_ref, wg_ref, wu_ref, cw_ref, cb_ref, wd_ref, gf_ref, o_ref,
                h_sc, acc_sc, *, seq_len, final_norm):
    tm = x_ref.shape[0]
    halo = SUBLANES
    ext = tm + 2 * halo
    i, j = pl.program_id(0), pl.program_id(1)

    @pl.when(j == 0)
    def _():
        g = g_ref[...]

        def norm_rows(c, carry):
            rows = pl.ds(pl.multiple_of(c * NORM_ROWS, NORM_ROWS), NORM_ROWS)
            h_sc[rows, :] = _rms(x_ref[rows, :], g).astype(BF16)
            return carry

        lax.fori_loop(0, tm // NORM_ROWS, norm_rows, 0, unroll=16)
        h_sc[tm:tm + halo] = _rms(xn_ref[...], g).astype(BF16)
        h_sc[tm + halo:] = _rms(xp_ref[...], g).astype(BF16)
        acc_sc[...] = jnp.zeros(acc_sc.shape, F32)

    gate = jnp.dot(h_sc[...], wg_ref[...], preferred_element_type=F32)
    up = jnp.dot(h_sc[:tm], wu_ref[...], preferred_element_type=F32)
    left = pltpu.roll(gate, 1, 0)[:tm]
    right = pltpu.roll(gate, ext - 1, 0)[:tm]
    pos = lax.rem(i * tm, seq_len) + lax.broadcasted_iota(jnp.int32, (tm, 1), 0)
    left = jnp.where(pos != 0, left, 0.0)
    right = jnp.where(pos != seq_len - 1, right, 0.0)
    cw = cw_ref[...]
    gc = cb_ref[...] + left * cw[0:1] + gate[:tm] * cw[1:2] + right * cw[2:3]
    act = (gc * (1.0 / (1.0 + jnp.exp(-gc)))) * up
    acc_sc[...] += jnp.dot(act.astype(BF16), wd_ref[...], preferred_element_type=F32)

    @pl.when(j == pl.num_programs(1) - 1)
    def _():
        y = x_ref[...] + acc_sc[...]
        if final_norm:
            y = _rms(y, gf_ref[...])
        o_ref[...] = y


def _ffn(x, g, wg, wu, cw, cb, wd, gf, *, layer, seq_len, tm, tf, final_norm):
    T, D = x.shape
    F = wg.shape[2]
    hb = tm // SUBLANES
    last = T // SUBLANES - 1
    kern = functools.partial(_ffn_kernel, seq_len=seq_len, final_norm=final_norm)
    return pl.pallas_call(
        kern,
        grid=(T // tm, F // tf),
        in_specs=[
            pl.BlockSpec((tm, D), lambda i, j: (i, 0)),
            pl.BlockSpec((SUBLANES, D), lambda i, j: (jnp.maximum(i * hb - 1, 0), 0)),
            pl.BlockSpec((SUBLANES, D), lambda i, j: (jnp.minimum((i + 1) * hb, last), 0)),
            pl.BlockSpec((1, D), lambda i, j: (0, 0)),
            pl.BlockSpec((None, D, tf), lambda i, j: (layer, 0, j)),
            pl.BlockSpec((None, D, tf), lambda i, j: (layer, 0, j)),
            pl.BlockSpec((CONV_WIDTH, tf), lambda i, j: (0, j)),
            pl.BlockSpec((1, tf), lambda i, j: (0, j)),
            pl.BlockSpec((None, tf, D), lambda i, j: (layer, j, 0)),
            pl.BlockSpec((1, D), lambda i, j: (0, 0)),
        ],
        out_specs=pl.BlockSpec((tm, D), lambda i, j: (i, 0)),
        out_shape=jax.ShapeDtypeStruct((T, D), F32),
        scratch_shapes=[pltpu.VMEM((tm + 2 * SUBLANES, D), BF16), pltpu.VMEM((tm, D), F32)],
        compiler_params=_cparams("parallel", "arbitrary"),
        name="ffn",
    )(x, x, x, g, wg, wu, cw, cb, wd, gf)


def _rope_tables(S, theta, n_rot):
    half = n_rot // 2
    lane = np.arange(LANES)
    idx = lane % (LANES // 4)
    rot = (idx < half)[None, :]
    sign = np.where(lane < LANES // 2, -1.0, 1.0).astype(np.float32)[None, :]
    pos = jnp.arange(S, dtype=F32)
    inv = jnp.power(jnp.float32(theta), -jnp.arange(half, dtype=F32) * 2.0 / n_rot)
    ang = pos[:, None] * inv[np.minimum(idx, half - 1)][None, :]
    return jnp.where(rot, jnp.cos(ang), 1.0), jnp.where(rot, jnp.sin(ang) * sign, 0.0)


def _interleave_pairs(w, n_rot):
    d = w.shape[0]
    half = n_rot // 2
    quarter = LANES // 4
    w = w.reshape(d, -1, 2, LANES // 2)
    rest = w[..., n_rot:]
    split = quarter - half
    first = jnp.concatenate([w[..., :half], rest[..., :split]], axis=-1)
    second = jnp.concatenate([w[..., half:n_rot], rest[..., split:]], axis=-1)
    return jnp.stack([first, second], axis=2).reshape(d, -1)


def _pick(n, pref):
    t = min(n, pref)
    while n % t:
        t //= 2
    return t


def _layer_weights(l, p):
    w = p["w_in"][l].astype(BF16)
    d = w.shape[0]
    a_qk = 2 * A_HEADS * 2 * A_DK
    kr0 = ZA_COLS + B_Q_RANK + B_KV_RANK
    c0 = kr0 + B_ROPE
    c_qk = 2 * C_HEADS * C_DK
    w_in = jnp.concatenate([
        _interleave_pairs(w[:, :a_qk], A_ROT), w[:, a_qk:kr0],
        _interleave_pairs(jnp.concatenate([w[:, kr0:c0], jnp.zeros((d, B_ROPE), BF16)], axis=1), B_ROPE),
        _interleave_pairs(w[:, c0:c0 + c_qk], C_DK), w[:, c0 + c_qk:]], axis=1)
    wuq = p["w_uq"][l].astype(BF16).reshape(B_Q_RANK, B_HEADS, B_NOPE + B_ROPE)
    wuq = jnp.concatenate([wuq[:, :, :B_NOPE].reshape(B_Q_RANK, -1),
                           _interleave_pairs(wuq[:, :, B_NOPE:].reshape(B_Q_RANK, -1), B_ROPE)], axis=1)
    wukv = p["w_ukv"][l].astype(BF16).reshape(B_KV_RANK, B_HEADS, B_NOPE + B_DV)
    wukv = jnp.concatenate([wukv[:, :, :B_NOPE].reshape(B_KV_RANK, -1), wukv[:, :, B_NOPE:].reshape(B_KV_RANK, -1)], axis=1)
    lp = p["diff_lambda"][l].astype(F32)
    lam_init = 0.8 - 0.6 * math.exp(-0.3 * l)
    lam = jnp.exp(jnp.sum(lp[0] * lp[1])) - jnp.exp(jnp.sum(lp[2] * lp[3])) + lam_init

    def lane_lg(dec):
        lg = jax.nn.log_sigmoid(dec.astype(F32)).reshape(C_HEADS // 2, 2)
        return jnp.tile(jnp.repeat(lg, C_DK // 2, axis=1), (1, 2)).reshape(C_HEADS // 2, 1, LANES)

    return dict(
        norm_mix=p["norm_mix"][l][None], w_in=w_in,
        lam=lam.reshape(1), lam_init=lam_init, diff_norm=p["diff_norm"][l][:, None],
        mla_q_norm=p["mla_q_norm"][l][None], mla_kv_norm=p["mla_kv_norm"][l][None],
        w_uq=wuq, w_ukv=wukv,
        lgf=lane_lg(p["ret_decay_fwd"][l]), lgb=lane_lg(p["ret_decay_bwd"][l]), ret_norm=p["ret_norm"][l][None],
        norm_ffn=p["norm_ffn"][l][None], conv_w=p["conv_w"][l], conv_b=p["conv_b"][l][None],
    )


def _trunk(x3, layers, stacked, tabs, norm_final):
    B, S, D = x3.shape
    T = B * S
    x = x3.reshape(T, D)
    tabs_a, tabs_b, tabs_c = tabs
    tm_in = _pick(S, 256)
    tq, tk = _pick(S, 512), _pick(S, 1024)
    nq_b = 2 if S % (2 * tq) == 0 else 1
    ts = _pick(S, 1024)
    tm_out = _pick(T, 512)
    tm_ffn, tf = _pick(S, 512), _pick(stacked["w_gate"].shape[2], 512)
    no_lam = jnp.zeros((1,), F32)
    no_norm = jnp.ones((B_DV, 1), F32)
    for l, w in enumerate(layers):
        qt, k, vt, qt_b, k_b, vt_b, zc = _in_proj_prep(
            x, w["norm_mix"], w["w_in"], tabs_a, w["mla_q_norm"], w["mla_kv_norm"], w["w_uq"], w["w_ukv"], tabs_b,
            B, S, tm_in)
        oa = _attention(w["lam"], qt, k, vt, w["diff_norm"], n_maps=2, n_q=1, dk=2 * A_DK, dv=A_DV, tq=tq, tk=tk,
                        post_scale=1.0 - w["lam_init"]).reshape(T, -1)
        ob = _attention(no_lam, qt_b.reshape(B, B_HEADS, 1, B_QK, S), k_b, vt_b, no_norm, n_maps=1, n_q=nq_b, dk=B_QK,
                        dv=B_DV, tq=tq, tk=tk, post_scale=1.0).reshape(T, -1)
        oc = _retention(zc, w["lgf"], w["lgb"], w["ret_norm"], tabs_c, B, S, ts)
        x = _out_proj(x, oa, ob, oc, stacked["w_o"], l, tm_out)
        x = _ffn(x, w["norm_ffn"], stacked["w_gate"], stacked["w_up"], w["conv_w"], w["conv_b"], stacked["w_down"],
                 norm_final, layer=l, seq_len=S, tm=tm_ffn, tf=tf, final_norm=(l == len(layers) - 1))
    return x.reshape(B, S, D)


def kernel(x_prompt, x_sample, norm_mix, w_in, diff_lambda, diff_norm, mla_q_norm, mla_kv_norm, w_uq, w_ukv,
           ret_decay_fwd, ret_decay_bwd, ret_norm, w_o, norm_ffn, w_gate, w_up, conv_w, conv_b, w_down, norm_final):
    p = dict(norm_mix=norm_mix, w_in=w_in, diff_lambda=diff_lambda, diff_norm=diff_norm, mla_q_norm=mla_q_norm,
             mla_kv_norm=mla_kv_norm, w_uq=w_uq, w_ukv=w_ukv, ret_decay_fwd=ret_decay_fwd,
             ret_decay_bwd=ret_decay_bwd, ret_norm=ret_norm, norm_ffn=norm_ffn, conv_w=conv_w, conv_b=conv_b)
    layers = [_layer_weights(l, p) for l in range(norm_mix.shape[0])]
    stacked = dict(w_o=w_o.astype(BF16), w_gate=w_gate.astype(BF16), w_up=w_up.astype(BF16),
                   w_down=w_down.astype(BF16))
    s_max = max(x_prompt.shape[1], x_sample.shape[1])
    tabs = (_rope_tables(s_max, ROPE_THETA, A_ROT), _rope_tables(s_max, ROPE_THETA, B_ROPE),
            _rope_tables(s_max, RET_THETA, C_DK))
    gf = norm_final[None]
    return (_trunk(x_prompt, layers, stacked, tabs, gf), _trunk(x_sample, layers, stacked, tabs, gf))
```

```python
import functools
import math

import jax
import jax.numpy as jnp
import numpy as np
from jax import lax
from jax.experimental import pallas as pl
from jax.experimental.pallas import tpu as pltpu

F32 = jnp.float32
BF16 = jnp.bfloat16

NORM_EPS = 1e-5
ROPE_THETA = 500000.0
RET_THETA = 10000.0
RET_CHUNK = 128
A_HEADS, A_DK, A_DV = 4, 64, 128
A_ROT = A_DK // 4
B_HEADS, B_Q_RANK, B_KV_RANK, B_NOPE, B_ROPE, B_DV = 6, 512, 256, 128, 64, 128
C_HEADS, C_DK, C_DV = 6, 64, 128
CONV_WIDTH = 3

LANES = 128
SUBLANES = 8
ZA_COLS = 3 * A_HEADS * 2 * A_DK
ZB_COLS = B_Q_RANK + B_KV_RANK + 2 * B_ROPE
ZC_COLS = 2 * C_HEADS * C_DK + 2 * C_HEADS * C_DV
B_QK = 2 * LANES
V_AUG = 2 * SUBLANES
KEY_SUB = 256
ATTN_STEP_QUERIES = 2048
NORM_ROWS = 2 * SUBLANES
VMEM_LIMIT = 56 * 1024 * 1024
LOG2E = math.log2(math.e)


def _cparams(*sem):
    return pltpu.CompilerParams(dimension_semantics=sem, vmem_limit_bytes=VMEM_LIMIT)


def _rms(x, g):
    return x * lax.rsqrt(jnp.mean(x * x, axis=-1, keepdims=True) + NORM_EPS) * g


def _ones_rows(n):
    return jnp.where(lax.broadcasted_iota(jnp.int32, (V_AUG, n), 0) == 0, 1.0, 0.0).astype(BF16)


def _rope_pairs(x, cos, ss):
    return x * cos + pltpu.roll(x, LANES // 2, 1) * ss


def _lane_owner(shape, axis):
    return (lax.broadcasted_iota(jnp.int32, shape, axis) // (LANES // 4)) % 2


def _in_proj_kernel(x_ref, g_ref, w_ref, cos_a, ss_a, qn_ref, kvn_ref, wuq_ref, wukv_ref, cos_b, ss_b,
                    qta_ref, ka_ref, vta_ref, qtb_ref, kb_ref, vtb_ref, zc_ref, za_sc, zb_sc):
    h = _rms(x_ref[...], g_ref[...]).astype(BF16)
    za_sc[...] = jnp.dot(h, w_ref[:, :ZA_COLS], preferred_element_type=F32)
    zb_sc[...] = jnp.dot(h, w_ref[:, ZA_COLS:ZA_COLS + ZB_COLS], preferred_element_type=F32)
    zc_ref[...] = jnp.dot(h, w_ref[:, ZA_COLS + ZB_COLS:], preferred_element_type=F32)
    _prep_a_kernel(za_sc, cos_a, ss_a, qta_ref, ka_ref, vta_ref)
    _prep_b_kernel(zb_sc, qn_ref, kvn_ref, wuq_ref, wukv_ref, cos_b, ss_b, qtb_ref, kb_ref, vtb_ref)


def _in_proj_prep(x, g, w, tabs_a, qn, kvn, wuq, wukv, tabs_b, B, S, tm):
    T, D = x.shape
    nb = S // tm
    rows = lambda n: pl.BlockSpec((tm, n), lambda b, i: (b * nb + i, 0))
    tab = pl.BlockSpec((tm, LANES), lambda b, i: (i, 0))
    full = lambda a: pl.BlockSpec(a.shape, lambda b, i: (0,) * a.ndim, pipeline_mode=pl.Buffered(1))
    heads_t = lambda h, n: pl.BlockSpec((None, h, n, tm), lambda b, i: (b, 0, 0, i))
    return pl.pallas_call(
        _in_proj_kernel,
        grid=(B, nb),
        in_specs=[rows(D), full(g), full(w), tab, tab, full(qn), full(kvn), full(wuq), full(wukv), tab, tab],
        out_specs=[
            pl.BlockSpec((None, A_HEADS, 2, LANES, tm), lambda b, i: (b, 0, 0, 0, i)),
            pl.BlockSpec((None, tm, A_HEADS * LANES), lambda b, i: (b, i, 0)),
            heads_t(A_HEADS, A_DV + V_AUG),
            heads_t(B_HEADS, B_QK),
            pl.BlockSpec((None, tm, B_HEADS * B_QK), lambda b, i: (b, i, 0)),
            heads_t(B_HEADS, B_DV + V_AUG),
            rows(ZC_COLS),
        ],
        out_shape=[
            jax.ShapeDtypeStruct((B, A_HEADS, 2, LANES, S), BF16),
            jax.ShapeDtypeStruct((B, S, A_HEADS * LANES), BF16),
            jax.ShapeDtypeStruct((B, A_HEADS, A_DV + V_AUG, S), BF16),
            jax.ShapeDtypeStruct((B, B_HEADS, B_QK, S), BF16),
            jax.ShapeDtypeStruct((B, S, B_HEADS * B_QK), BF16),
            jax.ShapeDtypeStruct((B, B_HEADS, B_DV + V_AUG, S), BF16),
            jax.ShapeDtypeStruct((T, ZC_COLS), F32),
        ],
        scratch_shapes=[pltpu.VMEM((tm, ZA_COLS), F32), pltpu.VMEM((tm, ZB_COLS), F32)],
        compiler_params=_cparams("parallel", "parallel"),
        name="in_proj",
    )(x, g, w, *tabs_a, qn, kvn, wuq, wukv, *tabs_b)


def _prep_a_kernel(z_ref, cos_ref, ss_ref, qt_ref, k_ref, vt_ref):
    cos, ss = cos_ref[...], ss_ref[...]
    tm = z_ref.shape[0]
    row_map = _lane_owner((LANES, tm), 0)
    hw = 2 * A_DK
    for h in range(A_HEADS):
        q = _rope_pairs(z_ref[:, h * hw:(h + 1) * hw], cos, ss) * (A_DK ** -0.5 * LOG2E)
        qt = q.T
        qt_ref[h, 0] = jnp.where(row_map == 0, qt, 0.0).astype(BF16)
        qt_ref[h, 1] = jnp.where(row_map == 1, qt, 0.0).astype(BF16)
        k = _rope_pairs(z_ref[:, (A_HEADS + h) * hw:(A_HEADS + h + 1) * hw], cos, ss)
        k_ref[:, h * hw:(h + 1) * hw] = k.astype(BF16)
        v = z_ref[:, (2 * A_HEADS + h) * hw:(2 * A_HEADS + h + 1) * hw]
        vt_ref[h, :A_DV] = v.T.astype(BF16)
        vt_ref[h, A_DV:] = _ones_rows(tm)


def _prep_b_kernel(z_ref, qn_ref, kvn_ref, wuq_ref, wukv_ref, cos_ref, ss_ref,
                   qt_ref, k_ref, vt_ref):
    cos, ss = cos_ref[...], ss_ref[...]
    tm = z_ref.shape[0]
    scale = (B_NOPE + B_ROPE) ** -0.5 * LOG2E
    owner = _lane_owner((tm, LANES), 1)
    nope_w = B_HEADS * B_NOPE

    cq = _rms(z_ref[:, :B_Q_RANK], qn_ref[...]).astype(BF16)
    q = jnp.dot(cq, wuq_ref[...], preferred_element_type=F32) * scale
    ckv = _rms(z_ref[:, B_Q_RANK:B_Q_RANK + B_KV_RANK], kvn_ref[...]).astype(BF16)
    kv = jnp.dot(ckv, wukv_ref[...], preferred_element_type=F32)
    kr_even = _rope_pairs(z_ref[:, B_Q_RANK + B_KV_RANK:ZB_COLS], cos, ss)
    kr = (kr_even.astype(BF16), pltpu.roll(kr_even, LANES // 4, 1).astype(BF16))

    for c in range(B_HEADS // 2):
        qr = _rope_pairs(q[:, nope_w + c * LANES:nope_w + (c + 1) * LANES], cos, ss)
        for e in range(2):
            h = 2 * c + e
            qt_ref[h, :LANES] = q[:, h * B_NOPE:(h + 1) * B_NOPE].T.astype(BF16)
            qt_ref[h, LANES:] = jnp.where(owner == e, qr, 0.0).T.astype(BF16)
    for h in range(B_HEADS):
        k_ref[:, h * B_QK:h * B_QK + LANES] = kv[:, h * B_NOPE:(h + 1) * B_NOPE].astype(BF16)
        k_ref[:, h * B_QK + LANES:(h + 1) * B_QK] = kr[h % 2]
        vt_ref[h, :B_DV] = kv[:, nope_w + h * B_DV:nope_w + (h + 1) * B_DV].T.astype(BF16)
        vt_ref[h, B_DV:] = _ones_rows(tm)


def _attn_kernel(lam_ref, qt_ref, k_ref, vt_ref, g_ref, o_ref, *scratch,
                 n_maps, n_q, n_grp, tq, tk, sub, nk, post_scale):
    chains = [(mi, qi) for mi in range(n_maps) for qi in range(n_q)]
    nch = len(chains)
    dv = o_ref.shape[1]
    nsub = tk // sub
    gq = n_q * tq
    s_sc = (scratch[0:nch], scratch[nch:2 * nch])
    mx_sc = (scratch[2 * nch:3 * nch], scratch[3 * nch:4 * nch])
    m_sc = scratch[4 * nch:5 * nch]
    acc_sc = scratch[5 * nch:6 * nch]

    def reset():
        for c in range(nch):
            m_sc[c][...] = jnp.full(m_sc[c].shape, -jnp.inf, F32)
            acc_sc[c][...] = jnp.zeros(acc_sc[c].shape, F32)

    def q_start(g, qi):
        return pl.multiple_of(g * gq + qi * tq, tq)

    def scores_sub(g, j, slot, r):
        kt = k_ref[pl.ds(pl.multiple_of(j * tk + r * sub, sub), sub), :]
        for c, (mi, qi) in enumerate(chains):
            s = jnp.dot(kt, qt_ref[mi, :, pl.ds(q_start(g, qi), tq)], preferred_element_type=F32)
            s_sc[slot][c][r * sub:(r + 1) * sub, :] = s
            mx = jnp.max(s, axis=0, keepdims=True)
            mx_sc[slot][c][...] = mx if r == 0 else jnp.maximum(mx_sc[slot][c][...], mx)

    def consume_sub(g, j, slot, r, stats):
        vt = vt_ref[:, pl.ds(pl.multiple_of(j * tk + r * sub, sub), sub)]
        for c in range(nch):
            m_new, alpha = stats[c]
            p = jnp.exp2(s_sc[slot][c][r * sub:(r + 1) * sub, :] - m_new).astype(BF16)
            pv = jnp.dot(vt, p, preferred_element_type=F32)
            acc = acc_sc[c]
            acc[...] = (alpha * acc[...] if r == 0 else acc[...]) + pv

    def tile(g, j, slot, nxt):
        stats = []
        for c in range(nch):
            m_old = m_sc[c][...]
            m_new = jnp.maximum(m_old, mx_sc[slot][c][...])
            stats.append((m_new, jnp.exp2(m_old - m_new)))
            m_sc[c][...] = m_new
        for r in range(nsub):
            if nxt is not None:
                scores_sub(nxt[0], nxt[1], 1 - slot, r)
            consume_sub(g, j, slot, r, stats)

    def finalize(g):
        for qi in range(n_q):
            a0 = acc_sc[qi]
            o = a0[:dv] / a0[dv:dv + 1]
            if n_maps == 2:
                a1 = acc_sc[n_q + qi]
                o = o - lam_ref[0] * (a1[:dv] / a1[dv:dv + 1])
                o = o * lax.rsqrt(jnp.mean(o * o, axis=0, keepdims=True) + NORM_EPS) * g_ref[...] * post_scale
            o_ref[pl.ds(q_start(g, qi), tq), :] = o.T.astype(o_ref.dtype)

    def group(g, last):
        def looped_pair(t, carry):
            tile(g, 2 * t, 0, (g, 2 * t + 1))
            tile(g, 2 * t + 1, 1, (g, 2 * t + 2))
            return carry

        lax.fori_loop(0, (nk - 1) // 2, looped_pair, 0)
        if nk % 2 == 0:
            tile(g, nk - 2, 0, (g, nk - 1))
        tile(g, nk - 1, (nk - 1) % 2, None if last else (g + 1, 0))
        finalize(g)
        if not last:
            reset()

    def looped_group(g, carry):
        group(g, False)
        return carry

    reset()
    for r in range(nsub):
        scores_sub(0, 0, 0, r)
    lax.fori_loop(0, n_grp - 1, looped_group, 0)
    group(n_grp - 1, True)


def _attention(lam, qt, k, vt, g, *, n_maps, n_q, dk, dv, tq, tk, post_scale):
    B, H = qt.shape[0], qt.shape[1]
    S = k.shape[1]
    nk = S // tk
    gq = n_q * tq
    n_grp = min(S // gq, ATTN_STEP_QUERIES // gq) if nk % 2 == 0 else 1
    bq = n_grp * gq
    nch = n_maps * n_q
    kern = functools.partial(_attn_kernel, n_maps=n_maps, n_q=n_q, n_grp=n_grp, tq=tq, tk=tk,
                             sub=min(tk, KEY_SUB), nk=nk, post_scale=post_scale)
    return pl.pallas_call(
        kern,
        grid=(B, H, S // bq),
        in_specs=[
            pl.BlockSpec(memory_space=pltpu.SMEM),
            pl.BlockSpec((None, None, n_maps, dk, bq), lambda b, h, i: (b, h, 0, 0, i)),
            pl.BlockSpec((None, S, dk), lambda b, h, i: (b, 0, h)),
            pl.BlockSpec((None, None, dv + V_AUG, S), lambda b, h, i: (b, h, 0, 0)),
            pl.BlockSpec((dv, 1), lambda b, h, i: (0, 0)),
        ],
        out_specs=pl.BlockSpec((None, bq, dv), lambda b, h, i: (b, i, h)),
        out_shape=jax.ShapeDtypeStruct((B, S, H * dv), BF16),
        scratch_shapes=(
            [pltpu.VMEM((tk, tq), F32)] * (2 * nch)
            + [pltpu.VMEM((1, tq), F32)] * (3 * nch)
            + [pltpu.VMEM((dv + V_AUG, tq), F32)] * nch
        ),
        compiler_params=_cparams("parallel", "parallel", "arbitrary"),
        name="attn_diff" if n_maps == 2 else "attn_mla",
    )(lam, qt, k, vt, g)


def _ret_rope(x_ref, cos_ref, ss_ref, off):
    rows = slice(off, off + RET_CHUNK)
    return _rope_pairs(x_ref[rows, :], cos_ref[rows, :], ss_ref[rows, :])


def _head_masks():
    r = _lane_owner((2 * C_DK, 2 * C_DV), 0)
    c = lax.broadcasted_iota(jnp.int32, (2 * C_DK, 2 * C_DV), 1) // C_DV
    return r == c


def _pair_views(p, qk_refs, v_refs):
    qk = [r.at[:, p * LANES:(p + 1) * LANES] for r in qk_refs]
    vs = [r.at[:, p * 2 * C_DV:(p + 1) * 2 * C_DV] for r in v_refs]
    return qk, vs


def _ret_bwd_kernel(lg_ref, q_ref, k_ref, v_ref, cos_ref, ss_ref, o_ref, r_sc, *, nc):
    C = RET_CHUNK
    pairs = C_HEADS // 2

    @pl.when(pl.program_id(1) == 0)
    def _():
        r_sc[...] = jnp.zeros(r_sc.shape, F32)

    idx = lax.broadcasted_iota(jnp.int32, (C, 2 * C_DK), 0).astype(F32)
    bd = _head_masks()
    consts = []
    for p in range(pairs):
        lg = lg_ref[p]
        consts.append((jnp.exp(lg * (C - idx)),
                       jnp.exp(lg * idx) * (C_DK ** -0.5),
                       jnp.exp(lg * C).T))

    for t in range(nc):
        off = (nc - 1 - t) * C
        for p in range(pairs):
            (q_p, k_p), (v_p, o_p) = _pair_views(p, (q_ref, k_ref), (v_ref, o_ref))
            q_dec, k_dec, decay = consts[p]
            qc = (_ret_rope(q_p, cos_ref, ss_ref, off) * q_dec).astype(BF16)
            kc = _ret_rope(k_p, cos_ref, ss_ref, off) * k_dec
            vc = v_p[off:off + C, :].astype(BF16)
            r = r_sc[p]
            o_p[off:off + C, :] = jnp.dot(qc, r.astype(BF16), preferred_element_type=F32)
            u = jnp.dot(kc.T.astype(BF16), vc, preferred_element_type=F32)
            r_sc[p] = r * decay + jnp.where(bd, u, 0.0)


def _ret_fwd_kernel(lgf_ref, lgb_ref, q_ref, k_ref, v_ref, gate_ref, xb_ref, cos_ref, ss_ref,
                    nrm_ref, o_ref, r_sc, *, nc):
    C = RET_CHUNK
    pairs = C_HEADS // 2

    @pl.when(pl.program_id(1) == 0)
    def _():
        r_sc[...] = jnp.zeros(r_sc.shape, F32)

    idx = lax.broadcasted_iota(jnp.int32, (C, 2 * C_DK), 0).astype(F32)
    bd = _head_masks()
    lane_head = _lane_owner((C, 2 * C_DK), 1)
    ii = lax.broadcasted_iota(jnp.int32, (C, C), 0)
    jj = lax.broadcasted_iota(jnp.int32, (C, C), 1)
    dist = (ii - jj).astype(F32)
    g = nrm_ref[...]
    consts = []
    for p in range(pairs):
        lgf, lgb = lgf_ref[p], lgb_ref[p]
        dmats = []
        for e in range(2):
            lane_e = e * (C_DK // 2)
            gf = lgf[:, lane_e:lane_e + 1]
            gb = lgb[:, lane_e:lane_e + 1]
            dmats.append(jnp.where(dist >= 0, jnp.exp(gf * jnp.maximum(dist, 0.0)),
                                   jnp.exp(gb * jnp.maximum(-dist, 0.0))))
        consts.append((jnp.exp(lgf * (idx + 1.0)), jnp.exp(lgf * (C - 1.0 - idx)), jnp.exp(lgf * C).T, dmats))

    def chunk_inputs(p, t):
        (q_p, k_p), _ = _pair_views(p, (q_ref, k_ref), ())
        off = t * C
        qr = _ret_rope(q_p, cos_ref, ss_ref, off)
        kr = _ret_rope(k_p, cos_ref, ss_ref, off) * (C_DK ** -0.5)
        kb = kr.astype(BF16)
        ss = [lax.dot_general(jnp.where(lane_head == e, qr, 0.0).astype(BF16), kb, (((1,), (1,)), ((), ())),
                              preferred_element_type=F32) for e in range(2)]
        return qr, kr, ss

    nxt = [chunk_inputs(p, 0) for p in range(pairs)]
    for p in range(pairs):
        for t in range(nc):
            off = t * C
            _, (v_p, gate_p, xb_p, o_p) = _pair_views(p, (), (v_ref, gate_ref, xb_ref, o_ref))
            q_dec, k_dec, decay, dmats = consts[p]
            qr, kr, ss = nxt[p]
            vc = v_p[off:off + C, :].astype(BF16)
            r = r_sc[p]
            cross = jnp.dot((qr * q_dec).astype(BF16), r.astype(BF16), preferred_element_type=F32)
            if t + 1 < nc:
                nxt[p] = chunk_inputs(p, t + 1)
            for e in range(2):
                inner = jnp.dot((ss[e] * dmats[e]).astype(BF16), vc[:, e * C_DV:(e + 1) * C_DV],
                                preferred_element_type=F32)
                ret = inner + cross[:, e * C_DV:(e + 1) * C_DV] + xb_p[off:off + C, e * C_DV:(e + 1) * C_DV]
                oc = _rms(ret, g)
                gt = gate_p[off:off + C, e * C_DV:(e + 1) * C_DV]
                gated = (gt * (1.0 / (1.0 + jnp.exp(-gt)))) * oc
                o_p[off:off + C, e * C_DV:(e + 1) * C_DV] = gated.astype(o_ref.dtype)
            u = jnp.dot((kr * k_dec).T.astype(BF16), vc, preferred_element_type=F32)
            r_sc[p] = r * decay + jnp.where(bd, u, 0.0)


def _retention(zc, lgf, lgb, nrm, tabs, B, S, ts):
    ns = S // ts
    nc = ts // RET_CHUNK
    pairs = C_HEADS // 2
    qk_w, v_w = C_HEADS * C_DK, C_HEADS * C_DV
    lg_spec = pl.BlockSpec((pairs, 1, LANES), lambda b, i: (0, 0, 0))
    state = pltpu.VMEM((pairs, 2 * C_DK, 2 * C_DV), F32)

    def specs(rev):
        pos = (lambda i: ns - 1 - i) if rev else (lambda i: i)
        return dict(
            q=pl.BlockSpec((ts, qk_w), lambda b, i: (b * ns + pos(i), 0)),
            k=pl.BlockSpec((ts, qk_w), lambda b, i: (b * ns + pos(i), 1)),
            v=pl.BlockSpec((ts, v_w), lambda b, i: (b * ns + pos(i), 1)),
            gate=pl.BlockSpec((ts, v_w), lambda b, i: (b * ns + pos(i), 2)),
            out=pl.BlockSpec((ts, v_w), lambda b, i: (b * ns + pos(i), 0)),
            tab=pl.BlockSpec((ts, LANES), lambda b, i: (pos(i), 0)),
        )

    sb_ = specs(True)
    xb = pl.pallas_call(
        functools.partial(_ret_bwd_kernel, nc=nc),
        grid=(B, ns),
        in_specs=[lg_spec, sb_["q"], sb_["k"], sb_["v"], sb_["tab"], sb_["tab"]],
        out_specs=sb_["out"],
        out_shape=jax.ShapeDtypeStruct((B * S, v_w), F32),
        scratch_shapes=[state],
        compiler_params=_cparams("parallel", "arbitrary"),
        name="ret_bwd",
    )(lgb, zc, zc, zc, *tabs)

    sf = specs(False)
    return pl.pallas_call(
        functools.partial(_ret_fwd_kernel, nc=nc),
        grid=(B, ns),
        in_specs=[lg_spec, lg_spec, sf["q"], sf["k"], sf["v"], sf["gate"], sf["out"],
                  sf["tab"], sf["tab"], pl.BlockSpec((1, C_DV), lambda b, i: (0, 0))],
        out_specs=sf["out"],
        out_shape=jax.ShapeDtypeStruct((B * S, v_w), BF16),
        scratch_shapes=[state],
        compiler_params=_cparams("parallel", "arbitrary"),
        name="ret_fwd",
    )(lgf, lgb, zc, zc, zc, zc, xb, *tabs, nrm)


def _out_proj_kernel(x_ref, oa_ref, ob_ref, oc_ref, w_ref, o_ref):
    na, nb = oa_ref.shape[1], ob_ref.shape[1]
    acc = jnp.dot(oa_ref[...], w_ref[:na], preferred_element_type=F32)
    acc += jnp.dot(ob_ref[...], w_ref[na:na + nb], preferred_element_type=F32)
    acc += jnp.dot(oc_ref[...], w_ref[na + nb:], preferred_element_type=F32)
    o_ref[...] = x_ref[...] + acc


def _out_proj(x, oa, ob, oc, w, layer, tm):
    T, D = x.shape
    row = lambda a: pl.BlockSpec((tm, a.shape[1]), lambda i: (i, 0))
    return pl.pallas_call(
        _out_proj_kernel,
        grid=(T // tm,),
        in_specs=[row(x), row(oa), row(ob), row(oc),
                  pl.BlockSpec((None,) + w.shape[1:], lambda i: (layer, 0, 0), pipeline_mode=pl.Buffered(1))],
        out_specs=row(x),
        out_shape=jax.ShapeDtypeStruct((T, D), F32),
        compiler_params=_cparams("parallel"),
        name="out_proj",
    )(x, oa, ob, oc, w)


def _ffn_kernel(x_ref, xp_ref, xn_ref, g_ref, wg_ref, wu_ref, cw_ref, cb_ref, wd_ref, gf_ref, o_ref,
                h_sc, acc_sc, *, seq_len, final_norm):
    tm = x_ref.shape[0]
    halo = SUBLANES
    ext = tm + 2 * halo
    i, j = pl.program_id(0), pl.program_id(1)

    @pl.when(j == 0)
    def _():
        g = g_ref[...]

        def norm_rows(c, carry):
            rows = pl.ds(pl.multiple_of(c * NORM_ROWS, NORM_ROWS), NORM_ROWS)
            h_sc[rows, :] = _rms(x_ref[rows, :], g).astype(BF16)
            return carry

        lax.fori_loop(0, tm // NORM_ROWS, norm_rows, 0, unroll=16)
        h_sc[tm:tm + halo] = _rms(xn_ref[...], g).astype(BF16)
        h_sc[tm + halo:] = _rms(xp_ref[...], g).astype(BF16)
        acc_sc[...] = jnp.zeros(acc_sc.shape, F32)

    gate = jnp.dot(h_sc[...], wg_ref[...], preferred_element_type=F32)
    up = jnp.dot(h_sc[:tm], wu_ref[...], preferred_element_type=F32)
    left = pltpu.roll(gate, 1, 0)[:tm]
    right = pltpu.roll(gate, ext - 1, 0)[:tm]
    pos = lax.rem(i * tm, seq_len) + lax.broadcasted_iota(jnp.int32, (tm, 1), 0)
    left = jnp.where(pos != 0, left, 0.0)
    right = jnp.where(pos != seq_len - 1, right, 0.0)
    cw = cw_ref[...]
    gc = cb_ref[...] + left * cw[0:1] + gate[:tm] * cw[1:2] + right * cw[2:3]
    act = (gc * (1.0 / (1.0 + jnp.exp(-gc)))) * up
    acc_sc[...] += jnp.dot(act.astype(BF16), wd_ref[...], preferred_element_type=F32)

    @pl.when(j == pl.num_programs(1) - 1)
    def _():
        y = x_ref[...] + acc_sc[...]
        if final_norm:
            y = _rms(y, gf_ref[...])
        o_ref[...] = y


def _ffn(x, g, wg, wu, cw, cb, wd, gf, *, layer, seq_len, tm, tf, final_norm):
    T, D = x.shape
    F = wg.shape[2]
    hb = tm // SUBLANES
    last = T // SUBLANES - 1
    kern = functools.partial(_ffn_kernel, seq_len=seq_len, final_norm=final_norm)
    return pl.pallas_call(
        kern,
        grid=(T // tm, F // tf),
        in_specs=[
            pl.BlockSpec((tm, D), lambda i, j: (i, 0)),
            pl.BlockSpec((SUBLANES, D), lambda i, j: (jnp.maximum(i * hb - 1, 0), 0)),
            pl.BlockSpec((SUBLANES, D), lambda i, j: (jnp.minimum((i + 1) * hb, last), 0)),
            pl.BlockSpec((1, D), lambda i, j: (0, 0)),
            pl.BlockSpec((None, D, tf), lambda i, j: (layer, 0, j)),
            pl.BlockSpec((None, D, tf), lambda i, j: (layer, 0, j)),
            pl.BlockSpec((CONV_WIDTH, tf), lambda i, j: (0, j)),
            pl.BlockSpec((1, tf), lambda i, j: (0, j)),
            pl.BlockSpec((None, tf, D), lambda i, j: (layer, j, 0)),
            pl.BlockSpec((1, D), lambda i, j: (0, 0)),
        ],
        out_specs=pl.BlockSpec((tm, D), lambda i, j: (i, 0)),
        out_shape=jax.ShapeDtypeStruct((T, D), F32),
        scratch_shapes=[pltpu.VMEM((tm + 2 * SUBLANES, D), BF16), pltpu.VMEM((tm, D), F32)],
        compiler_params=_cparams("parallel", "arbitrary"),
        name="ffn",
    )(x, x, x, g, wg, wu, cw, cb, wd, gf)


def _rope_tables(S, theta, n_rot):
    half = n_rot // 2
    lane = np.arange(LANES)
    idx = lane % (LANES // 4)
    rot = (idx < half)[None, :]
    sign = np.where(lane < LANES // 2, -1.0, 1.0).astype(np.float32)[None, :]
    pos = jnp.arange(S, dtype=F32)
    inv = jnp.power(jnp.float32(theta), -jnp.arange(half, dtype=F32) * 2.0 / n_rot)
    ang = pos[:, None] * inv[np.minimum(idx, half - 1)][None, :]
    return jnp.where(rot, jnp.cos(ang), 1.0), jnp.where(rot, jnp.sin(ang) * sign, 0.0)


def _interleave_pairs(w, n_rot):
    d = w.shape[0]
    half = n_rot // 2
    quarter = LANES // 4
    w = w.reshape(d, -1, 2, LANES // 2)
    rest = w[..., n_rot:]
    split = quarter - half
    first = jnp.concatenate([w[..., :half], rest[..., :split]], axis=-1)
    second = jnp.concatenate([w[..., half:n_rot], rest[..., split:]], axis=-1)
    return jnp.stack([first, second], axis=2).reshape(d, -1)


def _pick(n, pref):
    t = min(n, pref)
    while n % t:
        t //= 2
    return t


def _layer_weights(l, p):
    w = p["w_in"][l].astype(BF16)
    d = w.shape[0]
    a_qk = 2 * A_HEADS * 2 * A_DK
    kr0 = ZA_COLS + B_Q_RANK + B_KV_RANK
    c0 = kr0 + B_ROPE
    c_qk = 2 * C_HEADS * C_DK
    w_in = jnp.concatenate([
        _interleave_pairs(w[:, :a_qk], A_ROT), w[:, a_qk:kr0],
        _interleave_pairs(jnp.concatenate([w[:, kr0:c0], jnp.zeros((d, B_ROPE), BF16)], axis=1), B_ROPE),
        _interleave_pairs(w[:, c0:c0 + c_qk], C_DK), w[:, c0 + c_qk:]], axis=1)
    wuq = p["w_uq"][l].astype(BF16).reshape(B_Q_RANK, B_HEADS, B_NOPE + B_ROPE)
    wuq = jnp.concatenate([wuq[:, :, :B_NOPE].reshape(B_Q_RANK, -1),
                           _interleave_pairs(wuq[:, :, B_NOPE:].reshape(B_Q_RANK, -1), B_ROPE)], axis=1)
    wukv = p["w_ukv"][l].astype(BF16).reshape(B_KV_RANK, B_HEADS, B_NOPE + B_DV)
    wukv = jnp.concatenate([wukv[:, :, :B_NOPE].reshape(B_KV_RANK, -1), wukv[:, :, B_NOPE:].reshape(B_KV_RANK, -1)], axis=1)
    lp = p["diff_lambda"][l].astype(F32)
    lam_init = 0.8 - 0.6 * math.exp(-0.3 * l)
    lam = jnp.exp(jnp.sum(lp[0] * lp[1])) - jnp.exp(jnp.sum(lp[2] * lp[3])) + lam_init

    def lane_lg(dec):
        lg = jax.nn.log_sigmoid(dec.astype(F32)).reshape(C_HEADS // 2, 2)
        return jnp.tile(jnp.repeat(lg, C_DK // 2, axis=1), (1, 2)).reshape(C_HEADS // 2, 1, LANES)

    return dict(
        norm_mix=p["norm_mix"][l][None], w_in=w_in,
        lam=lam.reshape(1), lam_init=lam_init, diff_norm=p["diff_norm"][l][:, None],
        mla_q_norm=p["mla_q_norm"][l][None], mla_kv_norm=p["mla_kv_norm"][l][None],
        w_uq=wuq, w_ukv=wukv,
        lgf=lane_lg(p["ret_decay_fwd"][l]), lgb=lane_lg(p["ret_decay_bwd"][l]), ret_norm=p["ret_norm"][l][None],
        norm_ffn=p["norm_ffn"][l][None], conv_w=p["conv_w"][l], conv_b=p["conv_b"][l][None],
    )


def _trunk(x3, layers, stacked, tabs, norm_final):
    B, S, D = x3.shape
    T = B * S
    x = x3.reshape(T, D)
    tabs_a, tabs_b, tabs_c = tabs
    tm_in = _pick(S, 256)
    tq, tk = _pick(S, 512), _pick(S, 2048 if S > 2048 else 1024)
    nq_b = 2 if S % (2 * tq) == 0 else 1
    ts = _pick(S, 1024)
    tm_out = _pick(T, 512)
    tm_ffn, tf = _pick(S, 512), _pick(stacked["w_gate"].shape[2], 512)
    no_lam = jnp.zeros((1,), F32)
    no_norm = jnp.ones((B_DV, 1), F32)
    for l, w in enumerate(layers):
        qt, k, vt, qt_b, k_b, vt_b, zc = _in_proj_prep(
            x, w["norm_mix"], w["w_in"], tabs_a, w["mla_q_norm"], w["mla_kv_norm"], w["w_uq"], w["w_ukv"], tabs_b,
            B, S, tm_in)
        oa = _attention(w["lam"], qt, k, vt, w["diff_norm"], n_maps=2, n_q=1, dk=2 * A_DK, dv=A_DV, tq=tq, tk=tk,
                        post_scale=1.0 - w["lam_init"]).reshape(T, -1)
        ob = _attention(no_lam, qt_b.reshape(B, B_HEADS, 1, B_QK, S), k_b, vt_b, no_norm, n_maps=1, n_q=nq_b, dk=B_QK,
                        dv=B_DV, tq=tq, tk=tk, post_scale=1.0).reshape(T, -1)
        oc = _retention(zc, w["lgf"], w["lgb"], w["ret_norm"], tabs_c, B, S, ts)
        x = _out_proj(x, oa, ob, oc, stacked["w_o"], l, tm_out)
        x = _ffn(x, w["norm_ffn"], stacked["w_gate"], stacked["w_up"], w["conv_w"], w["conv_b"], stacked["w_down"],
                 norm_final, layer=l, seq_len=S, tm=tm_ffn, tf=tf, final_norm=(l == len(layers) - 1))
    return x.reshape(B, S, D)


def kernel(x_prompt, x_sample, norm_mix, w_in, diff_lambda, diff_norm, mla_q_norm, mla_kv_norm, w_uq, w_ukv,
           ret_decay_fwd, ret_decay_bwd, ret_norm, w_o, norm_ffn, w_gate, w_up, conv_w, conv_b, w_down, norm_final):
    p = dict(norm_mix=norm_mix, w_in=w_in, diff_lambda=diff_lambda, diff_norm=diff_norm, mla_q_norm=mla_q_norm,
             mla_kv_norm=mla_kv_norm, w_uq=w_uq, w_ukv=w_ukv, ret_decay_fwd=ret_decay_fwd,
             ret_decay_bwd=ret_decay_bwd, ret_norm=ret_norm, norm_ffn=norm_ffn, conv_w=conv_w, conv_b=conv_b)
    layers = [_layer_weights(l, p) for l in range(norm_mix.shape[0])]
    stacked = dict(w_o=w_o.astype(BF16), w_gate=w_gate.astype(BF16), w_up=w_up.astype(BF16),
                   w_down=w_down.astype(BF16))
    s_max = max(x_prompt.shape[1], x_sample.shape[1])
    tabs = (_rope_tables(s_max, ROPE_THETA, A_ROT), _rope_tables(s_max, ROPE_THETA, B_ROPE),
            _rope_tables(s_max, RET_THETA, C_DK))
    gf = norm_final[None]
    return (_trunk(x_prompt, layers, stacked, tabs, gf), _trunk(x_sample, layers, stacked, tabs, gf))
```

```python
import functools
import math

import jax
import jax.numpy as jnp
import numpy as np
from jax import lax
from jax.experimental import pallas as pl
from jax.experimental.pallas import tpu as pltpu

F32 = jnp.float32
BF16 = jnp.bfloat16

NORM_EPS = 1e-5
ROPE_THETA = 500000.0
RET_THETA = 10000.0
RET_CHUNK = 128
A_HEADS, A_DK, A_DV = 4, 64, 128
A_ROT = A_DK // 4
B_HEADS, B_Q_RANK, B_KV_RANK, B_NOPE, B_ROPE, B_DV = 6, 512, 256, 128, 64, 128
C_HEADS, C_DK, C_DV = 6, 64, 128
CONV_WIDTH = 3

LANES = 128
SUBLANES = 8
ZA_COLS = 3 * A_HEADS * 2 * A_DK
ZB_COLS = B_Q_RANK + B_KV_RANK + 2 * B_ROPE
ZC_COLS = 2 * C_HEADS * C_DK + 2 * C_HEADS * C_DV
B_QK = 2 * LANES
V_AUG = 2 * SUBLANES
KEY_SUB = 256
ATTN_STEP_QUERIES = 2048
NORM_ROWS = 2 * SUBLANES
VMEM_LIMIT = 56 * 1024 * 1024
LOG2E = math.log2(math.e)


def _cparams(*sem):
    return pltpu.CompilerParams(dimension_semantics=sem, vmem_limit_bytes=VMEM_LIMIT)


def _rms(x, g):
    return x * lax.rsqrt(jnp.mean(x * x, axis=-1, keepdims=True) + NORM_EPS) * g


def _ones_rows(n):
    return jnp.where(lax.broadcasted_iota(jnp.int32, (V_AUG, n), 0) == 0, 1.0, 0.0).astype(BF16)


def _rope_pairs(x, cos, ss):
    return x * cos + pltpu.roll(x, LANES // 2, 1) * ss


def _lane_owner(shape, axis):
    return (lax.broadcasted_iota(jnp.int32, shape, axis) // (LANES // 4)) % 2


def _in_proj_kernel(x_ref, g_ref, w_ref, cos_a, ss_a, qn_ref, kvn_ref, wuq_ref, wukv_ref, cos_b, ss_b,
                    qta_ref, ka_ref, vta_ref, qtb_ref, kb_ref, vtb_ref, zc_ref, za_sc, zb_sc):
    h = _rms(x_ref[...], g_ref[...]).astype(BF16)
    za_sc[...] = jnp.dot(h, w_ref[:, :ZA_COLS], preferred_element_type=F32)
    zb_sc[...] = jnp.dot(h, w_ref[:, ZA_COLS:ZA_COLS + ZB_COLS], preferred_element_type=F32)
    zc_ref[...] = jnp.dot(h, w_ref[:, ZA_COLS + ZB_COLS:], preferred_element_type=F32)
    _prep_a_kernel(za_sc, cos_a, ss_a, qta_ref, ka_ref, vta_ref)
    _prep_b_kernel(zb_sc, qn_ref, kvn_ref, wuq_ref, wukv_ref, cos_b, ss_b, qtb_ref, kb_ref, vtb_ref)


def _in_proj_prep(x, g, w, tabs_a, qn, kvn, wuq, wukv, tabs_b, B, S, tm):
    T, D = x.shape
    nb = S // tm
    rows = lambda n: pl.BlockSpec((tm, n), lambda b, i: (b * nb + i, 0))
    tab = pl.BlockSpec((tm, LANES), lambda b, i: (i, 0))
    full = lambda a: pl.BlockSpec(a.shape, lambda b, i: (0,) * a.ndim, pipeline_mode=pl.Buffered(1))
    heads_t = lambda h, n: pl.BlockSpec((None, h, n, tm), lambda b, i: (b, 0, 0, i))
    return pl.pallas_call(
        _in_proj_kernel,
        grid=(B, nb),
        in_specs=[rows(D), full(g), full(w), tab, tab, full(qn), full(kvn), full(wuq), full(wukv), tab, tab],
        out_specs=[
            pl.BlockSpec((None, A_HEADS, 2, LANES, tm), lambda b, i: (b, 0, 0, 0, i)),
            pl.BlockSpec((None, tm, A_HEADS * LANES), lambda b, i: (b, i, 0)),
            heads_t(A_HEADS, A_DV + V_AUG),
            heads_t(B_HEADS, B_QK),
            pl.BlockSpec((None, tm, B_HEADS * B_QK), lambda b, i: (b, i, 0)),
            heads_t(B_HEADS, B_DV + V_AUG),
            rows(ZC_COLS),
        ],
        out_shape=[
            jax.ShapeDtypeStruct((B, A_HEADS, 2, LANES, S), BF16),
            jax.ShapeDtypeStruct((B, S, A_HEADS * LANES), BF16),
            jax.ShapeDtypeStruct((B, A_HEADS, A_DV + V_AUG, S), BF16),
            jax.ShapeDtypeStruct((B, B_HEADS, B_QK, S), BF16),
            jax.ShapeDtypeStruct((B, S, B_HEADS * B_QK), BF16),
            jax.ShapeDtypeStruct((B, B_HEADS, B_DV + V_AUG, S), BF16),
            jax.ShapeDtypeStruct((T, ZC_COLS), F32),
        ],
        scratch_shapes=[pltpu.VMEM((tm, ZA_COLS), F32), pltpu.VMEM((tm, ZB_COLS), F32)],
        compiler_params=_cparams("parallel", "parallel"),
        name="in_proj",
    )(x, g, w, *tabs_a, qn, kvn, wuq, wukv, *tabs_b)


def _prep_a_kernel(z_ref, cos_ref, ss_ref, qt_ref, k_ref, vt_ref):
    cos, ss = cos_ref[...], ss_ref[...]
    tm = z_ref.shape[0]
    row_map = _lane_owner((LANES, tm), 0)
    hw = 2 * A_DK
    for h in range(A_HEADS):
        q = _rope_pairs(z_ref[:, h * hw:(h + 1) * hw], cos, ss) * (A_DK ** -0.5 * LOG2E)
        qt = q.T
        qt_ref[h, 0] = jnp.where(row_map == 0, qt, 0.0).astype(BF16)
        qt_ref[h, 1] = jnp.where(row_map == 1, qt, 0.0).astype(BF16)
        k = _rope_pairs(z_ref[:, (A_HEADS + h) * hw:(A_HEADS + h + 1) * hw], cos, ss)
        k_ref[:, h * hw:(h + 1) * hw] = k.astype(BF16)
        v = z_ref[:, (2 * A_HEADS + h) * hw:(2 * A_HEADS + h + 1) * hw]
        vt_ref[h, :A_DV] = v.T.astype(BF16)
        vt_ref[h, A_DV:] = _ones_rows(tm)


def _prep_b_kernel(z_ref, qn_ref, kvn_ref, wuq_ref, wukv_ref, cos_ref, ss_ref,
                   qt_ref, k_ref, vt_ref):
    cos, ss = cos_ref[...], ss_ref[...]
    tm = z_ref.shape[0]
    scale = (B_NOPE + B_ROPE) ** -0.5 * LOG2E
    owner = _lane_owner((tm, LANES), 1)
    nope_w = B_HEADS * B_NOPE

    cq = _rms(z_ref[:, :B_Q_RANK], qn_ref[...]).astype(BF16)
    q = jnp.dot(cq, wuq_ref[...], preferred_element_type=F32) * scale
    ckv = _rms(z_ref[:, B_Q_RANK:B_Q_RANK + B_KV_RANK], kvn_ref[...]).astype(BF16)
    kv = jnp.dot(ckv, wukv_ref[...], preferred_element_type=F32)
    kr_even = _rope_pairs(z_ref[:, B_Q_RANK + B_KV_RANK:ZB_COLS], cos, ss)
    kr = (kr_even.astype(BF16), pltpu.roll(kr_even, LANES // 4, 1).astype(BF16))

    for c in range(B_HEADS // 2):
        qr = _rope_pairs(q[:, nope_w + c * LANES:nope_w + (c + 1) * LANES], cos, ss)
        for e in range(2):
            h = 2 * c + e
            qt_ref[h, :LANES] = q[:, h * B_NOPE:(h + 1) * B_NOPE].T.astype(BF16)
            qt_ref[h, LANES:] = jnp.where(owner == e, qr, 0.0).T.astype(BF16)
    for h in range(B_HEADS):
        k_ref[:, h * B_QK:h * B_QK + LANES] = kv[:, h * B_NOPE:(h + 1) * B_NOPE].astype(BF16)
        k_ref[:, h * B_QK + LANES:(h + 1) * B_QK] = kr[h % 2]
        vt_ref[h, :B_DV] = kv[:, nope_w + h * B_DV:nope_w + (h + 1) * B_DV].T.astype(BF16)
        vt_ref[h, B_DV:] = _ones_rows(tm)


def _attn_kernel(lam_ref, qt_ref, k_ref, vt_ref, g_ref, o_ref, *scratch,
                 n_maps, n_q, n_grp, tq, tk, sub, nk, post_scale):
    chains = [(mi, qi) for mi in range(n_maps) for qi in range(n_q)]
    nch = len(chains)
    dv = o_ref.shape[1]
    nsub = tk // sub
    gq = n_q * tq
    s_sc = (scratch[0:nch], scratch[nch:2 * nch])
    mx_sc = (scratch[2 * nch:3 * nch], scratch[3 * nch:4 * nch])
    m_sc = scratch[4 * nch:5 * nch]
    acc_sc = scratch[5 * nch:6 * nch]

    def reset():
        for c in range(nch):
            m_sc[c][...] = jnp.full(m_sc[c].shape, -jnp.inf, F32)
            acc_sc[c][...] = jnp.zeros(acc_sc[c].shape, F32)

    def q_start(g, qi):
        return pl.multiple_of(g * gq + qi * tq, tq)

    def scores_sub(g, j, slot, r):
        kt = k_ref[pl.ds(pl.multiple_of(j * tk + r * sub, sub), sub), :]
        for c, (mi, qi) in enumerate(chains):
            s = jnp.dot(kt, qt_ref[mi, :, pl.ds(q_start(g, qi), tq)], preferred_element_type=F32)
            s_sc[slot][c][r * sub:(r + 1) * sub, :] = s
            mx = jnp.max(s, axis=0, keepdims=True)
            mx_sc[slot][c][...] = mx if r == 0 else jnp.maximum(mx_sc[slot][c][...], mx)

    def consume_sub(g, j, slot, r, stats):
        vt = vt_ref[:, pl.ds(pl.multiple_of(j * tk + r * sub, sub), sub)]
        for c in range(nch):
            m_new, alpha = stats[c]
            p = jnp.exp2(s_sc[slot][c][r * sub:(r + 1) * sub, :] - m_new).astype(BF16)
            pv = jnp.dot(vt, p, preferred_element_type=F32)
            acc = acc_sc[c]
            acc[...] = (alpha * acc[...] if r == 0 else acc[...]) + pv

    def tile(g, j, slot, nxt):
        stats = []
        for c in range(nch):
            m_old = m_sc[c][...]
            m_new = jnp.maximum(m_old, mx_sc[slot][c][...])
            stats.append((m_new, jnp.exp2(m_old - m_new)))
            m_sc[c][...] = m_new
        for r in range(nsub):
            if nxt is not None:
                scores_sub(nxt[0], nxt[1], 1 - slot, r)
            consume_sub(g, j, slot, r, stats)

    def finalize(g):
        for qi in range(n_q):
            a0 = acc_sc[qi]
            o = a0[:dv] / a0[dv:dv + 1]
            if n_maps == 2:
                a1 = acc_sc[n_q + qi]
                o = o - lam_ref[0] * (a1[:dv] / a1[dv:dv + 1])
                o = o * lax.rsqrt(jnp.mean(o * o, axis=0, keepdims=True) + NORM_EPS) * g_ref[...] * post_scale
            o_ref[pl.ds(q_start(g, qi), tq), :] = o.T.astype(o_ref.dtype)

    def group(g, last):
        def looped_pair(t, carry):
            tile(g, 2 * t, 0, (g, 2 * t + 1))
            tile(g, 2 * t + 1, 1, (g, 2 * t + 2))
            return carry

        lax.fori_loop(0, (nk - 1) // 2, looped_pair, 0)
        if nk % 2 == 0:
            tile(g, nk - 2, 0, (g, nk - 1))
        tile(g, nk - 1, (nk - 1) % 2, None if last else (g + 1, 0))
        finalize(g)
        if not last:
            reset()

    def looped_group(g, carry):
        group(g, False)
        return carry

    reset()
    for r in range(nsub):
        scores_sub(0, 0, 0, r)
    lax.fori_loop(0, n_grp - 1, looped_group, 0)
    group(n_grp - 1, True)


def _attention(lam, qt, k, vt, g, *, n_maps, n_q, dk, dv, tq, tk, post_scale):
    B, H = qt.shape[0], qt.shape[1]
    S = k.shape[1]
    nk = S // tk
    gq = n_q * tq
    n_grp = min(S // gq, ATTN_STEP_QUERIES // gq) if nk % 2 == 0 else 1
    bq = n_grp * gq
    nch = n_maps * n_q
    kern = functools.partial(_attn_kernel, n_maps=n_maps, n_q=n_q, n_grp=n_grp, tq=tq, tk=tk,
                             sub=min(tk, KEY_SUB), nk=nk, post_scale=post_scale)
    return pl.pallas_call(
        kern,
        grid=(B, H, S // bq),
        in_specs=[
            pl.BlockSpec(memory_space=pltpu.SMEM),
            pl.BlockSpec((None, None, n_maps, dk, bq), lambda b, h, i: (b, h, 0, 0, i)),
            pl.BlockSpec((None, S, dk), lambda b, h, i: (b, 0, h)),
            pl.BlockSpec((None, None, dv + V_AUG, S), lambda b, h, i: (b, h, 0, 0)),
            pl.BlockSpec((dv, 1), lambda b, h, i: (0, 0)),
        ],
        out_specs=pl.BlockSpec((None, bq, dv), lambda b, h, i: (b, i, h)),
        out_shape=jax.ShapeDtypeStruct((B, S, H * dv), BF16),
        scratch_shapes=(
            [pltpu.VMEM((tk, tq), F32)] * (2 * nch)
            + [pltpu.VMEM((1, tq), F32)] * (3 * nch)
            + [pltpu.VMEM((dv + V_AUG, tq), F32)] * nch
        ),
        compiler_params=_cparams("parallel", "parallel", "arbitrary"),
        name="attn_diff" if n_maps == 2 else "attn_mla",
    )(lam, qt, k, vt, g)


def _ret_rope(x_ref, cos_ref, ss_ref, off):
    rows = slice(off, off + RET_CHUNK)
    return _rope_pairs(x_ref[rows, :], cos_ref[rows, :], ss_ref[rows, :])


def _head_masks():
    r = _lane_owner((2 * C_DK, 2 * C_DV), 0)
    c = lax.broadcasted_iota(jnp.int32, (2 * C_DK, 2 * C_DV), 1) // C_DV
    return r == c


def _pair_views(p, qk_refs, v_refs):
    qk = [r.at[:, p * LANES:(p + 1) * LANES] for r in qk_refs]
    vs = [r.at[:, p * 2 * C_DV:(p + 1) * 2 * C_DV] for r in v_refs]
    return qk, vs


def _ret_bwd_kernel(lg_ref, q_ref, k_ref, v_ref, cos_ref, ss_ref, o_ref, r_sc, *, nc):
    C = RET_CHUNK
    pairs = C_HEADS // 2

    @pl.when(pl.program_id(1) == 0)
    def _():
        r_sc[...] = jnp.zeros(r_sc.shape, F32)

    idx = lax.broadcasted_iota(jnp.int32, (C, 2 * C_DK), 0).astype(F32)
    bd = _head_masks()
    consts = []
    for p in range(pairs):
        lg = lg_ref[p]
        consts.append((jnp.exp(lg * (C - idx)),
                       jnp.exp(lg * idx) * (C_DK ** -0.5),
                       jnp.exp(lg * C).T))

    for t in range(nc):
        off = (nc - 1 - t) * C
        for p in range(pairs):
            (q_p, k_p), (v_p, o_p) = _pair_views(p, (q_ref, k_ref), (v_ref, o_ref))
            q_dec, k_dec, decay = consts[p]
            qc = (_ret_rope(q_p, cos_ref, ss_ref, off) * q_dec).astype(BF16)
            kc = _ret_rope(k_p, cos_ref, ss_ref, off) * k_dec
            vc = v_p[off:off + C, :].astype(BF16)
            r = r_sc[p]
            o_p[off:off + C, :] = jnp.dot(qc, r.astype(BF16), preferred_element_type=F32)
            u = jnp.dot(kc.T.astype(BF16), vc, preferred_element_type=F32)
            r_sc[p] = r * decay + jnp.where(bd, u, 0.0)


def _ret_fwd_kernel(lgf_ref, lgb_ref, q_ref, k_ref, v_ref, gate_ref, xb_ref, cos_ref, ss_ref,
                    nrm_ref, o_ref, r_sc, *, nc):
    C = RET_CHUNK
    pairs = C_HEADS // 2

    @pl.when(pl.program_id(1) == 0)
    def _():
        r_sc[...] = jnp.zeros(r_sc.shape, F32)

    idx = lax.broadcasted_iota(jnp.int32, (C, 2 * C_DK), 0).astype(F32)
    bd = _head_masks()
    lane_head = _lane_owner((C, 2 * C_DK), 1)
    ii = lax.broadcasted_iota(jnp.int32, (C, C), 0)
    jj = lax.broadcasted_iota(jnp.int32, (C, C), 1)
    dist = (ii - jj).astype(F32)
    g = nrm_ref[...]
    consts = []
    for p in range(pairs):
        lgf, lgb = lgf_ref[p], lgb_ref[p]
        dmats = []
        for e in range(2):
            lane_e = e * (C_DK // 2)
            gf = lgf[:, lane_e:lane_e + 1]
            gb = lgb[:, lane_e:lane_e + 1]
            dmats.append(jnp.where(dist >= 0, jnp.exp(gf * jnp.maximum(dist, 0.0)),
                                   jnp.exp(gb * jnp.maximum(-dist, 0.0))))
        consts.append((jnp.exp(lgf * (idx + 1.0)), jnp.exp(lgf * (C - 1.0 - idx)), jnp.exp(lgf * C).T, dmats))

    def chunk_inputs(p, t):
        (q_p, k_p), _ = _pair_views(p, (q_ref, k_ref), ())
        off = t * C
        qr = _ret_rope(q_p, cos_ref, ss_ref, off)
        kr = _ret_rope(k_p, cos_ref, ss_ref, off) * (C_DK ** -0.5)
        kb = kr.astype(BF16)
        ss = [lax.dot_general(jnp.where(lane_head == e, qr, 0.0).astype(BF16), kb, (((1,), (1,)), ((), ())),
                              preferred_element_type=F32) for e in range(2)]
        return qr, kr, ss

    nxt = [chunk_inputs(p, 0) for p in range(pairs)]
    for p in range(pairs):
        for t in range(nc):
            off = t * C
            _, (v_p, gate_p, xb_p, o_p) = _pair_views(p, (), (v_ref, gate_ref, xb_ref, o_ref))
            q_dec, k_dec, decay, dmats = consts[p]
            qr, kr, ss = nxt[p]
            vc = v_p[off:off + C, :].astype(BF16)
            r = r_sc[p]
            cross = jnp.dot((qr * q_dec).astype(BF16), r.astype(BF16), preferred_element_type=F32)
            if t + 1 < nc:
                nxt[p] = chunk_inputs(p, t + 1)
            for e in range(2):
                inner = jnp.dot((ss[e] * dmats[e]).astype(BF16), vc[:, e * C_DV:(e + 1) * C_DV],
                                preferred_element_type=F32)
                ret = inner + cross[:, e * C_DV:(e + 1) * C_DV] + xb_p[off:off + C, e * C_DV:(e + 1) * C_DV]
                oc = _rms(ret, g)
                gt = gate_p[off:off + C, e * C_DV:(e + 1) * C_DV]
                gated = (gt * (1.0 / (1.0 + jnp.exp(-gt)))) * oc
                o_p[off:off + C, e * C_DV:(e + 1) * C_DV] = gated.astype(o_ref.dtype)
            u = jnp.dot((kr * k_dec).T.astype(BF16), vc, preferred_element_type=F32)
            r_sc[p] = r * decay + jnp.where(bd, u, 0.0)


def _retention(zc, lgf, lgb, nrm, tabs, B, S, ts_bwd, ts_fwd):
    pairs = C_HEADS // 2
    qk_w, v_w = C_HEADS * C_DK, C_HEADS * C_DV
    lg_spec = pl.BlockSpec((pairs, 1, LANES), lambda b, i: (0, 0, 0))
    state = pltpu.VMEM((pairs, 2 * C_DK, 2 * C_DV), F32)

    def specs(rev, ts):
        ns = S // ts
        pos = (lambda i: ns - 1 - i) if rev else (lambda i: i)
        return dict(
            ns=ns, nc=ts // RET_CHUNK,
            q=pl.BlockSpec((ts, qk_w), lambda b, i: (b * ns + pos(i), 0)),
            k=pl.BlockSpec((ts, qk_w), lambda b, i: (b * ns + pos(i), 1)),
            v=pl.BlockSpec((ts, v_w), lambda b, i: (b * ns + pos(i), 1)),
            gate=pl.BlockSpec((ts, v_w), lambda b, i: (b * ns + pos(i), 2)),
            out=pl.BlockSpec((ts, v_w), lambda b, i: (b * ns + pos(i), 0)),
            tab=pl.BlockSpec((ts, LANES), lambda b, i: (pos(i), 0)),
        )

    sb_ = specs(True, ts_bwd)
    xb = pl.pallas_call(
        functools.partial(_ret_bwd_kernel, nc=sb_["nc"]),
        grid=(B, sb_["ns"]),
        in_specs=[lg_spec, sb_["q"], sb_["k"], sb_["v"], sb_["tab"], sb_["tab"]],
        out_specs=sb_["out"],
        out_shape=jax.ShapeDtypeStruct((B * S, v_w), F32),
        scratch_shapes=[state],
        compiler_params=_cparams("parallel", "arbitrary"),
        name="ret_bwd",
    )(lgb, zc, zc, zc, *tabs)

    sf = specs(False, ts_fwd)
    return pl.pallas_call(
        functools.partial(_ret_fwd_kernel, nc=sf["nc"]),
        grid=(B, sf["ns"]),
        in_specs=[lg_spec, lg_spec, sf["q"], sf["k"], sf["v"], sf["gate"], sf["out"],
                  sf["tab"], sf["tab"], pl.BlockSpec((1, C_DV), lambda b, i: (0, 0))],
        out_specs=sf["out"],
        out_shape=jax.ShapeDtypeStruct((B * S, v_w), BF16),
        scratch_shapes=[state],
        compiler_params=_cparams("parallel", "arbitrary"),
        name="ret_fwd",
    )(lgf, lgb, zc, zc, zc, zc, xb, *tabs, nrm)


def _out_proj_kernel(x_ref, oa_ref, ob_ref, oc_ref, w_ref, o_ref):
    na, nb = oa_ref.shape[1], ob_ref.shape[1]
    acc = jnp.dot(oa_ref[...], w_ref[:na], preferred_element_type=F32)
    acc += jnp.dot(ob_ref[...], w_ref[na:na + nb], preferred_element_type=F32)
    acc += jnp.dot(oc_ref[...], w_ref[na + nb:], preferred_element_type=F32)
    o_ref[...] = x_ref[...] + acc


def _out_proj(x, oa, ob, oc, w, layer, tm):
    T, D = x.shape
    row = lambda a: pl.BlockSpec((tm, a.shape[1]), lambda i: (i, 0))
    return pl.pallas_call(
        _out_proj_kernel,
        grid=(T // tm,),
        in_specs=[row(x), row(oa), row(ob), row(oc),
                  pl.BlockSpec((None,) + w.shape[1:], lambda i: (layer, 0, 0), pipeline_mode=pl.Buffered(1))],
        out_specs=row(x),
        out_shape=jax.ShapeDtypeStruct((T, D), F32),
        compiler_params=_cparams("parallel"),
        name="out_proj",
    )(x, oa, ob, oc, w)


def _ffn_kernel(x_ref, xp_ref, xn_ref, g_ref, wg_ref, wu_ref, cw_ref, cb_ref, wd_ref, gf_ref, o_ref,
                h_sc, acc_sc, *, seq_len, final_norm):
    tm = x_ref.shape[0]
    halo = SUBLANES
    ext = tm + 2 * halo
    i, j = pl.program_id(0), pl.program_id(1)

    @pl.when(j == 0)
    def _():
        g = g_ref[...]

        def norm_rows(c, carry):
            rows = pl.ds(pl.multiple_of(c * NORM_ROWS, NORM_ROWS), NORM_ROWS)
            h_sc[rows, :] = _rms(x_ref[rows, :], g).astype(BF16)
            return carry

        lax.fori_loop(0, tm // NORM_ROWS, norm_rows, 0, unroll=16)
        h_sc[tm:tm + halo] = _rms(xn_ref[...], g).astype(BF16)
        h_sc[tm + halo:] = _rms(xp_ref[...], g).astype(BF16)
        acc_sc[...] = jnp.zeros(acc_sc.shape, F32)

    gate = jnp.dot(h_sc[...], wg_ref[...], preferred_element_type=F32)
    up = jnp.dot(h_sc[:tm], wu_ref[...], preferred_element_type=F32)
    left = pltpu.roll(gate, 1, 0)[:tm]
    right = pltpu.roll(gate, ext - 1, 0)[:tm]
    pos = lax.rem(i * tm, seq_len) + lax.broadcasted_iota(jnp.int32, (tm, 1), 0)
    left = jnp.where(pos != 0, left, 0.0)
    right = jnp.where(pos != seq_len - 1, right, 0.0)
    cw = cw_ref[...]
    gc = cb_ref[...] + left * cw[0:1] + gate[:tm] * cw[1:2] + right * cw[2:3]
    act = (gc * (1.0 / (1.0 + jnp.exp(-gc)))) * up
    acc_sc[...] += jnp.dot(act.astype(BF16), wd_ref[...], preferred_element_type=F32)

    @pl.when(j == pl.num_programs(1) - 1)
    def _():
        y = x_ref[...] + acc_sc[...]
        if final_norm:
            y = _rms(y, gf_ref[...])
        o_ref[...] = y


def _ffn(x, g, wg, wu, cw, cb, wd, gf, *, layer, seq_len, tm, tf, final_norm):
    T, D = x.shape
    F = wg.shape[2]
    hb = tm // SUBLANES
    last = T // SUBLANES - 1
    kern = functools.partial(_ffn_kernel, seq_len=seq_len, final_norm=final_norm)
    return pl.pallas_call(
        kern,
        grid=(T // tm, F // tf),
        in_specs=[
            pl.BlockSpec((tm, D), lambda i, j: (i, 0)),
            pl.BlockSpec((SUBLANES, D), lambda i, j: (jnp.maximum(i * hb - 1, 0), 0)),
            pl.BlockSpec((SUBLANES, D), lambda i, j: (jnp.minimum((i + 1) * hb, last), 0)),
            pl.BlockSpec((1, D), lambda i, j: (0, 0)),
            pl.BlockSpec((None, D, tf), lambda i, j: (layer, 0, j)),
            pl.BlockSpec((None, D, tf), lambda i, j: (layer, 0, j)),
            pl.BlockSpec((CONV_WIDTH, tf), lambda i, j: (0, j)),
            pl.BlockSpec((1, tf), lambda i, j: (0, j)),
            pl.BlockSpec((None, tf, D), lambda i, j: (layer, j, 0)),
            pl.BlockSpec((1, D), lambda i, j: (0, 0)),
        ],
        out_specs=pl.BlockSpec((tm, D), lambda i, j: (i, 0)),
        out_shape=jax.ShapeDtypeStruct((T, D), F32),
        scratch_shapes=[pltpu.VMEM((tm + 2 * SUBLANES, D), BF16), pltpu.VMEM((tm, D), F32)],
        compiler_params=_cparams("parallel", "arbitrary"),
        name="ffn",
    )(x, x, x, g, wg, wu, cw, cb, wd, gf)


def _rope_tables(S, theta, n_rot):
    half = n_rot // 2
    lane = np.arange(LANES)
    idx = lane % (LANES // 4)
    rot = (idx < half)[None, :]
    sign = np.where(lane < LANES // 2, -1.0, 1.0).astype(np.float32)[None, :]
    pos = jnp.arange(S, dtype=F32)
    inv = jnp.power(jnp.float32(theta), -jnp.arange(half, dtype=F32) * 2.0 / n_rot)
    ang = pos[:, None] * inv[np.minimum(idx, half - 1)][None, :]
    return jnp.where(rot, jnp.cos(ang), 1.0), jnp.where(rot, jnp.sin(ang) * sign, 0.0)


def _interleave_pairs(w, n_rot):
    d = w.shape[0]
    half = n_rot // 2
    quarter = LANES // 4
    w = w.reshape(d, -1, 2, LANES // 2)
    rest = w[..., n_rot:]
    split = quarter - half
    first = jnp.concatenate([w[..., :half], rest[..., :split]], axis=-1)
    second = jnp.concatenate([w[..., half:n_rot], rest[..., split:]], axis=-1)
    return jnp.stack([first, second], axis=2).reshape(d, -1)


def _pick(n, pref):
    t = min(n, pref)
    while n % t:
        t //= 2
    return t


def _layer_weights(l, p):
    w = p["w_in"][l].astype(BF16)
    d = w.shape[0]
    a_qk = 2 * A_HEADS * 2 * A_DK
    kr0 = ZA_COLS + B_Q_RANK + B_KV_RANK
    c0 = kr0 + B_ROPE
    c_qk = 2 * C_HEADS * C_DK
    w_in = jnp.concatenate([
        _interleave_pairs(w[:, :a_qk], A_ROT), w[:, a_qk:kr0],
        _interleave_pairs(jnp.concatenate([w[:, kr0:c0], jnp.zeros((d, B_ROPE), BF16)], axis=1), B_ROPE),
        _interleave_pairs(w[:, c0:c0 + c_qk], C_DK), w[:, c0 + c_qk:]], axis=1)
    wuq = p["w_uq"][l].astype(BF16).reshape(B_Q_RANK, B_HEADS, B_NOPE + B_ROPE)
    wuq = jnp.concatenate([wuq[:, :, :B_NOPE].reshape(B_Q_RANK, -1),
                           _interleave_pairs(wuq[:, :, B_NOPE:].reshape(B_Q_RANK, -1), B_ROPE)], axis=1)
    wukv = p["w_ukv"][l].astype(BF16).reshape(B_KV_RANK, B_HEADS, B_NOPE + B_DV)
    wukv = jnp.concatenate([wukv[:, :, :B_NOPE].reshape(B_KV_RANK, -1), wukv[:, :, B_NOPE:].reshape(B_KV_RANK, -1)], axis=1)
    lp = p["diff_lambda"][l].astype(F32)
    lam_init = 0.8 - 0.6 * math.exp(-0.3 * l)
    lam = jnp.exp(jnp.sum(lp[0] * lp[1])) - jnp.exp(jnp.sum(lp[2] * lp[3])) + lam_init

    def lane_lg(dec):
        lg = jax.nn.log_sigmoid(dec.astype(F32)).reshape(C_HEADS // 2, 2)
        return jnp.tile(jnp.repeat(lg, C_DK // 2, axis=1), (1, 2)).reshape(C_HEADS // 2, 1, LANES)

    return dict(
        norm_mix=p["norm_mix"][l][None], w_in=w_in,
        lam=lam.reshape(1), lam_init=lam_init, diff_norm=p["diff_norm"][l][:, None],
        mla_q_norm=p["mla_q_norm"][l][None], mla_kv_norm=p["mla_kv_norm"][l][None],
        w_uq=wuq, w_ukv=wukv,
        lgf=lane_lg(p["ret_decay_fwd"][l]), lgb=lane_lg(p["ret_decay_bwd"][l]), ret_norm=p["ret_norm"][l][None],
        norm_ffn=p["norm_ffn"][l][None], conv_w=p["conv_w"][l], conv_b=p["conv_b"][l][None],
    )


def _trunk(x3, layers, stacked, tabs, norm_final):
    B, S, D = x3.shape
    T = B * S
    x = x3.reshape(T, D)
    tabs_a, tabs_b, tabs_c = tabs
    tm_in = _pick(S, 256)
    tq, tk = _pick(S, 512), _pick(S, 2048 if S > 2048 else 1024)
    nq_b = 2 if S % (2 * tq) == 0 else 1
    ts_bwd, ts_fwd = _pick(S, 1024), _pick(S, 512)
    tm_out = _pick(T, 512)
    tm_ffn, tf = _pick(S, 512), _pick(stacked["w_gate"].shape[2], 512)
    no_lam = jnp.zeros((1,), F32)
    no_norm = jnp.ones((B_DV, 1), F32)
    for l, w in enumerate(layers):
        qt, k, vt, qt_b, k_b, vt_b, zc = _in_proj_prep(
            x, w["norm_mix"], w["w_in"], tabs_a, w["mla_q_norm"], w["mla_kv_norm"], w["w_uq"], w["w_ukv"], tabs_b,
            B, S, tm_in)
        oa = _attention(w["lam"], qt, k, vt, w["diff_norm"], n_maps=2, n_q=1, dk=2 * A_DK, dv=A_DV, tq=tq, tk=tk,
                        post_scale=1.0 - w["lam_init"]).reshape(T, -1)
        ob = _attention(no_lam, qt_b.reshape(B, B_HEADS, 1, B_QK, S), k_b, vt_b, no_norm, n_maps=1, n_q=nq_b, dk=B_QK,
                        dv=B_DV, tq=tq, tk=tk, post_scale=1.0).reshape(T, -1)
        oc = _retention(zc, w["lgf"], w["lgb"], w["ret_norm"], tabs_c, B, S, ts_bwd, ts_fwd)
        x = _out_proj(x, oa, ob, oc, stacked["w_o"], l, tm_out)
        x = _ffn(x, w["norm_ffn"], stacked["w_gate"], stacked["w_up"], w["conv_w"], w["conv_b"], stacked["w_down"],
                 norm_final, layer=l, seq_len=S, tm=tm_ffn, tf=tf, final_norm=(l == len(layers) - 1))
    return x.reshape(B, S, D)


def kernel(x_prompt, x_sample, norm_mix, w_in, diff_lambda, diff_norm, mla_q_norm, mla_kv_norm, w_uq, w_ukv,
           ret_decay_fwd, ret_decay_bwd, ret_norm, w_o, norm_ffn, w_gate, w_up, conv_w, conv_b, w_down, norm_final):
    p = dict(norm_mix=norm_mix, w_in=w_in, diff_lambda=diff_lambda, diff_norm=diff_norm, mla_q_norm=mla_q_norm,
             mla_kv_norm=mla_kv_norm, w_uq=w_uq, w_ukv=w_ukv, ret_decay_fwd=ret_decay_fwd,
             ret_decay_bwd=ret_decay_bwd, ret_norm=ret_norm, norm_ffn=norm_ffn, conv_w=conv_w, conv_b=conv_b)
    layers = [_layer_weights(l, p) for l in range(norm_mix.shape[0])]
    stacked = dict(w_o=w_o.astype(BF16), w_gate=w_gate.astype(BF16), w_up=w_up.astype(BF16),
                   w_down=w_down.astype(BF16))
    s_max = max(x_prompt.shape[1], x_sample.shape[1])
    tabs = (_rope_tables(s_max, ROPE_THETA, A_ROT), _rope_tables(s_max, ROPE_THETA, B_ROPE),
            _rope_tables(s_max, RET_THETA, C_DK))
    gf = norm_final[None]
    return (_trunk(x_prompt, layers, stacked, tabs, gf), _trunk(x_sample, layers, stacked, tabs, gf))
```
